```python
import math
import jax
import jax.numpy as jnp
from jax import lax
import numpy as np

D_MODEL = 1024
BATCH = 32
SEQ = 256
DEPTH = 2
DEC_BATCH = 2
DEC_SEQ = 4096
PAST_LEN = 256

GRID_W = 64
ROPE_BASE = 10000.0
NORM_EPS = 1e-6
NEG_INF = -1e30
QBLOCK = 128
N_BRANCH = 4
BRANCH_W = 256
RET_H = 4
RET_DK = 64
RET_DV = 64
RET_CHUNK = 128
WIN_HQ = 4
WIN_HKV = 2
WIN_HD = 64
WINDOW = 128
WIN_BLOCK = 128
DIFF_H = 4
DIFF_D = 32
DIFF_DV = 2 * DIFF_D
MLA_H = 4
MLA_NOPE = 64
MLA_ROPE = 32
MLA_V = 64
MLA_QRANK = 256
MLA_KVRANK = 128
D_FF = 2816
CONV_W = 3
IN_SPLITS = (RET_H * RET_DK, RET_H * RET_DK, RET_H * RET_DV, RET_H * RET_DV, WIN_HQ * WIN_HD, WIN_HKV * WIN_HD, WIN_HKV * WIN_HD, DIFF_H * 2 * DIFF_D, DIFF_H * 2 * DIFF_D, DIFF_H * DIFF_DV, MLA_QRANK, MLA_KVRANK, MLA_ROPE)
IN_WIDTH = sum(IN_SPLITS)

kernel_name = 'hybrid_diffusion_step'


def rmsnorm(x, gain):
    xf = x.astype(jnp.float32)
    y = xf * lax.rsqrt(jnp.mean(xf * xf, axis=-1, keepdims=True) + NORM_EPS)
    return (y * gain.astype(jnp.float32)).astype(x.dtype)


def head_layernorm(x):
    xf = x.astype(jnp.float32)
    mu = jnp.mean(xf, axis=-1, keepdims=True)
    var = jnp.mean(jnp.square(xf - mu), axis=-1, keepdims=True)
    return (xf - mu) * lax.rsqrt(var + NORM_EPS)


def _rope_axis(x, pos):
    half = x.shape[-1] // 2
    freqs = ROPE_BASE ** (-jnp.arange(half, dtype=jnp.float32) / half)
    ang = pos.astype(jnp.float32)[:, None] * freqs[None, :]
    cos = jnp.cos(ang)[None, :, None, :]
    sin = jnp.sin(ang)[None, :, None, :]
    xf = x.astype(jnp.float32)
    x1, x2 = xf[..., :half], xf[..., half:]
    return jnp.concatenate([x1 * cos - x2 * sin, x2 * cos + x1 * sin], axis=-1).astype(x.dtype)


def rope_2d(x):
    n_tok = x.shape[1]
    rows = n_tok // GRID_W
    row = jnp.repeat(jnp.arange(rows), GRID_W)
    col = jnp.tile(jnp.arange(GRID_W), rows)
    r = x.shape[-1] // 2
    return jnp.concatenate([_rope_axis(x[..., :r], row), _rope_axis(x[..., r:], col)], axis=-1)


def map_query_blocks(fn, q):
    b, l = q.shape[:2]
    nb = l // QBLOCK
    qb = jnp.moveaxis(q.reshape((b, nb, QBLOCK) + q.shape[2:]), 1, 0)
    out = jnp.moveaxis(lax.map(fn, qb), 0, 1)
    return out.reshape((b, l) + out.shape[3:])


def modulation(cond, w_mod, b_mod):
    m = jax.nn.silu(cond) @ w_mod + b_mod
    return jnp.split(m[..., None, :], 6, axis=-1)


def retention_scan(q, k, v, log_gamma, s0):
    b, l, h, _ = q.shape
    dv = v.shape[-1]
    n = l // RET_CHUNK

    def chunks(t):
        return t.astype(jnp.float32).reshape(b, n, RET_CHUNK, h, t.shape[-1]).transpose(1, 0, 3, 2, 4)

    idx = jnp.arange(RET_CHUNK, dtype=jnp.float32)
    dist = idx[:, None] - idx[None, :]
    lg = log_gamma.astype(jnp.float32)
    intra = jnp.where(dist >= 0, jnp.exp(lg[:, None, None] * jnp.maximum(dist, 0.0)), 0.0)
    q_dec = jnp.exp(lg[:, None] * (idx + 1.0))[None, :, :, None]
    k_dec = jnp.exp(lg[:, None] * (RET_CHUNK - 1.0 - idx))[None, :, :, None]
    c_dec = jnp.exp(lg * RET_CHUNK)[None, :, None, None]

    def step(s, blk):
        qc, kc, vc = blk
        scores = jnp.einsum('bhqd,bhkd->bhqk', qc, kc) * intra
        o = jnp.einsum('bhqk,bhke->bhqe', scores, vc) + jnp.einsum('bhqd,bhde->bhqe', qc, s) * q_dec
        s = s * c_dec + jnp.einsum('bhkd,bhke->bhde', kc * k_dec, vc)
        return s, o

    s_fin, o = lax.scan(step, s0.astype(jnp.float32), (chunks(q), chunks(k) * (RET_DK ** -0.5), chunks(v)))
    return o.transpose(1, 0, 3, 2, 4).reshape(b, l, h, dv), s_fin


def bidir_retention(q, k, v, g, decay_logit, s0):
    log_gamma = jax.nn.log_sigmoid(decay_logit.astype(jnp.float32))
    o_f, s_f = retention_scan(q, k, v, log_gamma[0], s0[:, 0])
    o_b, s_b = retention_scan(jnp.flip(q, 1), jnp.flip(k, 1), jnp.flip(v, 1), log_gamma[1], s0[:, 1])
    o = head_layernorm(o_f + jnp.flip(o_b, 1))
    b, l = q.shape[:2]
    o = o.reshape(b, l, -1).astype(g.dtype) * jax.nn.silu(g)
    return o, jnp.stack([s_f, s_b], axis=1)


def sink_attention_context(q, k, v, sink):
    b, _, hq, hd = q.shape
    g = k.shape[2]
    r = hq // g
    sink_logit = sink.astype(jnp.float32).reshape(g, r)
    scale = hd ** -0.5

    def block(qb):
        qb = qb.reshape(b, QBLOCK, g, r, hd)
        s = jnp.einsum('bqgrd,bkgd->bgrqk', qb, k).astype(jnp.float32) * scale
        sk = jnp.broadcast_to(sink_logit[None, :, :, None, None], s.shape[:-1] + (1,))
        p = jax.nn.softmax(jnp.concatenate([s, sk], axis=-1), axis=-1)[..., :-1]
        return jnp.einsum('bgrqk,bkgd->bqgrd', p.astype(v.dtype), v).reshape(b, QBLOCK, hq * hd)

    return map_query_blocks(block, q)


def window_attention(q, k, v, kc, vc, sink):
    b, n_tok, hq, hd = q.shape
    g = k.shape[2]
    r = hq // g
    m = kc.shape[1]
    nb = n_tok // WIN_BLOCK
    w = WIN_BLOCK
    qb = q.reshape(b, nb, w, g, r, hd)

    def neighbours(t):
        tp = jnp.pad(t, ((0, 0), (w, w), (0, 0), (0, 0))).reshape(b, nb + 2, w, g, hd)
        return jnp.concatenate([tp[:, :-2], tp[:, 1:-1], tp[:, 2:]], axis=2)

    kn, vn = neighbours(k), neighbours(v)
    blk = jnp.arange(nb)[:, None]
    qpos = blk * w + jnp.arange(w)[None, :]
    kpos = blk * w - w + jnp.arange(3 * w)[None, :]
    mask = (jnp.abs(qpos[:, :, None] - kpos[:, None, :]) <= WINDOW) & ((kpos >= 0) & (kpos < n_tok))[:, None, :]
    scale = hd ** -0.5
    s_lat = jnp.einsum('bnqgrd,bnkgd->bngrqk', qb, kn).astype(jnp.float32) * scale
    s_lat = jnp.where(mask[None, :, None, None], s_lat, NEG_INF)
    s_ctx = jnp.einsum('bnqgrd,bmgd->bngrqm', qb, kc).astype(jnp.float32) * scale
    s_sink = jnp.broadcast_to(sink.astype(jnp.float32).reshape(g, r)[None, None, :, :, None, None], s_lat.shape[:-1] + (1,))
    p = jax.nn.softmax(jnp.concatenate([s_lat, s_ctx, s_sink], axis=-1), axis=-1)
    p_lat = p[..., :3 * w].astype(v.dtype)
    p_ctx = p[..., 3 * w:3 * w + m].astype(v.dtype)
    o = jnp.einsum('bngrqk,bnkgd->bnqgrd', p_lat, vn) + jnp.einsum('bngrqm,bmgd->bnqgrd', p_ctx, vc)
    return o.reshape(b, n_tok, hq * hd)


def diff_attention(q, k, v, lam):
    scale = DIFF_D ** -0.5

    def block(qb):
        s = jnp.einsum('bqhcd,bkhcd->bhcqk', qb, k).astype(jnp.float32) * scale
        p = jax.nn.softmax(s, axis=-1)
        a = p[:, :, 0] - lam * p[:, :, 1]
        return jnp.einsum('bhqk,bkhe->bqhe', a.astype(v.dtype), v)

    return map_query_blocks(block, q)


def mla_attention(q, k_nope, k_rope, v):
    scale = (MLA_NOPE + MLA_ROPE) ** -0.5

    def block(qb):
        s = jnp.einsum('bqhd,bkhd->bhqk', qb[..., :MLA_NOPE], k_nope) + jnp.einsum('bqhd,bkd->bhqk', qb[..., MLA_NOPE:], k_rope)
        p = jax.nn.softmax(s.astype(jnp.float32) * scale, axis=-1)
        return jnp.einsum('bhqk,bkhe->bqhe', p.astype(v.dtype), v)

    return map_query_blocks(block, q)


def trunk_layer(x, cond, p, lam_init, ctx):
    b, l, _ = x.shape
    latent = ctx is not None
    shift1, scale1, gate1, shift2, scale2, gate2 = modulation(cond, p['w_mod'], p['b_mod'])
    h = rmsnorm(x, p['g_mix']) * (1 + scale1) + shift1
    cuts = [int(cc) for cc in np.cumsum(IN_SPLITS)[:-1]]
    (rq, rk, rv, rg, wq, wk, wv, dq, dk, dv, mcq, mckv, mkr) = jnp.split(h @ p['w_in'], cuts, axis=-1)

    rq = rq.reshape(b, l, RET_H, RET_DK)
    rk = rk.reshape(b, l, RET_H, RET_DK)
    rv = rv.reshape(b, l, RET_H, RET_DV)
    s0 = ctx[0] if latent else jnp.zeros((b, 2, RET_H, RET_DK, RET_DV), jnp.float32)
    o_ret, ret_state = bidir_retention(rq, rk, rv, rg, p['ret_decay'], s0)

    wq = wq.reshape(b, l, WIN_HQ, WIN_HD)
    wk = wk.reshape(b, l, WIN_HKV, WIN_HD)
    wv = wv.reshape(b, l, WIN_HKV, WIN_HD)
    if latent:
        o_win = window_attention(rope_2d(wq), rope_2d(wk), wv, ctx[1], ctx[2], p['win_sink'])
    else:
        o_win = sink_attention_context(wq, wk, wv, p['win_sink'])

    dl = p['diff_lambda'].astype(jnp.float32)
    lam = jnp.exp(jnp.sum(dl[0] * dl[1])) - jnp.exp(jnp.sum(dl[2] * dl[3])) + lam_init
    dk_flat = dk.reshape(b, l, DIFF_H, 2 * DIFF_D)
    dv_h = dv.reshape(b, l, DIFF_H, DIFF_DV)
    if latent:
        dq_h = rope_2d(dq.reshape(b, l, 2 * DIFF_H, DIFF_D)).reshape(b, l, DIFF_H, 2, DIFF_D)
        dk_lat = rope_2d(dk.reshape(b, l, 2 * DIFF_H, DIFF_D)).reshape(b, l, DIFF_H, 2, DIFF_D)
        dk_all = jnp.concatenate([dk_lat, ctx[3].reshape(b, -1, DIFF_H, 2, DIFF_D)], axis=1)
        dv_all = jnp.concatenate([dv_h, ctx[4]], axis=1)
    else:
        dq_h = dq.reshape(b, l, DIFF_H, 2, DIFF_D)
        dk_all = dk.reshape(b, l, DIFF_H, 2, DIFF_D)
        dv_all = dv_h
    o_diff = diff_attention(dq_h, dk_all, dv_all, lam)
    o_diff = (rmsnorm(o_diff, p['g_diff']) * (1.0 - lam_init)).reshape(b, l, DIFF_H * DIFF_DV)

    mq = (rmsnorm(mcq, p['g_mla_q']) @ p['w_mla_uq']).reshape(b, l, MLA_H, MLA_NOPE + MLA_ROPE)
    ckv = rmsnorm(mckv, p['g_mla_kv'])
    kv = (ckv @ p['w_mla_ukv']).reshape(b, l, MLA_H, MLA_NOPE + MLA_V)
    if latent:
        mq = jnp.concatenate([mq[..., :MLA_NOPE], rope_2d(mq[..., MLA_NOPE:])], axis=-1)
        kr_lat = rope_2d(mkr[:, :, None, :])[:, :, 0]
        kv_ctx = (ctx[5] @ p['w_mla_ukv']).reshape(b, -1, MLA_H, MLA_NOPE + MLA_V)
        kv_all = jnp.concatenate([kv, kv_ctx], axis=1)
        kr_all = jnp.concatenate([kr_lat, ctx[6]], axis=1)
    else:
        kv_all = kv
        kr_all = mkr
    o_mla = mla_attention(mq, kv_all[..., :MLA_NOPE], kr_all, kv_all[..., MLA_NOPE:]).reshape(b, l, MLA_H * MLA_V)

    branches = jnp.stack([o_ret, o_win, o_diff, o_mla], axis=2)
    branches = jnp.einsum('blnc,ncd->blnd', branches, p['w_branch'])
    gates = jax.nn.sigmoid((h @ p['w_gate'] + p['b_gate']).reshape(b, l, N_BRANCH, D_MODEL))
    mixed = jnp.sum(gates * branches, axis=2) @ p['w_out']
    x = x + gate1 * mixed

    h2 = rmsnorm(x, p['g_ffn']) * (1 + scale2) + shift2
    u = h2 @ p['w_up'] + p['b_up']
    pad = CONV_W // 2
    up = jnp.pad(u, ((0, 0), (pad, pad), (0, 0)))
    conv = p['b_conv']
    for j in range(CONV_W):
        conv = conv + p['w_conv'][j] * up[:, j:j + l]
    a, gg = jnp.split(conv, 2, axis=-1)
    x = x + gate2 * ((jax.nn.silu(gg) * a) @ p['w_down'])

    if latent:
        return x, None
    return x, (ret_state, wk, wv, dk_flat, dv_h, ckv, mkr)


def setup_inputs(seed: int = 0) -> dict:
    key = jax.random.key(seed)
    ks = iter(jax.random.split(key, 48))

    def nrm(shape, scale):
        return jax.random.normal(next(ks), shape, jnp.float32) * scale

    def gain(shape):
        return 1.0 + nrm(shape, 0.1)

    gamma = 1.0 - 2.0 ** (-5.0 - np.arange(RET_H))
    decay_base = jnp.asarray(np.log(gamma / (1.0 - gamma)).astype(np.float32))
    d = D_MODEL
    return {
        'x_prompt': nrm((BATCH, SEQ, d), 1.0),
        'x_sample': nrm((DEC_BATCH, DEC_SEQ, d), 1.0),
        'state_ret': nrm((DEC_BATCH, DEPTH, 2, RET_H, RET_DK, RET_DV), 1.0),
        'cache_win_k': nrm((DEC_BATCH, DEPTH, PAST_LEN, WIN_HKV, WIN_HD), 1.0),
        'cache_win_v': nrm((DEC_BATCH, DEPTH, PAST_LEN, WIN_HKV, WIN_HD), 1.0),
        'cache_diff_k': nrm((DEC_BATCH, DEPTH, PAST_LEN, DIFF_H, 2 * DIFF_D), 1.0),
        'cache_diff_v': nrm((DEC_BATCH, DEPTH, PAST_LEN, DIFF_H, DIFF_DV), 1.0),
        'cache_mla_ckv': nrm((DEC_BATCH, DEPTH, PAST_LEN, MLA_KVRANK), 1.0),
        'cache_mla_krope': nrm((DEC_BATCH, DEPTH, PAST_LEN, MLA_ROPE), 1.0),
        'c': nrm((DEC_BATCH, d), 1.0),
        'c_ctx': nrm((d,), 1.0),
        'w_mod': nrm((DEPTH, d, 6 * d), 0.5 * d ** -0.5),
        'b_mod': nrm((DEPTH, 6 * d), 0.02),
        'g_mix': gain((DEPTH, d)),
        'w_in': nrm((DEPTH, d, IN_WIDTH), d ** -0.5),
        'ret_decay': decay_base[None, None, :] + nrm((DEPTH, 2, RET_H), 0.1),
        'win_sink': nrm((DEPTH, WIN_HQ), 0.5),
        'diff_lambda': nrm((DEPTH, 4, DIFF_D), 0.1),
        'g_diff': gain((DEPTH, DIFF_DV)),
        'g_mla_q': gain((DEPTH, MLA_QRANK)),
        'w_mla_uq': nrm((DEPTH, MLA_QRANK, MLA_H * (MLA_NOPE + MLA_ROPE)), MLA_QRANK ** -0.5),
        'g_mla_kv': gain((DEPTH, MLA_KVRANK)),
        'w_mla_ukv': nrm((DEPTH, MLA_KVRANK, MLA_H * (MLA_NOPE + MLA_V)), MLA_KVRANK ** -0.5),
        'w_branch': nrm((DEPTH, N_BRANCH, BRANCH_W, d), BRANCH_W ** -0.5),
        'w_gate': nrm((DEPTH, d, N_BRANCH * d), d ** -0.5),
        'b_gate': nrm((DEPTH, N_BRANCH * d), 0.02),
        'w_out': nrm((DEPTH, d, d), d ** -0.5),
        'g_ffn': gain((DEPTH, d)),
        'w_up': nrm((DEPTH, d, 2 * D_FF), d ** -0.5),
        'b_up': nrm((DEPTH, 2 * D_FF), 0.02),
        'w_conv': nrm((DEPTH, CONV_W, 2 * D_FF), CONV_W ** -0.5),
        'b_conv': nrm((DEPTH, 2 * D_FF), 0.02),
        'w_down': nrm((DEPTH, D_FF, d), D_FF ** -0.5),
        'g_final': gain((d,)),
    }


def reference(x_prompt, x_sample, state_ret, cache_win_k, cache_win_v, cache_diff_k, cache_diff_v, cache_mla_ckv, cache_mla_krope, c, c_ctx, w_mod, b_mod, g_mix, w_in, ret_decay, win_sink, diff_lambda, g_diff, g_mla_q, w_mla_uq, g_mla_kv, w_mla_ukv, w_branch, w_gate, b_gate, w_out, g_ffn, w_up, b_up, w_conv, b_conv, w_down, g_final):
    xp = x_prompt
    xs = x_sample
    produced = [[] for _ in range(7)]
    for li in range(DEPTH):
        p = {'w_mod': w_mod[li], 'b_mod': b_mod[li], 'g_mix': g_mix[li], 'w_in': w_in[li],
             'ret_decay': ret_decay[li], 'win_sink': win_sink[li], 'diff_lambda': diff_lambda[li],
             'g_diff': g_diff[li], 'g_mla_q': g_mla_q[li], 'w_mla_uq': w_mla_uq[li],
             'g_mla_kv': g_mla_kv[li], 'w_mla_ukv': w_mla_ukv[li], 'w_branch': w_branch[li],
             'w_gate': w_gate[li], 'b_gate': b_gate[li], 'w_out': w_out[li], 'g_ffn': g_ffn[li],
             'w_up': w_up[li], 'b_up': b_up[li], 'w_conv': w_conv[li], 'b_conv': b_conv[li],
             'w_down': w_down[li]}
        lam_init = 0.8 - 0.6 * math.exp(-0.3 * li)
        xp, ctx_new = trunk_layer(xp, c_ctx, p, lam_init, None)
        for lst, t in zip(produced, ctx_new):
            lst.append(t)
        ctx_cached = (state_ret[:, li], cache_win_k[:, li], cache_win_v[:, li], cache_diff_k[:, li],
                      cache_diff_v[:, li], cache_mla_ckv[:, li], cache_mla_krope[:, li])
        xs, _ = trunk_layer(xs, c, p, lam_init, ctx_cached)
    y_prompt = rmsnorm(xp, g_final)
    y_sample = rmsnorm(xs, g_final)
    new_state_ret = jnp.stack(produced[0], axis=1)
    new_cache_win_k = jnp.stack(produced[1], axis=1)
    new_cache_win_v = jnp.stack(produced[2], axis=1)
    new_cache_diff_k = jnp.stack(produced[3], axis=1)
    new_cache_diff_v = jnp.stack(produced[4], axis=1)
    new_cache_mla_ckv = jnp.stack(produced[5], axis=1)
    new_cache_mla_krope = jnp.stack(produced[6], axis=1)
    return (y_prompt, y_sample, new_state_ret, new_cache_win_k, new_cache_win_v, new_cache_diff_k, new_cache_diff_v, new_cache_mla_ckv, new_cache_mla_krope)
```

```python
import functools
import math

import numpy as np
import jax
import jax.numpy as jnp
from jax import lax
from jax.experimental import pallas as pl
from jax.experimental.pallas import tpu as pltpu

F32 = jnp.float32
BF16 = jnp.bfloat16

D_MODEL = 1024
DEPTH = 2
GRID_W = 64
ROPE_BASE = 10000.0
NORM_EPS = 1e-6
NEG_INF = -1e30
RET_H = 4
RET_DK = 64
RET_CHUNK = 128
WIN_HD = 64
WINDOW = 128
DIFF_H = 4
DIFF_D = 32
MLA_H = 4
MLA_NOPE = 64
MLA_ROPE = 32
MLA_V = 64
MLA_QRANK = 256
MLA_KVRANK = 128
D_FF = 2816
IN_SPLITS = (256, 256, 256, 256, 256, 128, 128, 256, 256, 256, 256, 128, 32)

LANES = 128
BF16_ROWS = 16
BRANCH_W = 256
HEAD_W = 64

_C_RET, _C_WQ, _C_WK, _C_WV, _C_DQ, _C_DK, _C_DV, _C_MCQ, _C_MCKV, _C_MKR, _C_END = (
    0, 1024, 1280, 1536, 1792, 2048, 2304, 2560, 2816, 2944, 3072)

TM_DENSE = 256
FF_CHUNK = 256
TQ_ATTN = 256
KC_ATTN = 256
CONV_HALO = BF16_ROWS
VMEM_LIMIT = 48 * 1024 * 1024


def _cparams(sem):
    return pltpu.CompilerParams(dimension_semantics=sem, vmem_limit_bytes=VMEM_LIMIT)


def _dot(a, b):
    return jnp.dot(a, b, preferred_element_type=F32)


def _dot_nt(a, b):
    return lax.dot_general(a, b, (((1,), (1,)), ((), ())), preferred_element_type=F32)


def _sigmoid(x):
    return 1.0 / (1.0 + jnp.exp(-x))


def _rmsnorm(x, gain):
    ms = jnp.mean(x * x, axis=-1, keepdims=True)
    return x * lax.rsqrt(ms + NORM_EPS) * gain


def _mod_spec(mod, tiles_per_seq):
    blk = (1,) + mod.shape[1:]
    if mod.shape[0] == 1:
        return pl.BlockSpec(blk, lambda i: (0, 0, 0))
    return pl.BlockSpec(blk, lambda i: (i // tiles_per_seq, 0, 0))


def _lane_group_mask(shape, group_width, group):
    lane = lax.broadcasted_iota(jnp.int32, shape, len(shape) - 1)
    return (lane // group_width) == group


def _rope_rows(x, cos, sin_up, sin_dn, half):
    width = x.shape[-1]
    up = pltpu.roll(x, width - half, axis=1)
    dn = pltpu.roll(x, half, axis=1)
    return x * cos + up * sin_up + dn * sin_dn


def _mod_kernel(c_ref, w_ref, b_ref, o_ref):
    c = c_ref[...]
    s = c * _sigmoid(c)
    o_ref[0] = _dot(s.astype(BF16), w_ref[0].astype(BF16)) + b_ref[0]


def _modulation(cond_rows, w_mod, b_mod):
    depth, d, n = w_mod.shape
    tn = 1024
    return pl.pallas_call(
        _mod_kernel,
        grid=(depth, n // tn),
        in_specs=[
            pl.BlockSpec((8, d), lambda l, j: (0, 0)),
            pl.BlockSpec((1, d, tn), lambda l, j: (l, 0, j)),
            pl.BlockSpec((1, 1, tn), lambda l, j: (l, 0, j)),
        ],
        out_specs=pl.BlockSpec((1, 8, tn), lambda l, j: (l, 0, j)),
        out_shape=jax.ShapeDtypeStruct((depth, 8, n), F32),
        compiler_params=_cparams(("arbitrary", "arbitrary")),
        name="modulation",
    )(cond_rows, w_mod, b_mod.reshape(depth, 1, n))


_ROPE_HALF = (16, 8, 8)


def _pre_kernel(*refs, rope, ctx):
    it = iter(refs)
    x_ref, mod_ref, g_ref, w_ref, wuq_ref, wukv_ref, gq_ref, gkv_ref = (next(it) for _ in range(8))
    tab_ref = next(it) if rope else None
    (ret_ref, wq_ref, wke_ref, wve_ref, dq_ref, dk_ref, dv_ref,
     mq_ref, mk_ref, mv_ref) = (next(it) for _ in range(10))
    if ctx:
        wke32_ref, wve32_ref, dk32_ref, dv32_ref, ckv32_ref, mkr32_ref = (next(it) for _ in range(6))

    x = x_ref[...]
    mod = mod_ref[0]
    h = _rmsnorm(x, g_ref[...]) * (1.0 + mod[1:2]) + mod[0:1]
    hb = h.astype(BF16)

    def proj(a, b):
        return _dot(hb, w_ref[:, a:b])

    def rot(v, t):
        if not rope:
            return v
        return _rope_rows(v, tab_ref[3 * t], tab_ref[3 * t + 1], tab_ref[3 * t + 2], _ROPE_HALF[t])

    def store(o_ref, val, t=None, o32_ref=None):
        for j in range(val.shape[1] // LANES):
            v = val[:, LANES * j:LANES * (j + 1)]
            if o32_ref is not None:
                o32_ref[:, LANES * j:LANES * (j + 1)] = v
            if t is not None:
                v = rot(v, t)
            o_ref[:, LANES * j:LANES * (j + 1)] = v.astype(o_ref.dtype)

    ret_ref[...] = proj(_C_RET, _C_WQ)
    store(wq_ref, proj(_C_WQ, _C_WK), 0)
    store(wke_ref, proj(_C_WK, _C_WV), 0, wke32_ref if ctx else None)
    store(wve_ref, proj(_C_WV, _C_DQ), None, wve32_ref if ctx else None)
    store(dq_ref, proj(_C_DQ, _C_DK), 1)
    store(dk_ref, proj(_C_DK, _C_DV), 1, dk32_ref if ctx else None)
    store(dv_ref, proj(_C_DV, _C_MCQ), None, dv32_ref if ctx else None)

    cq = _rmsnorm(proj(_C_MCQ, _C_MCKV), gq_ref[...])
    store(mq_ref, _dot(cq.astype(BF16), wuq_ref[...]), 2)
    ckv = _rmsnorm(proj(_C_MCKV, _C_MKR), gkv_ref[...])
    kvp = _dot(ckv.astype(BF16), wukv_ref[...])
    mkr = proj(_C_MKR, _C_END)
    mkr_rot = rot(mkr, 2)
    for hd in range(MLA_H):
        mk_ref[:, LANES * hd:LANES * (hd + 1)] = (kvp[:, LANES * hd:LANES * (hd + 1)] + mkr_rot).astype(BF16)
    mv_ref[...] = kvp[:, MLA_H * LANES:].astype(BF16)
    if ctx:
        ckv32_ref[...] = ckv
        mkr32_ref[...] = mkr


def _pre(x2d, mod, g_mix, w_in_p, w_uq_p, w_ukv_p, g_q, g_kv, tabs, *, seq_len, ctx):
    t, d = x2d.shape
    tm = TM_DENSE
    tiles_per_seq = seq_len // tm
    rope = tabs is not None
    const = lambda i: (0, 0)
    row = lambda i: (i, 0)
    in_specs = [
        pl.BlockSpec((tm, d), row),
        _mod_spec(mod, tiles_per_seq),
        pl.BlockSpec((1, d), const),
        pl.BlockSpec(w_in_p.shape, const),
        pl.BlockSpec(w_uq_p.shape, const),
        pl.BlockSpec(w_ukv_p.shape, const),
        pl.BlockSpec((1, MLA_QRANK), const),
        pl.BlockSpec((1, MLA_KVRANK), const),
    ]
    args = [x2d, mod, g_mix.reshape(1, d), w_in_p, w_uq_p, w_ukv_p,
            g_q.reshape(1, MLA_QRANK), g_kv.reshape(1, MLA_KVRANK)]
    if rope:
        in_specs.append(pl.BlockSpec((9, tm, LANES), lambda i: (0, i % tiles_per_seq, 0)))
        args.append(tabs)
    widths = [(1024, F32)] + [(256, BF16)] * 6 + [(512, BF16), (512, BF16), (256, BF16)]
    if ctx:
        widths += [(256, F32)] * 4 + [(128, F32)] * 2
    out_specs = [pl.BlockSpec((tm, w), row) for w, _ in widths]
    out_shape = [jax.ShapeDtypeStruct((t, w), dt) for w, dt in widths]
    return pl.pallas_call(
        functools.partial(_pre_kernel, rope=rope, ctx=ctx),
        grid=(t // tm,),
        in_specs=in_specs,
        out_specs=out_specs,
        out_shape=out_shape,
        compiler_params=_cparams(("arbitrary",)),
        name="pre_ctx" if ctx else "pre_lat",
    )(*args)


def _log_sigmoid(z):
    return jnp.minimum(z, 0.0) - jnp.log(1.0 + jnp.exp(-jnp.abs(z)))


def _ret_kernel(r_ref, s0_ref, dl_ref, ds_ref, dh_ref, o_ref, st_ref,
                s_sc, ob_sc, qdec_sc, kdec_sc, cdec_sc, intra_sc, *, n_chunks):
    p = pl.program_id(1)
    c = pl.program_id(2)
    fwd = p == 1
    chunk = RET_CHUNK
    w = BRANCH_W

    @pl.when(c == 0)
    def _init():
        s_sc[...] = s0_ref[0, 0]
        lgl = _log_sigmoid(dl_ref[0])[0:1, :]
        i = lax.broadcasted_iota(jnp.int32, (chunk, w), 0).astype(F32)
        qe = jnp.where(fwd, i + 1.0, chunk - i)
        ke = jnp.where(fwd, chunk - 1.0 - i, i)
        qdec_sc[...] = jnp.exp(lgl * qe)
        kdec_sc[...] = jnp.exp(lgl * ke)
        rr = lax.broadcasted_iota(jnp.int32, (w, w), 0) // HEAD_W
        cc = lax.broadcasted_iota(jnp.int32, (w, w), 1) // HEAD_W
        cdec_sc[...] = jnp.where(rr == cc, jnp.exp(_log_sigmoid(ds_ref[0]) * float(chunk)), 0.0)
        ii = lax.broadcasted_iota(jnp.int32, (chunk, chunk), 0)
        jj = lax.broadcasted_iota(jnp.int32, (chunk, chunk), 1)
        dist = jnp.where(fwd, ii - jj, jj - ii)
        for hd in range(RET_H):
            lgh = _log_sigmoid(dh_ref[0, hd])[0:1, :]
            intra_sc[hd] = jnp.where(dist >= 0, jnp.exp(lgh * jnp.maximum(dist, 0).astype(F32)), 0.0)

    blk = r_ref[...]
    q = blk[:, 0:w]
    k = blk[:, w:2 * w] * (RET_DK ** -0.5)
    v = blk[:, 2 * w:3 * w]
    qb = q.astype(BF16)
    kb = k.astype(BF16)
    vb = v.astype(BF16)
    s = s_sc[...]
    o = _dot(qb, s.astype(BF16)) * qdec_sc[...]
    for hd in range(RET_H):
        mh = _lane_group_mask((chunk, w), HEAD_W, hd)
        sc = _dot_nt(jnp.where(mh, qb, jnp.zeros_like(qb)), kb) * intra_sc[hd]
        o = o + jnp.where(mh, _dot(sc.astype(BF16), vb), 0.0)
    kd_t = jnp.transpose(k * kdec_sc[...]).astype(BF16)
    rr = lax.broadcasted_iota(jnp.int32, (w, w), 0) // HEAD_W
    cc = lax.broadcasted_iota(jnp.int32, (w, w), 1) // HEAD_W
    s_new = s * cdec_sc[...] + jnp.where(rr == cc, _dot(kd_t, vb), 0.0)
    s_sc[...] = s_new

    @pl.when(p == 0)
    def _bwd():
        ob_sc[n_chunks - 1 - c] = o

    @pl.when(p == 1)
    def _fwd():
        tot = o + ob_sc[c]
        mu = jnp.zeros_like(tot)
        for hd in range(RET_H):
            mh = _lane_group_mask((chunk, w), HEAD_W, hd)
            m1 = jnp.sum(jnp.where(mh, tot, 0.0), axis=-1, keepdims=True) * (1.0 / HEAD_W)
            mu = jnp.where(mh, m1, mu)
        xc = tot - mu
        var = jnp.zeros_like(tot)
        for hd in range(RET_H):
            mh = _lane_group_mask((chunk, w), HEAD_W, hd)
            v1 = jnp.sum(jnp.where(mh, xc * xc, 0.0), axis=-1, keepdims=True) * (1.0 / HEAD_W)
            var = jnp.where(mh, v1, var)
        g = blk[:, 3 * w:4 * w]
        o_ref[...] = (xc * lax.rsqrt(var + NORM_EPS) * (g * _sigmoid(g))).astype(o_ref.dtype)

    @pl.when(c == n_chunks - 1)
    def _state():
        for hd in range(RET_H):
            st_ref[0, 0, hd] = s_new[HEAD_W * hd:HEAD_W * (hd + 1), HEAD_W * hd:HEAD_W * (hd + 1)]


def _retention(ret4, s0_bd, decay, *, batch, seq_len):
    n = seq_len // RET_CHUNK
    w = BRANCH_W
    dlane = jnp.broadcast_to(jnp.repeat(decay, HEAD_W, axis=1)[:, None, :], (2, 8, w))
    dsub = jnp.broadcast_to(jnp.repeat(decay, HEAD_W, axis=1)[:, :, None], (2, w, w))
    dhead = jnp.broadcast_to(decay[:, :, None, None], (2, RET_H, 8, LANES))
    chunk_of = lambda b, p, c: b * n + c * p + (n - 1 - c) * (1 - p)
    o, st = pl.pallas_call(
        functools.partial(_ret_kernel, n_chunks=n),
        grid=(batch, 2, n),
        in_specs=[
            pl.BlockSpec((RET_CHUNK, 4 * w), lambda b, p, c: (chunk_of(b, p, c), 0)),
            pl.BlockSpec((1, 1, w, w), lambda b, p, c: (b, 1 - p, 0, 0)),
            pl.BlockSpec((1, 8, w), lambda b, p, c: (1 - p, 0, 0)),
            pl.BlockSpec((1, w, w), lambda b, p, c: (1 - p, 0, 0)),
            pl.BlockSpec((1, RET_H, 8, LANES), lambda b, p, c: (1 - p, 0, 0, 0)),
        ],
        out_specs=[
            pl.BlockSpec((RET_CHUNK, w), lambda b, p, c: (b * n + c * p, 0)),
            pl.BlockSpec((1, 1, RET_H, RET_DK, RET_DK), lambda b, p, c: (b, 1 - p, 0, 0, 0)),
        ],
        out_shape=[
            jax.ShapeDtypeStruct((batch * seq_len, w), BF16),
            jax.ShapeDtypeStruct((batch, 2, RET_H, RET_DK, RET_DK), F32),
        ],
        scratch_shapes=[
            pltpu.VMEM((w, w), F32),
            pltpu.VMEM((n, RET_CHUNK, w), F32),
            pltpu.VMEM((RET_CHUNK, w), F32),
            pltpu.VMEM((RET_CHUNK, w), F32),
            pltpu.VMEM((w, w), F32),
            pltpu.VMEM((RET_H, RET_CHUNK, RET_CHUNK), F32),
        ],
        compiler_params=_cparams(("arbitrary", "arbitrary", "arbitrary")),
        name="retention",
    )(ret4, s0_bd, dlane, dsub, dhead)
    return o, st


def _accum_lanes(acc, x, op):
    for j in range(x.shape[1] // LANES):
        acc = op(acc, x[:, LANES * j:LANES * (j + 1)])
    return acc


def _attn_kernel(*refs, kind, n_chunks, has_ctx, lam_init):
    it = iter(refs)
    q_ref, k_ref, v_ref = next(it), next(it), next(it)
    if has_ctx:
        if kind == "mla":
            cckv_ref, ckr_ref, wukv_ref = next(it), next(it), next(it)
        else:
            kx_ref, vx_ref = next(it), next(it)
    if kind == "sink":
        sink_ref = next(it)
    if kind == "diff":
        lam_ref, gd_ref = next(it), next(it)
    o_ref = next(it)
    s_sc, m_sc, l_sc, acc_sc = next(it), next(it), next(it), next(it)
    if has_ctx and kind == "mla":
        kx_sc, vx_sc = next(it), next(it)

    tq = q_ref.shape[1]
    kc = KC_ATTN
    w = BRANCH_W
    q = q_ref[0]
    n_all = n_chunks + (1 if has_ctx else 0)

    if has_ctx and kind == "mla":
        kvc = _dot(cckv_ref[0].astype(BF16), wukv_ref[...])
        kr = ckr_ref[0]
        for hd in range(MLA_H):
            kx_sc[:, LANES * hd:LANES * (hd + 1)] = (kvc[:, LANES * hd:LANES * (hd + 1)] + kr).astype(BF16)
        vx_sc[...] = kvc[:, MLA_H * LANES:].astype(BF16)

    def k_chunk(ch, lo, hi):
        return k_ref[0, pl.ds(pl.multiple_of(ch * kc, kc), kc), lo:hi]

    def v_chunk(ch):
        return v_ref[0, pl.ds(pl.multiple_of(ch * kc, kc), kc), :]

    def kx_tile(lo, hi):
        if kind == "mla":
            return kx_sc[:, lo:hi]
        return kx_ref[0][:, lo:hi]

    def vx_tile():
        if kind == "mla":
            return vx_sc[...]
        return vx_ref[0]

    def score_pass(qm, lo, hi, slot):
        m_sc[...] = jnp.full((tq, LANES), -jnp.inf, F32)

        def body(ch, carry):
            s = _dot_nt(qm, k_chunk(ch, lo, hi))
            s_sc[slot, ch] = s
            m_sc[...] = _accum_lanes(m_sc[...], s, jnp.maximum)
            return carry

        lax.fori_loop(0, n_chunks, body, 0)
        if has_ctx:
            s = _dot_nt(qm, kx_tile(lo, hi))
            s_sc[slot, n_chunks] = s
            m_sc[...] = _accum_lanes(m_sc[...], s, jnp.maximum)
        return jnp.max(m_sc[...], axis=-1, keepdims=True)

    def exp_pass(slot, m, scale):
        l_sc[...] = jnp.zeros((tq, LANES), F32)

        def body(ch, carry):
            pr = jnp.exp((s_sc[slot, ch] - m) * scale)
            s_sc[slot, ch] = pr
            l_sc[...] = _accum_lanes(l_sc[...], pr, jnp.add)
            return carry

        lax.fori_loop(0, n_all, body, 0)
        return jnp.sum(l_sc[...], axis=-1, keepdims=True)

    def pv_pass(weight_of):
        acc_sc[...] = jnp.zeros((tq, w), F32)

        def body(ch, carry):
            acc_sc[...] += _dot(weight_of(ch).astype(BF16), v_chunk(ch))
            return carry

        lax.fori_loop(0, n_chunks, body, 0)
        if has_ctx:
            acc_sc[...] += _dot(weight_of(n_chunks).astype(BF16), vx_tile())
        return acc_sc[...]

    out = jnp.zeros((tq, w), F32)
    if kind == "sink":
        scale = WIN_HD ** -0.5
        for hd in range(4):
            mh = _lane_group_mask((tq, w), HEAD_W, hd)
            qm = jnp.where(mh, q, jnp.zeros_like(q))
            sink = sink_ref[hd:hd + 1, 0:1]
            m = jnp.maximum(score_pass(qm, 0, w, 0) * scale, sink)
            l = exp_pass(0, m * (1.0 / scale), scale) + jnp.exp(sink - m)
            pv = pv_pass(lambda ch: s_sc[0, ch])
            out = out + jnp.where(mh, pv * (1.0 / l), 0.0)
        o_ref[0] = out.astype(o_ref.dtype)
    elif kind == "mla":
        scale = (MLA_NOPE + MLA_ROPE) ** -0.5
        for hd in range(MLA_H):
            mh = _lane_group_mask((tq, w), HEAD_W, hd)
            qh = q[:, LANES * hd:LANES * (hd + 1)]
            m = score_pass(qh, LANES * hd, LANES * (hd + 1), 0)
            l = exp_pass(0, m, scale)
            pv = pv_pass(lambda ch: s_sc[0, ch])
            out = out + jnp.where(mh, pv * (1.0 / l), 0.0)
        o_ref[0] = out.astype(o_ref.dtype)
    else:
        scale = DIFF_D ** -0.5
        dl = lam_ref[...]
        lam = (jnp.exp(jnp.sum(dl[0:1] * dl[1:2], axis=-1, keepdims=True))
               - jnp.exp(jnp.sum(dl[2:3] * dl[3:4], axis=-1, keepdims=True)) + lam_init)
        for hd in range(DIFF_H):
            mh = _lane_group_mask((tq, w), HEAD_W, hd)
            r = []
            for comp in range(2):
                mg = _lane_group_mask((tq, w), DIFF_D, 2 * hd + comp)
                qm = jnp.where(mg, q, jnp.zeros_like(q))
                m = score_pass(qm, 0, w, comp)
                l = exp_pass(comp, m, scale)
                r.append(1.0 / l)
            r1 = lam * r[1]
            pv = pv_pass(lambda ch: s_sc[0, ch] * r[0] - s_sc[1, ch] * r1)
            out = out + jnp.where(mh, pv, 0.0)
        ms = jnp.zeros_like(out)
        for hd in range(DIFF_H):
            mh = _lane_group_mask((tq, w), HEAD_W, hd)
            m1 = jnp.sum(jnp.where(mh, out * out, 0.0), axis=-1, keepdims=True) * (1.0 / HEAD_W)
            ms = jnp.where(mh, m1, ms)
        y = out * lax.rsqrt(ms + NORM_EPS) * gd_ref[...]
        o_ref[0] = (y * (1.0 - lam_init)).astype(o_ref.dtype)


def _attention(kind, q, k, v, *, ctx_args=(), params=(), lam_init=0.0):
    b, n, wq = q.shape
    nk = k.shape[1]
    tq = min(TQ_ATTN, n)
    n_chunks = nk // KC_ATTN
    has_ctx = len(ctx_args) > 0
    w = BRANCH_W
    per_b = lambda bb, i: (bb, 0, 0)
    in_specs = [
        pl.BlockSpec((1, tq, wq), lambda bb, i: (bb, i, 0)),
        pl.BlockSpec((1, nk, k.shape[2]), per_b),
        pl.BlockSpec((1, nk, w), per_b),
    ]
    args = [q, k, v]
    for a in ctx_args:
        if a.ndim == 3:
            in_specs.append(pl.BlockSpec((1,) + a.shape[1:], per_b))
        else:
            in_specs.append(pl.BlockSpec(a.shape, lambda bb, i: (0, 0)))
        args.append(a)
    for a in params:
        in_specs.append(pl.BlockSpec(a.shape, lambda bb, i: (0, 0)))
        args.append(a)
    n_all = n_chunks + (1 if has_ctx else 0)
    scratch = [
        pltpu.VMEM((2 if kind == "diff" else 1, n_all, tq, KC_ATTN), F32),
        pltpu.VMEM((tq, LANES), F32),
        pltpu.VMEM((tq, LANES), F32),
        pltpu.VMEM((tq, w), F32),
    ]
    if has_ctx and kind == "mla":
        m_ctx = ctx_args[0].shape[1]
        scratch += [pltpu.VMEM((m_ctx, MLA_H * LANES), BF16), pltpu.VMEM((m_ctx, w), BF16)]
    return pl.pallas_call(
        functools.partial(_attn_kernel, kind=kind, n_chunks=n_chunks, has_ctx=has_ctx, lam_init=lam_init),
        grid=(b, n // tq),
        in_specs=in_specs,
        out_specs=pl.BlockSpec((1, tq, w), lambda bb, i: (bb, i, 0)),
        out_shape=jax.ShapeDtypeStruct((b, n, w), BF16),
        scratch_shapes=scratch,
        compiler_params=_cparams(("arbitrary", "arbitrary")),
        name="attn_" + kind + ("_lat" if has_ctx else "_ctx"),
    )(*args)


def _win_kernel(q_ref, kp_ref, kc_ref, kn_ref, vp_ref, vc_ref, vn_ref, kx_ref, vx_ref, sink_ref, o_ref):
    nb = pl.num_programs(1)
    n = pl.program_id(1)
    tq = q_ref.shape[1]
    w = BRANCH_W
    scale = WIN_HD ** -0.5
    q = q_ref[0]
    ii = lax.broadcasted_iota(jnp.int32, (tq, tq), 0)
    jj = lax.broadcasted_iota(jnp.int32, (tq, tq), 1)
    mask_prev = (jj >= ii) & (n > 0)
    mask_next = (jj <= ii) & (n < nb - 1)
    out = jnp.zeros((tq, w), F32)
    for hd in range(4):
        mh = _lane_group_mask((tq, w), HEAD_W, hd)
        qm = jnp.where(mh, q, jnp.zeros_like(q))
        sp = jnp.where(mask_prev, _dot_nt(qm, kp_ref[0]) * scale, NEG_INF)
        sc = _dot_nt(qm, kc_ref[0]) * scale
        sn = jnp.where(mask_next, _dot_nt(qm, kn_ref[0]) * scale, NEG_INF)
        sx = _dot_nt(qm, kx_ref[0]) * scale
        sink = sink_ref[hd:hd + 1, 0:1]
        m = jnp.maximum(jnp.maximum(jnp.max(sp, axis=-1, keepdims=True), jnp.max(sc, axis=-1, keepdims=True)),
                        jnp.maximum(jnp.max(sn, axis=-1, keepdims=True), jnp.max(sx, axis=-1, keepdims=True)))
        m = jnp.maximum(m, sink)
        pp, pc, pn, px = (jnp.exp(s - m) for s in (sp, sc, sn, sx))
        l = (jnp.sum(pp, axis=-1, keepdims=True) + jnp.sum(pc, axis=-1, keepdims=True)
             + jnp.sum(pn, axis=-1, keepdims=True) + jnp.sum(px, axis=-1, keepdims=True) + jnp.exp(sink - m))
        pv = (_dot(pp.astype(BF16), vp_ref[0]) + _dot(pc.astype(BF16), vc_ref[0])
              + _dot(pn.astype(BF16), vn_ref[0]) + _dot(px.astype(BF16), vx_ref[0]))
        out = out + jnp.where(mh, pv * (1.0 / l), 0.0)
    o_ref[0] = out.astype(o_ref.dtype)


def _window_attention(q, k, v, kx, vx, sink_tile):
    b, n, w = q.shape
    tq = WINDOW
    nb = n // tq
    m_ctx = kx.shape[1]
    blk = lambda f: pl.BlockSpec((1, tq, w), f)
    prev = lambda bb, i: (bb, jnp.maximum(i - 1, 0), 0)
    cur = lambda bb, i: (bb, i, 0)
    nxt = lambda bb, i: (bb, jnp.minimum(i + 1, nb - 1), 0)
    per_b = lambda bb, i: (bb, 0, 0)
    return pl.pallas_call(
        _win_kernel,
        grid=(b, nb),
        in_specs=[blk(cur), blk(prev), blk(cur), blk(nxt), blk(prev), blk(cur), blk(nxt),
                  pl.BlockSpec((1, m_ctx, w), per_b), pl.BlockSpec((1, m_ctx, w), per_b),
                  pl.BlockSpec(sink_tile.shape, lambda bb, i: (0, 0))],
        out_specs=blk(cur),
        out_shape=jax.ShapeDtypeStruct((b, n, w), BF16),
        compiler_params=_cparams(("arbitrary", "arbitrary")),
        name="attn_window",
    )(q, k, k, k, v, v, v, kx, vx, sink_tile)


def _post_kernel(x_ref, mod_ref, g_ref, o0_ref, o1_ref, o2_ref, o3_ref,
                 wg_ref, bg_ref, wb_ref, wo_ref, out_ref, mix_sc):
    d = D_MODEL
    x = x_ref[...]
    mod = mod_ref[0]
    h = _rmsnorm(x, g_ref[...]) * (1.0 + mod[1:2]) + mod[0:1]
    hb = h.astype(BF16)
    cw = 256
    for j in range(d // cw):
        mixed = None
        for nbr, o_ref in enumerate((o0_ref, o1_ref, o2_ref, o3_ref)):
            lo = nbr * d + j * cw
            gate = _sigmoid(_dot(hb, wg_ref[:, lo:lo + cw]) + bg_ref[:, lo:lo + cw])
            term = gate * _dot(o_ref[...], wb_ref[nbr, :, j * cw:(j + 1) * cw])
            mixed = term if mixed is None else mixed + term
        mix_sc[:, j * cw:(j + 1) * cw] = mixed.astype(BF16)
    out_ref[...] = x + mod[2:3] * _dot(mix_sc[...], wo_ref[...])


def _post(x2d, mod, g_mix, outs, w_gate, b_gate, w_branch, w_out, *, seq_len):
    t, d = x2d.shape
    tm = TM_DENSE
    tiles_per_seq = seq_len // tm
    const2 = lambda i: (0, 0)
    row = lambda i: (i, 0)
    return pl.pallas_call(
        _post_kernel,
        grid=(t // tm,),
        in_specs=[
            pl.BlockSpec((tm, d), row),
            _mod_spec(mod, tiles_per_seq),
            pl.BlockSpec((1, d), const2),
        ] + [pl.BlockSpec((tm, BRANCH_W), row)] * 4 + [
            pl.BlockSpec(w_gate.shape, const2),
            pl.BlockSpec((1, 4 * d), const2),
            pl.BlockSpec(w_branch.shape, lambda i: (0, 0, 0)),
            pl.BlockSpec(w_out.shape, const2),
        ],
        out_specs=pl.BlockSpec((tm, d), row),
        out_shape=jax.ShapeDtypeStruct((t, d), F32),
        scratch_shapes=[pltpu.VMEM((tm, d), BF16)],
        compiler_params=_cparams(("arbitrary",)),
        name="post",
    )(x2d, mod, g_mix.reshape(1, d), *outs, w_gate, b_gate.reshape(1, 4 * d), w_branch, w_out)


def _ffn_kernel(*refs, tiles_per_seq, final):
    it = iter(refs)
    x_ref, xp_ref, xn_ref, mod_ref, g_ref, wup_ref, bup_ref, wcv_ref, bcv_ref, wdn_ref = (next(it) for _ in range(10))
    gf_ref = next(it) if final else None
    out_ref = next(it)
    h_sc, u_sc, acc_sc = next(it), next(it), next(it)

    tm = x_ref.shape[0]
    halo = CONV_HALO
    n_ff = D_FF // FF_CHUNK
    i = pl.program_id(0)
    first = (i % tiles_per_seq) == 0
    last = (i % tiles_per_seq) == tiles_per_seq - 1
    mod = mod_ref[0]
    g = g_ref[...]

    def norm_mod(rows):
        return (_rmsnorm(rows, g) * (1.0 + mod[4:5]) + mod[3:4]).astype(BF16)

    x = x_ref[...]
    h_sc[0:halo, :] = norm_mod(xp_ref[...])
    h_sc[halo:halo + tm, :] = norm_mod(x)
    h_sc[halo + tm:, :] = norm_mod(xn_ref[...])
    acc_sc[...] = jnp.zeros_like(acc_sc)
    ridx = lax.broadcasted_iota(jnp.int32, (tm, FF_CHUNK), 0)
    kill_prev = (ridx == 0) & first
    kill_next = (ridx == tm - 1) & last

    def conv_half(ch):
        u_sc[...] = _dot(h_sc[...], wup_ref[ch]) + bup_ref[ch]
        um = jnp.where(kill_prev, 0.0, u_sc[halo - 1:halo - 1 + tm, :])
        uc = u_sc[halo:halo + tm, :]
        un = jnp.where(kill_next, 0.0, u_sc[halo + 1:halo + 1 + tm, :])
        wc = wcv_ref[ch]
        return bcv_ref[ch] + wc[0:1] * um + wc[1:2] * uc + wc[2:3] * un

    def body(c, carry):
        a = conv_half(c)
        gg = conv_half(c + n_ff)
        act = (gg * _sigmoid(gg) * a).astype(BF16)
        acc_sc[...] += _dot(act, wdn_ref[c])
        return carry

    lax.fori_loop(0, n_ff, body, 0)
    y = x + mod[5:6] * acc_sc[...]
    if final:
        y = _rmsnorm(y, gf_ref[...])
    out_ref[...] = y


def _ffn(x2d, mod, g_ffn, w_up_c, b_up_c, w_conv_c, b_conv_c, w_down_c, g_final, *, seq_len):
    t, d = x2d.shape
    tm = TM_DENSE
    halo = CONV_HALO
    tiles_per_seq = seq_len // tm
    final = g_final is not None
    hb = tm // halo
    n_hblocks = t // halo
    const2 = lambda i: (0, 0)
    const3 = lambda i: (0, 0, 0)
    row = lambda i: (i, 0)
    in_specs = [
        pl.BlockSpec((tm, d), row),
        pl.BlockSpec((halo, d), lambda i: (jnp.maximum(i * hb - 1, 0), 0)),
        pl.BlockSpec((halo, d), lambda i: (jnp.minimum((i + 1) * hb, n_hblocks - 1), 0)),
        _mod_spec(mod, tiles_per_seq),
        pl.BlockSpec((1, d), const2),
        pl.BlockSpec(w_up_c.shape, const3),
        pl.BlockSpec(b_up_c.shape, const3),
        pl.BlockSpec(w_conv_c.shape, const3),
        pl.BlockSpec(b_conv_c.shape, const3),
        pl.BlockSpec(w_down_c.shape, const3),
    ]
    args = [x2d, x2d, x2d, mod, g_ffn.reshape(1, d), w_up_c, b_up_c, w_conv_c, b_conv_c, w_down_c]
    if final:
        in_specs.append(pl.BlockSpec((1, d), const2))
        args.append(g_final.reshape(1, d))
    return pl.pallas_call(
        functools.partial(_ffn_kernel, tiles_per_seq=tiles_per_seq, final=final),
        grid=(t // tm,),
        in_specs=in_specs,
        out_specs=pl.BlockSpec((tm, d), row),
        out_shape=jax.ShapeDtypeStruct((t, d), F32),
        scratch_shapes=[
            pltpu.VMEM((tm + 2 * halo, d), BF16),
            pltpu.VMEM((tm + 2 * halo, FF_CHUNK), F32),
            pltpu.VMEM((tm, d), F32),
        ],
        compiler_params=_cparams(("arbitrary",)),
        name="ffn_final" if final else "ffn",
    )(*args)


def _rope_tables(n_tok):
    rows = n_tok // GRID_W
    row = jnp.repeat(jnp.arange(rows), GRID_W)
    col = jnp.tile(jnp.arange(GRID_W), rows)

    def axis(pos, half):
        freqs = ROPE_BASE ** (-jnp.arange(half, dtype=F32) / half)
        ang = pos.astype(F32)[:, None] * freqs[None, :]
        return jnp.cos(ang), jnp.sin(ang)

    def unit(width):
        half = width // 4
        cr, sr = axis(row, half)
        cc, sc = axis(col, half)
        z = jnp.zeros_like(sr)
        return (jnp.concatenate([cr, cr, cc, cc], axis=1),
                jnp.concatenate([-sr, z, -sc, z], axis=1),
                jnp.concatenate([z, sr, z, sc], axis=1))

    a = [jnp.tile(t, (1, 2)) for t in unit(64)]
    b = [jnp.tile(t, (1, 4)) for t in unit(32)]
    cu = unit(32)
    ones = jnp.ones((n_tok, 64), F32)
    z64 = jnp.zeros((n_tok, 64), F32)
    z32 = jnp.zeros((n_tok, 32), F32)
    c = [jnp.concatenate([ones, cu[0], ones[:, :32]], axis=1),
         jnp.concatenate([z64, cu[1], z32], axis=1),
         jnp.concatenate([z64, cu[2], z32], axis=1)]
    return jnp.stack(a + b + c, axis=0)


def _dup_groups(w):
    g0, g1 = w[..., :HEAD_W], w[..., HEAD_W:]
    return jnp.concatenate([g0, g0, g1, g1], axis=-1)


def _prep_layer(p):
    d = D_MODEL
    cuts = np.cumsum((0,) + IN_SPLITS)
    parts = [p["w_in"][:, cuts[i]:cuts[i + 1]] for i in range(len(IN_SPLITS))]
    rq, rk, rv, rg, wq, wk, wv, dq, dk, dv, mcq, mckv, mkr = parts
    mkr128 = jnp.pad(mkr, ((0, 0), (MLA_NOPE, LANES - MLA_NOPE - MLA_ROPE)))
    w_in_p = jnp.concatenate([rq, rk, rv, rg, wq, _dup_groups(wk), _dup_groups(wv), dq, dk, dv,
                              mcq, mckv, mkr128], axis=1).astype(BF16)
    w_uq = p["w_mla_uq"].reshape(MLA_QRANK, MLA_H, MLA_NOPE + MLA_ROPE)
    w_uq_p = jnp.pad(w_uq, ((0, 0), (0, 0), (0, LANES - MLA_NOPE - MLA_ROPE))).reshape(MLA_QRANK, MLA_H * LANES)
    w_ukv = p["w_mla_ukv"].reshape(MLA_KVRANK, MLA_H, MLA_NOPE + MLA_V)
    w_uk = jnp.pad(w_ukv[:, :, :MLA_NOPE], ((0, 0), (0, 0), (0, LANES - MLA_NOPE))).reshape(MLA_KVRANK, MLA_H * LANES)
    w_uv = w_ukv[:, :, MLA_NOPE:].reshape(MLA_KVRANK, MLA_H * MLA_V)
    n_ff = D_FF // FF_CHUNK
    chunked = lambda a: a.reshape(a.shape[0], 2 * n_ff, FF_CHUNK).transpose(1, 0, 2)
    return dict(
        w_in_p=w_in_p,
        w_uq_p=w_uq_p.astype(BF16),
        w_ukv_p=jnp.concatenate([w_uk, w_uv], axis=1).astype(BF16),
        w_gate=p["w_gate"].astype(BF16),
        w_branch=p["w_branch"].astype(BF16),
        w_out=p["w_out"].astype(BF16),
        w_up_c=chunked(p["w_up"]).astype(BF16),
        b_up_c=chunked(p["b_up"].reshape(1, -1)),
        w_conv_c=chunked(p["w_conv"]),
        b_conv_c=chunked(p["b_conv"].reshape(1, -1)),
        w_down_c=p["w_down"].reshape(n_ff, FF_CHUNK, d).astype(BF16),
        sink_tile=jnp.broadcast_to(p["win_sink"].astype(F32)[:, None], (4, LANES)),
        sink_tile8=jnp.pad(jnp.broadcast_to(p["win_sink"].astype(F32)[:, None], (4, LANES)), ((0, 4), (0, 0))),
        g_diff4=jnp.tile(p["g_diff"], DIFF_H).reshape(1, BRANCH_W),
    )


def _block_diag_state(s):
    b = s.shape[0]
    eye = jnp.eye(RET_H, dtype=s.dtype)
    bd = s[:, :, :, :, None, :] * eye[None, None, :, None, :, None]
    return bd.reshape(b, 2, RET_H * RET_DK, RET_H * RET_DK)


def _mod_rows(mods_l, start, count):
    m = mods_l[start:start + count].reshape(count, 6, D_MODEL)
    return jnp.pad(m, ((0, 0), (0, 2), (0, 0)))


def kernel(x_prompt, x_sample, state_ret, cache_win_k, cache_win_v, cache_diff_k, cache_diff_v, cache_mla_ckv, cache_mla_krope, c, c_ctx, w_mod, b_mod, g_mix, w_in, ret_decay, win_sink, diff_lambda, g_diff, g_mla_q, w_mla_uq, g_mla_kv, w_mla_ukv, w_branch, w_gate, b_gate, w_out, g_ffn, w_up, b_up, w_conv, b_conv, w_down, g_final):
    d = D_MODEL
    bc, lc, _ = x_prompt.shape
    bl, ll, _ = x_sample.shape
    m_ctx = cache_win_k.shape[2]

    cond_rows = jnp.zeros((8, d), F32).at[0].set(c_ctx).at[1:1 + bl].set(c)
    mods = _modulation(cond_rows, w_mod, b_mod)
    tabs = _rope_tables(ll)

    xp = x_prompt.reshape(bc * lc, d)
    xs = x_sample.reshape(bl * ll, d)
    produced = [[] for _ in range(7)]
    for li in range(DEPTH):
        p = _prep_layer(dict(w_in=w_in[li], w_mla_uq=w_mla_uq[li], w_mla_ukv=w_mla_ukv[li], w_gate=w_gate[li],
                             w_branch=w_branch[li], w_out=w_out[li], w_up=w_up[li], b_up=b_up[li],
                             w_conv=w_conv[li], b_conv=b_conv[li], w_down=w_down[li], win_sink=win_sink[li],
                             g_diff=g_diff[li]))
        lam_init = 0.8 - 0.6 * math.exp(-0.3 * li)
        mod_c = _mod_rows(mods[li], 0, 1)
        mod_l = _mod_rows(mods[li], 1, bl)
        final_g = g_final if li == DEPTH - 1 else None

        (ret4, wq, wke, wve, dq, dk, dv, mq, mk, mv,
         wke32, wve32, dk32, dv32, ckv32, mkr32) = _pre(
            xp, mod_c, g_mix[li], p["w_in_p"], p["w_uq_p"], p["w_ukv_p"], g_mla_q[li], g_mla_kv[li], None,
            seq_len=lc, ctx=True)
        zero_state = jnp.zeros((bc, 2, BRANCH_W, BRANCH_W), F32)
        o_ret, st = _retention(ret4, zero_state, ret_decay[li], batch=bc, seq_len=lc)
        r3 = lambda a: a.reshape(bc, lc, a.shape[-1])
        o_win = _attention("sink", r3(wq), r3(wke), r3(wve), params=(p["sink_tile8"],))
        o_diff = _attention("diff", r3(dq), r3(dk), r3(dv), params=(diff_lambda[li], p["g_diff4"]), lam_init=lam_init)
        o_mla = _attention("mla", r3(mq), r3(mk), r3(mv))
        f2 = lambda a: a.reshape(bc * lc, BRANCH_W)
        xp = _post(xp, mod_c, g_mix[li], (o_ret, f2(o_win), f2(o_diff), f2(o_mla)),
                   p["w_gate"], b_gate[li], p["w_branch"], p["w_out"], seq_len=lc)
        xp = _ffn(xp, mod_c, g_ffn[li], p["w_up_c"], p["b_up_c"], p["w_conv_c"], p["b_conv_c"], p["w_down_c"],
                  final_g, seq_len=lc)
        undup = lambda a: a.reshape(bc, lc, 2, 2, HEAD_W)[:, :, :, 0, :]
        produced[0].append(st)
        produced[1].append(undup(wke32))
        produced[2].append(undup(wve32))
        produced[3].append(dk32.reshape(bc, lc, DIFF_H, 2 * DIFF_D))
        produced[4].append(dv32.reshape(bc, lc, DIFF_H, 2 * DIFF_D))
        produced[5].append(ckv32.reshape(bc, lc, MLA_KVRANK))
        produced[6].append(mkr32.reshape(bc, lc, LANES)[:, :, MLA_NOPE:MLA_NOPE + MLA_ROPE])

        (ret4, wq, wke, wve, dq, dk, dv, mq, mk, mv) = _pre(
            xs, mod_l, g_mix[li], p["w_in_p"], p["w_uq_p"], p["w_ukv_p"], g_mla_q[li], g_mla_kv[li], tabs,
            seq_len=ll, ctx=False)
        o_ret, _ = _retention(ret4, _block_diag_state(state_ret[:, li]), ret_decay[li], batch=bl, seq_len=ll)
        r3 = lambda a: a.reshape(bl, ll, a.shape[-1])
        kx_win = _dup_groups(cache_win_k[:, li].reshape(bl, m_ctx, 2 * HEAD_W)).astype(BF16)
        vx_win = _dup_groups(cache_win_v[:, li].reshape(bl, m_ctx, 2 * HEAD_W)).astype(BF16)
        o_win = _window_attention(r3(wq), r3(wke), r3(wve), kx_win, vx_win, p["sink_tile8"])
        kx_diff = cache_diff_k[:, li].reshape(bl, m_ctx, BRANCH_W).astype(BF16)
        vx_diff = cache_diff_v[:, li].reshape(bl, m_ctx, BRANCH_W).astype(BF16)
        o_diff = _attention("diff", r3(dq), r3(dk), r3(dv), ctx_args=(kx_diff, vx_diff),
                            params=(diff_lambda[li], p["g_diff4"]), lam_init=lam_init)
        kr128 = jnp.pad(cache_mla_krope[:, li], ((0, 0), (0, 0), (MLA_NOPE, LANES - MLA_NOPE - MLA_ROPE)))
        o_mla = _attention("mla", r3(mq), r3(mk), r3(mv), ctx_args=(cache_mla_ckv[:, li], kr128, p["w_ukv_p"]))
        f2 = lambda a: a.reshape(bl * ll, BRANCH_W)
        xs = _post(xs, mod_l, g_mix[li], (o_ret, f2(o_win), f2(o_diff), f2(o_mla)),
                   p["w_gate"], b_gate[li], p["w_branch"], p["w_out"], seq_len=ll)
        xs = _ffn(xs, mod_l, g_ffn[li], p["w_up_c"], p["b_up_c"], p["w_conv_c"], p["b_conv_c"], p["w_down_c"],
                  final_g, seq_len=ll)

    y_prompt = xp.reshape(bc, lc, d)
    y_sample = xs.reshape(bl, ll, d)
    stack = lambda lst: jnp.stack(lst, axis=1)
    return (y_prompt, y_sample, stack(produced[0]),
            stack(produced[1]), stack(produced[2]), stack(produced[3]), stack(produced[4]),
            stack(produced[5]), stack(produced[6]))
```

```python
import functools
import math

import numpy as np
import jax
import jax.numpy as jnp
from jax import lax
from jax.experimental import pallas as pl
from jax.experimental.pallas import tpu as pltpu

F32 = jnp.float32
BF16 = jnp.bfloat16

D_MODEL = 1024
DEPTH = 2
GRID_W = 64
ROPE_BASE = 10000.0
NORM_EPS = 1e-6
NEG_INF = -1e30
LOG2E = 1.4426950408889634
RET_H = 4
RET_DK = 64
RET_CHUNK = 128
WIN_HD = 64
WINDOW = 128
DIFF_H = 4
DIFF_D = 32
MLA_H = 4
MLA_NOPE = 64
MLA_ROPE = 32
MLA_V = 64
MLA_QRANK = 256
MLA_KVRANK = 128
D_FF = 2816
IN_SPLITS = (256, 256, 256, 256, 256, 128, 128, 256, 256, 256, 256, 128, 32)

LANES = 128
BF16_ROWS = 16
BRANCH_W = 256
HEAD_W = 64

_C_RET, _C_WQ, _C_WK, _C_WV, _C_DQ, _C_DK, _C_DV, _C_MCQ, _C_MCKV, _C_MKR, _C_END = (
    0, 1024, 1280, 1536, 1792, 2048, 2304, 2560, 2816, 2944, 3072)

TM_DENSE = 256
TM_FFN = 256
FF_CHUNK = 256
TQ_ATTN = 256
KC_ATTN = 512
CONV_HALO = BF16_ROWS
VMEM_LIMIT = 48 * 1024 * 1024


def _cparams(sem):
    return pltpu.CompilerParams(dimension_semantics=sem, vmem_limit_bytes=VMEM_LIMIT)


def _dot(a, b):
    return jnp.dot(a, b, preferred_element_type=F32)


def _dot_nt(a, b):
    return lax.dot_general(a, b, (((1,), (1,)), ((), ())), preferred_element_type=F32)


def _sigmoid(x):
    return 1.0 / (1.0 + jnp.exp(-x))


def _rmsnorm(x, gain):
    ms = jnp.mean(x * x, axis=-1, keepdims=True)
    return x * lax.rsqrt(ms + NORM_EPS) * gain


def _resident(shape):
    zeros = (0,) * len(shape)
    return pl.BlockSpec(shape, lambda i: zeros, pipeline_mode=pl.Buffered(1))


def _mod_spec(mod, tiles_per_seq):
    blk = (1,) + mod.shape[1:]
    if mod.shape[0] == 1:
        return pl.BlockSpec(blk, lambda i: (0, 0, 0))
    return pl.BlockSpec(blk, lambda i: (i // tiles_per_seq, 0, 0))


def _lane_group_mask(shape, group_width, group):
    lane = lax.broadcasted_iota(jnp.int32, shape, len(shape) - 1)
    return (lane // group_width) == group


def _rope_rows(x, cos, sin_up, sin_dn, half):
    width = x.shape[-1]
    up = pltpu.roll(x, width - half, axis=1)
    dn = pltpu.roll(x, half, axis=1)
    return x * cos + up * sin_up + dn * sin_dn


def _mod_kernel(c_ref, w_ref, b_ref, o_ref):
    c = c_ref[...]
    s = c * _sigmoid(c)
    o_ref[0] = _dot(s.astype(BF16), w_ref[0].astype(BF16)) + b_ref[0]


def _modulation(cond_rows, w_mod, b_mod):
    depth, d, n = w_mod.shape
    tn = 1024
    return pl.pallas_call(
        _mod_kernel,
        grid=(depth, n // tn),
        in_specs=[
            pl.BlockSpec((8, d), lambda l, j: (0, 0)),
            pl.BlockSpec((1, d, tn), lambda l, j: (l, 0, j)),
            pl.BlockSpec((1, 1, tn), lambda l, j: (l, 0, j)),
        ],
        out_specs=pl.BlockSpec((1, 8, tn), lambda l, j: (l, 0, j)),
        out_shape=jax.ShapeDtypeStruct((depth, 8, n), F32),
        compiler_params=_cparams(("arbitrary", "arbitrary")),
        name="modulation",
    )(cond_rows, w_mod, b_mod.reshape(depth, 1, n))


_ROPE_HALF = (16, 8, 8)


def _pre_kernel(*refs, rope, ctx):
    it = iter(refs)
    x_ref, mod_ref, g_ref, w_ref, wuq_ref, wukv_ref, gq_ref, gkv_ref = (next(it) for _ in range(8))
    tab_ref = next(it) if rope else None
    (ret_ref, wq_ref, wke_ref, wve_ref, dq_ref, dk_ref, dv_ref,
     mq_ref, mk_ref, mv_ref) = (next(it) for _ in range(10))
    if ctx:
        wke32_ref, wve32_ref, dk32_ref, dv32_ref, ckv32_ref, mkr32_ref = (next(it) for _ in range(6))

    x = x_ref[...]
    mod = mod_ref[0]
    h = _rmsnorm(x, g_ref[...]) * (1.0 + mod[1:2]) + mod[0:1]
    hb = h.astype(BF16)

    def proj(a, b):
        return _dot(hb, w_ref[:, a:b])

    def rot(v, t):
        if not rope:
            return v
        return _rope_rows(v, tab_ref[3 * t], tab_ref[3 * t + 1], tab_ref[3 * t + 2], _ROPE_HALF[t])

    def store(o_ref, val, t=None, o32_ref=None):
        for j in range(val.shape[1] // LANES):
            v = val[:, LANES * j:LANES * (j + 1)]
            if o32_ref is not None:
                o32_ref[:, LANES * j:LANES * (j + 1)] = v
            if t is not None:
                v = rot(v, t)
            o_ref[:, LANES * j:LANES * (j + 1)] = v.astype(o_ref.dtype)

    ret_ref[...] = proj(_C_RET, _C_WQ)
    store(wq_ref, proj(_C_WQ, _C_WK), 0)
    store(wke_ref, proj(_C_WK, _C_WV), 0, wke32_ref if ctx else None)
    store(wve_ref, proj(_C_WV, _C_DQ), None, wve32_ref if ctx else None)
    store(dq_ref, proj(_C_DQ, _C_DK), 1)
    store(dk_ref, proj(_C_DK, _C_DV), 1, dk32_ref if ctx else None)
    store(dv_ref, proj(_C_DV, _C_MCQ), None, dv32_ref if ctx else None)

    cq = _rmsnorm(proj(_C_MCQ, _C_MCKV), gq_ref[...])
    store(mq_ref, _dot(cq.astype(BF16), wuq_ref[...]), 2)
    ckv = _rmsnorm(proj(_C_MCKV, _C_MKR), gkv_ref[...])
    kvp = _dot(ckv.astype(BF16), wukv_ref[...])
    mkr = proj(_C_MKR, _C_END)
    mkr_rot = rot(mkr, 2)
    for hd in range(MLA_H):
        mk_ref[:, LANES * hd:LANES * (hd + 1)] = (kvp[:, LANES * hd:LANES * (hd + 1)] + mkr_rot).astype(BF16)
    mv_ref[...] = kvp[:, MLA_H * LANES:].astype(BF16)
    if ctx:
        ckv32_ref[...] = ckv
        mkr32_ref[...] = mkr


def _pre(x2d, mod, g_mix, w_in_p, w_uq_p, w_ukv_p, g_q, g_kv, tabs, *, seq_len, ctx):
    t, d = x2d.shape
    tm = TM_DENSE
    tiles_per_seq = seq_len // tm
    rope = tabs is not None
    const = lambda i: (0, 0)
    row = lambda i: (i, 0)
    in_specs = [
        pl.BlockSpec((tm, d), row),
        _mod_spec(mod, tiles_per_seq),
        pl.BlockSpec((1, d), const),
        _resident(w_in_p.shape),
        _resident(w_uq_p.shape),
        _resident(w_ukv_p.shape),
        pl.BlockSpec((1, MLA_QRANK), const),
        pl.BlockSpec((1, MLA_KVRANK), const),
    ]
    args = [x2d, mod, g_mix.reshape(1, d), w_in_p, w_uq_p, w_ukv_p,
            g_q.reshape(1, MLA_QRANK), g_kv.reshape(1, MLA_KVRANK)]
    if rope:
        in_specs.append(pl.BlockSpec((9, tm, LANES), lambda i: (0, i % tiles_per_seq, 0)))
        args.append(tabs)
    widths = [(1024, F32)] + [(256, BF16)] * 6 + [(512, BF16), (512, BF16), (256, BF16)]
    if ctx:
        widths += [(256, F32)] * 4 + [(128, F32)] * 2
    out_specs = [pl.BlockSpec((tm, w), row) for w, _ in widths]
    out_shape = [jax.ShapeDtypeStruct((t, w), dt) for w, dt in widths]
    return pl.pallas_call(
        functools.partial(_pre_kernel, rope=rope, ctx=ctx),
        grid=(t // tm,),
        in_specs=in_specs,
        out_specs=out_specs,
        out_shape=out_shape,
        compiler_params=_cparams(("arbitrary",)),
        name="pre_ctx" if ctx else "pre_lat",
    )(*args)


def _log_sigmoid(z):
    return jnp.minimum(z, 0.0) - jnp.log(1.0 + jnp.exp(-jnp.abs(z)))


def _ret_kernel(r_ref, s0_ref, dl_ref, ds_ref, dh_ref, o_ref, st_ref,
                s_sc, ob_sc, qdec_sc, kdec_sc, cdec_sc, intra_sc, *, n_chunks):
    p = pl.program_id(1)
    c = pl.program_id(2)
    fwd = p == 1
    chunk = RET_CHUNK
    w = BRANCH_W

    @pl.when(c == 0)
    def _init():
        s_sc[...] = s0_ref[0, 0]
        lgl = _log_sigmoid(dl_ref[0])[0:1, :]
        i = lax.broadcasted_iota(jnp.int32, (chunk, w), 0).astype(F32)
        qe = jnp.where(fwd, i + 1.0, chunk - i)
        ke = jnp.where(fwd, chunk - 1.0 - i, i)
        qdec_sc[...] = jnp.exp(lgl * qe)
        kdec_sc[...] = jnp.exp(lgl * ke)
        rr = lax.broadcasted_iota(jnp.int32, (w, w), 0) // HEAD_W
        cc = lax.broadcasted_iota(jnp.int32, (w, w), 1) // HEAD_W
        cdec_sc[...] = jnp.where(rr == cc, jnp.exp(_log_sigmoid(ds_ref[0]) * float(chunk)), 0.0)
        ii = lax.broadcasted_iota(jnp.int32, (chunk, chunk), 0)
        jj = lax.broadcasted_iota(jnp.int32, (chunk, chunk), 1)
        dist = jnp.where(fwd, ii - jj, jj - ii)
        for hd in range(RET_H):
            lgh = _log_sigmoid(dh_ref[0, hd])[0:1, :]
            intra_sc[hd] = jnp.where(dist >= 0, jnp.exp(lgh * jnp.maximum(dist, 0).astype(F32)), 0.0)

    blk = r_ref[...]
    q = blk[:, 0:w]
    k = blk[:, w:2 * w] * (RET_DK ** -0.5)
    v = blk[:, 2 * w:3 * w]
    qb = q.astype(BF16)
    kb = k.astype(BF16)
    vb = v.astype(BF16)
    s = s_sc[...]
    o = _dot(qb, s.astype(BF16)) * qdec_sc[...]
    for hd in range(RET_H):
        mh = _lane_group_mask((chunk, w), HEAD_W, hd)
        sc = _dot_nt(jnp.where(mh, qb, jnp.zeros_like(qb)), kb) * intra_sc[hd]
        o = o + jnp.where(mh, _dot(sc.astype(BF16), vb), 0.0)
    kd_t = jnp.transpose(k * kdec_sc[...]).astype(BF16)
    rr = lax.broadcasted_iota(jnp.int32, (w, w), 0) // HEAD_W
    cc = lax.broadcasted_iota(jnp.int32, (w, w), 1) // HEAD_W
    s_new = s * cdec_sc[...] + jnp.where(rr == cc, _dot(kd_t, vb), 0.0)
    s_sc[...] = s_new

    @pl.when(p == 0)
    def _bwd():
        ob_sc[n_chunks - 1 - c] = o

    @pl.when(p == 1)
    def _fwd():
        tot = o + ob_sc[c]
        mu = jnp.zeros_like(tot)
        for hd in range(RET_H):
            mh = _lane_group_mask((chunk, w), HEAD_W, hd)
            m1 = jnp.sum(jnp.where(mh, tot, 0.0), axis=-1, keepdims=True) * (1.0 / HEAD_W)
            mu = jnp.where(mh, m1, mu)
        xc = tot - mu
        var = jnp.zeros_like(tot)
        for hd in range(RET_H):
            mh = _lane_group_mask((chunk, w), HEAD_W, hd)
            v1 = jnp.sum(jnp.where(mh, xc * xc, 0.0), axis=-1, keepdims=True) * (1.0 / HEAD_W)
            var = jnp.where(mh, v1, var)
        g = blk[:, 3 * w:4 * w]
        o_ref[...] = (xc * lax.rsqrt(var + NORM_EPS) * (g * _sigmoid(g))).astype(o_ref.dtype)

    @pl.when(c == n_chunks - 1)
    def _state():
        for hd in range(RET_H):
            st_ref[0, 0, hd] = s_new[HEAD_W * hd:HEAD_W * (hd + 1), HEAD_W * hd:HEAD_W * (hd + 1)]


def _retention(ret4, s0_bd, decay, *, batch, seq_len):
    n = seq_len // RET_CHUNK
    w = BRANCH_W
    dlane = jnp.broadcast_to(jnp.repeat(decay, HEAD_W, axis=1)[:, None, :], (2, 8, w))
    dsub = jnp.broadcast_to(jnp.repeat(decay, HEAD_W, axis=1)[:, :, None], (2, w, w))
    dhead = jnp.broadcast_to(decay[:, :, None, None], (2, RET_H, 8, LANES))
    chunk_of = lambda b, p, c: b * n + c * p + (n - 1 - c) * (1 - p)
    o, st = pl.pallas_call(
        functools.partial(_ret_kernel, n_chunks=n),
        grid=(batch, 2, n),
        in_specs=[
            pl.BlockSpec((RET_CHUNK, 4 * w), lambda b, p, c: (chunk_of(b, p, c), 0)),
            pl.BlockSpec((1, 1, w, w), lambda b, p, c: (b, 1 - p, 0, 0)),
            pl.BlockSpec((1, 8, w), lambda b, p, c: (1 - p, 0, 0)),
            pl.BlockSpec((1, w, w), lambda b, p, c: (1 - p, 0, 0)),
            pl.BlockSpec((1, RET_H, 8, LANES), lambda b, p, c: (1 - p, 0, 0, 0)),
        ],
        out_specs=[
            pl.BlockSpec((RET_CHUNK, w), lambda b, p, c: (b * n + c * p, 0)),
            pl.BlockSpec((1, 1, RET_H, RET_DK, RET_DK), lambda b, p, c: (b, 1 - p, 0, 0, 0)),
        ],
        out_shape=[
            jax.ShapeDtypeStruct((batch * seq_len, w), BF16),
            jax.ShapeDtypeStruct((batch, 2, RET_H, RET_DK, RET_DK), F32),
        ],
        scratch_shapes=[
            pltpu.VMEM((w, w), F32),
            pltpu.VMEM((n, RET_CHUNK, w), F32),
            pltpu.VMEM((RET_CHUNK, w), F32),
            pltpu.VMEM((RET_CHUNK, w), F32),
            pltpu.VMEM((w, w), F32),
            pltpu.VMEM((RET_H, RET_CHUNK, RET_CHUNK), F32),
        ],
        compiler_params=_cparams(("arbitrary", "arbitrary", "arbitrary")),
        name="retention",
    )(ret4, s0_bd, dlane, dsub, dhead)
    return o, st


def _col_fold(acc, x, op):
    for j in range(x.shape[1] // LANES):
        acc = op(acc, x[:, LANES * j:LANES * (j + 1)])
    return acc


_N_STACK = 4


def _attn_kernel(*refs, kind, n_chunks, kc, has_ctx, lam_init):
    it = iter(refs)
    q_ref, k_ref, v_ref = next(it), next(it), next(it)
    if has_ctx:
        if kind == "mla":
            cckv_ref, ckr_ref, wukv_ref = next(it), next(it), next(it)
        else:
            kx_ref, vx_ref = next(it), next(it)
    if kind == "sink":
        sink_ref = next(it)
    if kind == "diff":
        lam_ref, gd_ref = next(it), next(it)
    o_ref = next(it)
    s_sc, m_sc, l_sc, acc_sc = next(it), next(it), next(it), next(it)
    qst_sc = next(it) if kind != "mla" else None
    sx_sc = next(it) if has_ctx else None
    if has_ctx and kind == "mla":
        kx_sc, vx_sc = next(it), next(it)

    tq = q_ref.shape[1]
    w = BRANCH_W
    nv = _N_STACK
    rows = nv * tq
    q = q_ref[0]
    scale = {"sink": WIN_HD ** -0.5, "diff": DIFF_D ** -0.5, "mla": (MLA_NOPE + MLA_ROPE) ** -0.5}[kind]

    if has_ctx and kind == "mla":
        kvc = _dot(cckv_ref[0].astype(BF16), wukv_ref[...])
        kr = ckr_ref[0]
        for hd in range(MLA_H):
            kx_sc[:, LANES * hd:LANES * (hd + 1)] = (kvc[:, LANES * hd:LANES * (hd + 1)] + kr).astype(BF16)
        vx_sc[...] = kvc[:, MLA_H * LANES:].astype(BF16)

    def k_chunk(ch):
        return k_ref[0, pl.ds(pl.multiple_of(ch * kc, kc), kc), :]

    def v_chunk(ch):
        return v_ref[0, pl.ds(pl.multiple_of(ch * kc, kc), kc), :]

    def kx_tile():
        return kx_sc[...] if kind == "mla" else kx_ref[0]

    def vx_tile():
        return vx_sc[...] if kind == "mla" else vx_ref[0]

    def scores(kt):
        if kind == "mla":
            return jnp.concatenate(
                [_dot_nt(q[:, LANES * hd:LANES * (hd + 1)], kt[:, LANES * hd:LANES * (hd + 1)])
                 for hd in range(MLA_H)], axis=0)
        return _dot_nt(qst_sc[...], kt)

    def probs(s):
        mb = m_sc[...]
        cols = [jnp.exp2(s[:, LANES * j:LANES * (j + 1)] * (scale * LOG2E) - mb)
                for j in range(s.shape[1] // LANES)]
        tot = l_sc[...]
        for col in cols:
            tot = tot + col
        l_sc[...] = tot
        return jnp.concatenate(cols, axis=1)

    out = jnp.zeros((tq, w), F32)
    for grp in range(2 if kind == "diff" else 1):
        if kind != "mla":
            for v in range(nv):
                mask = (_lane_group_mask((tq, w), HEAD_W, v) if kind == "sink"
                        else _lane_group_mask((tq, w), DIFF_D, nv * grp + v))
                qst_sc[v * tq:(v + 1) * tq, :] = jnp.where(mask, q, jnp.zeros_like(q))

        m_sc[...] = jnp.full((rows, LANES), -jnp.inf, F32)

        def score_body(ch, carry):
            s = scores(k_chunk(ch))
            s_sc[ch] = s
            m_sc[...] = _col_fold(m_sc[...], s, jnp.maximum)
            return carry

        lax.fori_loop(0, n_chunks, score_body, 0)
        if has_ctx:
            s = scores(kx_tile())
            sx_sc[...] = s
            m_sc[...] = _col_fold(m_sc[...], s, jnp.maximum)
        m = jnp.max(m_sc[...], axis=-1, keepdims=True) * scale
        if kind == "sink":
            sink_col = jnp.concatenate(
                [jnp.broadcast_to(sink_ref[v:v + 1, 0:1], (tq, 1)) for v in range(nv)], axis=0)
            m = jnp.maximum(m, sink_col)
        m_sc[...] = jnp.broadcast_to(m * LOG2E, (rows, LANES))
        l_sc[...] = jnp.zeros((rows, LANES), F32)
        acc_sc[...] = jnp.zeros(acc_sc.shape, F32)

        if kind != "diff":
            def pv_body(ch, carry):
                acc_sc[...] += _dot(probs(s_sc[ch]).astype(BF16), v_chunk(ch))
                return carry

            lax.fori_loop(0, n_chunks, pv_body, 0)
            if has_ctx:
                acc_sc[...] += _dot(probs(sx_sc[...]).astype(BF16), vx_tile())
            l = jnp.sum(l_sc[...], axis=-1, keepdims=True)
            if kind == "sink":
                l = l + jnp.exp(sink_col - m)
            pv = acc_sc[...] * (1.0 / l)
            for v in range(nv):
                out = out + jnp.where(_lane_group_mask((tq, w), HEAD_W, v), pv[v * tq:(v + 1) * tq], 0.0)
        else:
            def exp_body(ch, carry):
                s_sc[ch] = probs(s_sc[ch])
                return carry

            lax.fori_loop(0, n_chunks, exp_body, 0)
            if has_ctx:
                sx_sc[...] = probs(sx_sc[...])
            dl = lam_ref[...]
            lam = (jnp.exp(jnp.sum(dl[0:1] * dl[1:2], axis=-1, keepdims=True))
                   - jnp.exp(jnp.sum(dl[2:3] * dl[3:4], axis=-1, keepdims=True)) + lam_init)
            l = jnp.sum(l_sc[...], axis=-1, keepdims=True)
            second = (lax.broadcasted_iota(jnp.int32, (rows, 1), 0) // tq) % 2 == 1
            norm = jnp.where(second, lam, 1.0) * (1.0 / l)
            l_sc[...] = jnp.broadcast_to(norm, (rows, LANES))

            def weights(pr):
                nb = l_sc[...]
                cols = []
                for j in range(pr.shape[1] // LANES):
                    x = pr[:, LANES * j:LANES * (j + 1)] * nb
                    cols.append(jnp.concatenate([x[0:tq] - x[tq:2 * tq], x[2 * tq:3 * tq] - x[3 * tq:4 * tq]],
                                                axis=0))
                return jnp.concatenate(cols, axis=1).astype(BF16)

            def comb_body(ch, carry):
                acc_sc[...] += _dot(weights(s_sc[ch]), v_chunk(ch))
                return carry

            lax.fori_loop(0, n_chunks, comb_body, 0)
            if has_ctx:
                acc_sc[...] += _dot(weights(sx_sc[...]), vx_tile())
            acc = acc_sc[...]
            for v in range(2):
                out = out + jnp.where(_lane_group_mask((tq, w), HEAD_W, 2 * grp + v), acc[v * tq:(v + 1) * tq], 0.0)

    if kind == "diff":
        ms = jnp.zeros_like(out)
        for hd in range(DIFF_H):
            mh = _lane_group_mask((tq, w), HEAD_W, hd)
            m1 = jnp.sum(jnp.where(mh, out * out, 0.0), axis=-1, keepdims=True) * (1.0 / HEAD_W)
            ms = jnp.where(mh, m1, ms)
        out = out * lax.rsqrt(ms + NORM_EPS) * gd_ref[...] * (1.0 - lam_init)
    o_ref[0] = out.astype(o_ref.dtype)


def _attention(kind, q, k, v, *, ctx_args=(), params=(), lam_init=0.0):
    b, n, wq = q.shape
    nk = k.shape[1]
    tq = min(TQ_ATTN, n)
    kc = min(KC_ATTN, nk)
    n_chunks = nk // kc
    has_ctx = len(ctx_args) > 0
    w = BRANCH_W
    rows = _N_STACK * tq
    per_b = lambda bb, i: (bb, 0, 0)
    in_specs = [
        pl.BlockSpec((1, tq, wq), lambda bb, i: (bb, i, 0)),
        pl.BlockSpec((1, nk, k.shape[2]), per_b),
        pl.BlockSpec((1, nk, w), per_b),
    ]
    args = [q, k, v]
    for a in ctx_args:
        if a.ndim == 3:
            in_specs.append(pl.BlockSpec((1,) + a.shape[1:], per_b))
        else:
            in_specs.append(pl.BlockSpec(a.shape, lambda bb, i: (0, 0)))
        args.append(a)
    for a in params:
        in_specs.append(pl.BlockSpec(a.shape, lambda bb, i: (0, 0)))
        args.append(a)
    scratch = [
        pltpu.VMEM((n_chunks, rows, kc), F32),
        pltpu.VMEM((rows, LANES), F32),
        pltpu.VMEM((rows, LANES), F32),
        pltpu.VMEM((2 * tq if kind == "diff" else rows, w), F32),
    ]
    if kind != "mla":
        scratch.append(pltpu.VMEM((rows, w), BF16))
    if has_ctx:
        m_ctx = ctx_args[0].shape[1]
        scratch.append(pltpu.VMEM((rows, m_ctx), F32))
        if kind == "mla":
            scratch += [pltpu.VMEM((m_ctx, MLA_H * LANES), BF16), pltpu.VMEM((m_ctx, w), BF16)]
    return pl.pallas_call(
        functools.partial(_attn_kernel, kind=kind, n_chunks=n_chunks, kc=kc, has_ctx=has_ctx, lam_init=lam_init),
        grid=(b, n // tq),
        in_specs=in_specs,
        out_specs=pl.BlockSpec((1, tq, w), lambda bb, i: (bb, i, 0)),
        out_shape=jax.ShapeDtypeStruct((b, n, w), BF16),
        scratch_shapes=scratch,
        compiler_params=_cparams(("arbitrary", "arbitrary")),
        name="attn_" + kind + ("_lat" if has_ctx else "_ctx"),
    )(*args)


def _win_kernel(q_ref, kp_ref, kc_ref, kn_ref, vp_ref, vc_ref, vn_ref, kx_ref, vx_ref, sink_ref, o_ref):
    nb = pl.num_programs(1)
    n = pl.program_id(1)
    tq = q_ref.shape[1]
    w = BRANCH_W
    scale = WIN_HD ** -0.5
    q = q_ref[0]
    ii = lax.broadcasted_iota(jnp.int32, (tq, tq), 0)
    jj = lax.broadcasted_iota(jnp.int32, (tq, tq), 1)
    mask_prev = (jj >= ii) & (n > 0)
    mask_next = (jj <= ii) & (n < nb - 1)
    out = jnp.zeros((tq, w), F32)
    for hd in range(4):
        mh = _lane_group_mask((tq, w), HEAD_W, hd)
        qm = jnp.where(mh, q, jnp.zeros_like(q))
        sp = jnp.where(mask_prev, _dot_nt(qm, kp_ref[0]) * scale, NEG_INF)
        sc = _dot_nt(qm, kc_ref[0]) * scale
        sn = jnp.where(mask_next, _dot_nt(qm, kn_ref[0]) * scale, NEG_INF)
        sx = _dot_nt(qm, kx_ref[0]) * scale
        sink = sink_ref[hd:hd + 1, 0:1]
        m = jnp.maximum(jnp.maximum(jnp.max(sp, axis=-1, keepdims=True), jnp.max(sc, axis=-1, keepdims=True)),
                        jnp.maximum(jnp.max(sn, axis=-1, keepdims=True), jnp.max(sx, axis=-1, keepdims=True)))
        m = jnp.maximum(m, sink)
        pp, pc, pn, px = (jnp.exp(s - m) for s in (sp, sc, sn, sx))
        l = (jnp.sum(pp, axis=-1, keepdims=True) + jnp.sum(pc, axis=-1, keepdims=True)
             + jnp.sum(pn, axis=-1, keepdims=True) + jnp.sum(px, axis=-1, keepdims=True) + jnp.exp(sink - m))
        pv = (_dot(pp.astype(BF16), vp_ref[0]) + _dot(pc.astype(BF16), vc_ref[0])
              + _dot(pn.astype(BF16), vn_ref[0]) + _dot(px.astype(BF16), vx_ref[0]))
        out = out + jnp.where(mh, pv * (1.0 / l), 0.0)
    o_ref[0] = out.astype(o_ref.dtype)


def _window_attention(q, k, v, kx, vx, sink_tile):
    b, n, w = q.shape
    tq = WINDOW
    nb = n // tq
    m_ctx = kx.shape[1]
    blk = lambda f: pl.BlockSpec((1, tq, w), f)
    prev = lambda bb, i: (bb, jnp.maximum(i - 1, 0), 0)
    cur = lambda bb, i: (bb, i, 0)
    nxt = lambda bb, i: (bb, jnp.minimum(i + 1, nb - 1), 0)
    per_b = lambda bb, i: (bb, 0, 0)
    return pl.pallas_call(
        _win_kernel,
        grid=(b, nb),
        in_specs=[blk(cur), blk(prev), blk(cur), blk(nxt), blk(prev), blk(cur), blk(nxt),
                  pl.BlockSpec((1, m_ctx, w), per_b), pl.BlockSpec((1, m_ctx, w), per_b),
                  pl.BlockSpec(sink_tile.shape, lambda bb, i: (0, 0))],
        out_specs=blk(cur),
        out_shape=jax.ShapeDtypeStruct((b, n, w), BF16),
        compiler_params=_cparams(("arbitrary", "arbitrary")),
        name="attn_window",
    )(q, k, k, k, v, v, v, kx, vx, sink_tile)


def _post_kernel(x_ref, mod_ref, g_ref, o0_ref, o1_ref, o2_ref, o3_ref,
                 wg_ref, bg_ref, wb_ref, wo_ref, out_ref, mix_sc):
    d = D_MODEL
    x = x_ref[...]
    mod = mod_ref[0]
    h = _rmsnorm(x, g_ref[...]) * (1.0 + mod[1:2]) + mod[0:1]
    hb = h.astype(BF16)
    cw = 256
    for j in range(d // cw):
        mixed = None
        for nbr, o_ref in enumerate((o0_ref, o1_ref, o2_ref, o3_ref)):
            lo = nbr * d + j * cw
            gate = _sigmoid(_dot(hb, wg_ref[:, lo:lo + cw]) + bg_ref[:, lo:lo + cw])
            term = gate * _dot(o_ref[...], wb_ref[nbr, :, j * cw:(j + 1) * cw])
            mixed = term if mixed is None else mixed + term
        mix_sc[:, j * cw:(j + 1) * cw] = mixed.astype(BF16)
    out_ref[...] = x + mod[2:3] * _dot(mix_sc[...], wo_ref[...])


def _post(x2d, mod, g_mix, outs, w_gate, b_gate, w_branch, w_out, *, seq_len):
    t, d = x2d.shape
    tm = TM_DENSE
    tiles_per_seq = seq_len // tm
    const2 = lambda i: (0, 0)
    row = lambda i: (i, 0)
    return pl.pallas_call(
        _post_kernel,
        grid=(t // tm,),
        in_specs=[
            pl.BlockSpec((tm, d), row),
            _mod_spec(mod, tiles_per_seq),
            pl.BlockSpec((1, d), const2),
        ] + [pl.BlockSpec((tm, BRANCH_W), row)] * 4 + [
            _resident(w_gate.shape),
            pl.BlockSpec((1, 4 * d), const2),
            _resident(w_branch.shape),
            _resident(w_out.shape),
        ],
        out_specs=pl.BlockSpec((tm, d), row),
        out_shape=jax.ShapeDtypeStruct((t, d), F32),
        scratch_shapes=[pltpu.VMEM((tm, d), BF16)],
        compiler_params=_cparams(("arbitrary",)),
        name="post",
    )(x2d, mod, g_mix.reshape(1, d), *outs, w_gate, b_gate.reshape(1, 4 * d), w_branch, w_out)


def _ffn_kernel(*refs, tiles_per_seq, final):
    halo = tiles_per_seq > 1
    it = iter(refs)
    x_ref = next(it)
    if halo:
        xp_ref, xn_ref = next(it), next(it)
    mod_ref, g_ref, wup_ref, bup_ref, wcv_ref, bcv_ref, wdn_ref = (next(it) for _ in range(7))
    gf_ref = next(it) if final else None
    out_ref = next(it)
    h_sc, u_sc, acc_sc = next(it), next(it), next(it)

    tm = x_ref.shape[0]
    hr = CONV_HALO
    n_ff = D_FF // FF_CHUNK
    mod = mod_ref[0]
    g = g_ref[...]

    def norm_mod(rows):
        return (_rmsnorm(rows, g) * (1.0 + mod[4:5]) + mod[3:4]).astype(BF16)

    x = x_ref[...]
    h_sc[hr:hr + tm, :] = norm_mod(x)
    if halo:
        h_sc[0:hr, :] = norm_mod(xp_ref[...])
        h_sc[hr + tm:, :] = norm_mod(xn_ref[...])
        i = pl.program_id(0)
        keep_lo = jnp.where((i % tiles_per_seq) == 0, 0.0, 1.0)
        keep_hi = jnp.where((i % tiles_per_seq) == tiles_per_seq - 1, 0.0, 1.0)
        rid = lax.broadcasted_iota(jnp.int32, (8, 1), 0)
        edge_lo = jnp.where(rid == 7, keep_lo, 1.0)
        edge_hi = jnp.where(rid == 0, keep_hi, 1.0)
    else:
        for slot in range(4):
            u_sc[slot, hr - 8:hr, :] = jnp.zeros((8, FF_CHUNK), F32)
            u_sc[slot, hr + tm:hr + tm + 8, :] = jnp.zeros((8, FF_CHUNK), F32)

    def up(ch, slot):
        lo = ch * FF_CHUNK
        if halo:
            u = _dot(h_sc[...], wup_ref[:, lo:lo + FF_CHUNK]) + bup_ref[:, lo:lo + FF_CHUNK]
            u_sc[slot] = u
            u_sc[slot, hr - 8:hr, :] = u[hr - 8:hr] * edge_lo
            u_sc[slot, hr + tm:hr + tm + 8, :] = u[hr + tm:hr + tm + 8] * edge_hi
        else:
            u_sc[slot, hr:hr + tm, :] = (_dot(h_sc[hr:hr + tm, :], wup_ref[:, lo:lo + FF_CHUNK])
                                         + bup_ref[:, lo:lo + FF_CHUNK])

    def conv(ch, slot):
        lo = ch * FF_CHUNK
        wc = wcv_ref[:, lo:lo + FF_CHUNK]
        return (bcv_ref[:, lo:lo + FF_CHUNK] + wc[0:1] * u_sc[slot, hr - 1:hr - 1 + tm, :]
                + wc[1:2] * u_sc[slot, hr:hr + tm, :] + wc[2:3] * u_sc[slot, hr + 1:hr + 1 + tm, :])

    up(0, 0)
    up(n_ff, 1)
    for c in range(n_ff):
        pair = 2 * (c % 2)
        if c + 1 < n_ff:
            up(c + 1, 2 - pair)
            up(c + 1 + n_ff, 3 - pair)
        a = conv(c, pair)
        gg = conv(c + n_ff, pair + 1)
        act = (gg * _sigmoid(gg) * a).astype(BF16)
        term = _dot(act, wdn_ref[c * FF_CHUNK:(c + 1) * FF_CHUNK, :])
        acc = term if c == 0 else acc + term
    y = x + mod[5:6] * acc
    if final:
        y = _rmsnorm(y, gf_ref[...])
    out_ref[...] = y


def _ffn(x2d, mod, g_ffn, w_up, b_up, w_conv, b_conv, w_down, g_final, *, seq_len):
    t, d = x2d.shape
    tm = min(TM_FFN, seq_len)
    hr = CONV_HALO
    tiles_per_seq = seq_len // tm
    halo = tiles_per_seq > 1
    final = g_final is not None
    hb = tm // hr
    n_hblocks = t // hr
    const2 = lambda i: (0, 0)
    row = lambda i: (i, 0)
    in_specs = [pl.BlockSpec((tm, d), row)]
    args = [x2d]
    if halo:
        in_specs += [pl.BlockSpec((hr, d), lambda i: (jnp.maximum(i * hb - 1, 0), 0)),
                     pl.BlockSpec((hr, d), lambda i: (jnp.minimum((i + 1) * hb, n_hblocks - 1), 0))]
        args += [x2d, x2d]
    in_specs += [
        _mod_spec(mod, tiles_per_seq),
        pl.BlockSpec((1, d), const2),
        _resident(w_up.shape),
        pl.BlockSpec(b_up.shape, const2),
        pl.BlockSpec(w_conv.shape, const2),
        pl.BlockSpec(b_conv.shape, const2),
        _resident(w_down.shape),
    ]
    args += [mod, g_ffn.reshape(1, d), w_up, b_up, w_conv, b_conv, w_down]
    if final:
        in_specs.append(pl.BlockSpec((1, d), const2))
        args.append(g_final.reshape(1, d))
    return pl.pallas_call(
        functools.partial(_ffn_kernel, tiles_per_seq=tiles_per_seq, final=final),
        grid=(t // tm,),
        in_specs=in_specs,
        out_specs=pl.BlockSpec((tm, d), row),
        out_shape=jax.ShapeDtypeStruct((t, d), F32),
        scratch_shapes=[
            pltpu.VMEM((tm + 2 * hr, d), BF16),
            pltpu.VMEM((4, tm + 2 * hr, FF_CHUNK), F32),
            pltpu.VMEM((tm, d), F32),
        ],
        compiler_params=_cparams(("arbitrary",)),
        name=("ffn_final" if final else "ffn") + ("_lat" if halo else "_ctx"),
    )(*args)


def _rope_tables(n_tok):
    rows = n_tok // GRID_W
    row = jnp.repeat(jnp.arange(rows), GRID_W)
    col = jnp.tile(jnp.arange(GRID_W), rows)

    def axis(pos, half):
        freqs = ROPE_BASE ** (-jnp.arange(half, dtype=F32) / half)
        ang = pos.astype(F32)[:, None] * freqs[None, :]
        return jnp.cos(ang), jnp.sin(ang)

    def unit(width):
        half = width // 4
        cr, sr = axis(row, half)
        cc, sc = axis(col, half)
        z = jnp.zeros_like(sr)
        return (jnp.concatenate([cr, cr, cc, cc], axis=1),
                jnp.concatenate([-sr, z, -sc, z], axis=1),
                jnp.concatenate([z, sr, z, sc], axis=1))

    a = [jnp.tile(t, (1, 2)) for t in unit(64)]
    b = [jnp.tile(t, (1, 4)) for t in unit(32)]
    cu = unit(32)
    ones = jnp.ones((n_tok, 64), F32)
    z64 = jnp.zeros((n_tok, 64), F32)
    z32 = jnp.zeros((n_tok, 32), F32)
    c = [jnp.concatenate([ones, cu[0], ones[:, :32]], axis=1),
         jnp.concatenate([z64, cu[1], z32], axis=1),
         jnp.concatenate([z64, cu[2], z32], axis=1)]
    return jnp.stack(a + b + c, axis=0)


def _dup_groups(w):
    g0, g1 = w[..., :HEAD_W], w[..., HEAD_W:]
    return jnp.concatenate([g0, g0, g1, g1], axis=-1)


def _prep_layer(p):
    d = D_MODEL
    cuts = np.cumsum((0,) + IN_SPLITS)
    parts = [p["w_in"][:, cuts[i]:cuts[i + 1]] for i in range(len(IN_SPLITS))]
    rq, rk, rv, rg, wq, wk, wv, dq, dk, dv, mcq, mckv, mkr = parts
    mkr128 = jnp.pad(mkr, ((0, 0), (MLA_NOPE, LANES - MLA_NOPE - MLA_ROPE)))
    w_in_p = jnp.concatenate([rq, rk, rv, rg, wq, _dup_groups(wk), _dup_groups(wv), dq, dk, dv,
                              mcq, mckv, mkr128], axis=1).astype(BF16)
    w_uq = p["w_mla_uq"].reshape(MLA_QRANK, MLA_H, MLA_NOPE + MLA_ROPE)
    w_uq_p = jnp.pad(w_uq, ((0, 0), (0, 0), (0, LANES - MLA_NOPE - MLA_ROPE))).reshape(MLA_QRANK, MLA_H * LANES)
    w_ukv = p["w_mla_ukv"].reshape(MLA_KVRANK, MLA_H, MLA_NOPE + MLA_V)
    w_uk = jnp.pad(w_ukv[:, :, :MLA_NOPE], ((0, 0), (0, 0), (0, LANES - MLA_NOPE))).reshape(MLA_KVRANK, MLA_H * LANES)
    w_uv = w_ukv[:, :, MLA_NOPE:].reshape(MLA_KVRANK, MLA_H * MLA_V)
    return dict(
        w_in_p=w_in_p,
        w_uq_p=w_uq_p.astype(BF16),
        w_ukv_p=jnp.concatenate([w_uk, w_uv], axis=1).astype(BF16),
        w_gate=p["w_gate"].astype(BF16),
        w_branch=p["w_branch"].astype(BF16),
        w_out=p["w_out"].astype(BF16),
        w_up=p["w_up"].astype(BF16),
        b_up=p["b_up"].reshape(1, -1),
        w_conv=p["w_conv"],
        b_conv=p["b_conv"].reshape(1, -1),
        w_down=p["w_down"].astype(BF16),
        sink_tile8=jnp.pad(jnp.broadcast_to(p["win_sink"].astype(F32)[:, None], (4, LANES)), ((0, 4), (0, 0))),
        g_diff4=jnp.tile(p["g_diff"], DIFF_H).reshape(1, BRANCH_W),
    )


def _block_diag_state(s):
    b = s.shape[0]
    eye = jnp.eye(RET_H, dtype=s.dtype)
    bd = s[:, :, :, :, None, :] * eye[None, None, :, None, :, None]
    return bd.reshape(b, 2, RET_H * RET_DK, RET_H * RET_DK)


def _mod_rows(mods_l, start, count):
    m = mods_l[start:start + count].reshape(count, 6, D_MODEL)
    return jnp.pad(m, ((0, 0), (0, 2), (0, 0)))


def kernel(x_prompt, x_sample, state_ret, cache_win_k, cache_win_v, cache_diff_k, cache_diff_v, cache_mla_ckv, cache_mla_krope, c, c_ctx, w_mod, b_mod, g_mix, w_in, ret_decay, win_sink, diff_lambda, g_diff, g_mla_q, w_mla_uq, g_mla_kv, w_mla_ukv, w_branch, w_gate, b_gate, w_out, g_ffn, w_up, b_up, w_conv, b_conv, w_down, g_final):
    d = D_MODEL
    bc, lc, _ = x_prompt.shape
    bl, ll, _ = x_sample.shape
    m_ctx = cache_win_k.shape[2]

    cond_rows = jnp.zeros((8, d), F32).at[0].set(c_ctx).at[1:1 + bl].set(c)
    mods = _modulation(cond_rows, w_mod, b_mod)
    tabs = _rope_tables(ll)

    xp = x_prompt.reshape(bc * lc, d)
    xs = x_sample.reshape(bl * ll, d)
    produced = [[] for _ in range(7)]
    for li in range(DEPTH):
        p = _prep_layer(dict(w_in=w_in[li], w_mla_uq=w_mla_uq[li], w_mla_ukv=w_mla_ukv[li], w_gate=w_gate[li],
                             w_branch=w_branch[li], w_out=w_out[li], w_up=w_up[li], b_up=b_up[li],
                             w_conv=w_conv[li], b_conv=b_conv[li], w_down=w_down[li], win_sink=win_sink[li],
                             g_diff=g_diff[li]))
        lam_init = 0.8 - 0.6 * math.exp(-0.3 * li)
        mod_c = _mod_rows(mods[li], 0, 1)
        mod_l = _mod_rows(mods[li], 1, bl)
        final_g = g_final if li == DEPTH - 1 else None

        (ret4, wq, wke, wve, dq, dk, dv, mq, mk, mv,
         wke32, wve32, dk32, dv32, ckv32, mkr32) = _pre(
            xp, mod_c, g_mix[li], p["w_in_p"], p["w_uq_p"], p["w_ukv_p"], g_mla_q[li], g_mla_kv[li], None,
            seq_len=lc, ctx=True)
        zero_state = jnp.zeros((bc, 2, BRANCH_W, BRANCH_W), F32)
        o_ret, st = _retention(ret4, zero_state, ret_decay[li], batch=bc, seq_len=lc)
        r3 = lambda a: a.reshape(bc, lc, a.shape[-1])
        o_win = _attention("sink", r3(wq), r3(wke), r3(wve), params=(p["sink_tile8"],))
        o_diff = _attention("diff", r3(dq), r3(dk), r3(dv), params=(diff_lambda[li], p["g_diff4"]), lam_init=lam_init)
        o_mla = _attention("mla", r3(mq), r3(mk), r3(mv))
        f2 = lambda a: a.reshape(bc * lc, BRANCH_W)
        xp = _post(xp, mod_c, g_mix[li], (o_ret, f2(o_win), f2(o_diff), f2(o_mla)),
                   p["w_gate"], b_gate[li], p["w_branch"], p["w_out"], seq_len=lc)
        xp = _ffn(xp, mod_c, g_ffn[li], p["w_up"], p["b_up"], p["w_conv"], p["b_conv"], p["w_down"],
                  final_g, seq_len=lc)
        undup = lambda a: a.reshape(bc, lc, 2, 2, HEAD_W)[:, :, :, 0, :]
        produced[0].append(st)
        produced[1].append(undup(wke32))
        produced[2].append(undup(wve32))
        produced[3].append(dk32.reshape(bc, lc, DIFF_H, 2 * DIFF_D))
        produced[4].append(dv32.reshape(bc, lc, DIFF_H, 2 * DIFF_D))
        produced[5].append(ckv32.reshape(bc, lc, MLA_KVRANK))
        produced[6].append(mkr32.reshape(bc, lc, LANES)[:, :, MLA_NOPE:MLA_NOPE + MLA_ROPE])

        (ret4, wq, wke, wve, dq, dk, dv, mq, mk, mv) = _pre(
            xs, mod_l, g_mix[li], p["w_in_p"], p["w_uq_p"], p["w_ukv_p"], g_mla_q[li], g_mla_kv[li], tabs,
            seq_len=ll, ctx=False)
        o_ret, _ = _retention(ret4, _block_diag_state(state_ret[:, li]), ret_decay[li], batch=bl, seq_len=ll)
        r3 = lambda a: a.reshape(bl, ll, a.shape[-1])
        kx_win = _dup_groups(cache_win_k[:, li].reshape(bl, m_ctx, 2 * HEAD_W)).astype(BF16)
        vx_win = _dup_groups(cache_win_v[:, li].reshape(bl, m_ctx, 2 * HEAD_W)).astype(BF16)
        o_win = _window_attention(r3(wq), r3(wke), r3(wve), kx_win, vx_win, p["sink_tile8"])
        kx_diff = cache_diff_k[:, li].reshape(bl, m_ctx, BRANCH_W).astype(BF16)
        vx_diff = cache_diff_v[:, li].reshape(bl, m_ctx, BRANCH_W).astype(BF16)
        o_diff = _attention("diff", r3(dq), r3(dk), r3(dv), ctx_args=(kx_diff, vx_diff),
                            params=(diff_lambda[li], p["g_diff4"]), lam_init=lam_init)
        kr128 = jnp.pad(cache_mla_krope[:, li], ((0, 0), (0, 0), (MLA_NOPE, LANES - MLA_NOPE - MLA_ROPE)))
        o_mla = _attention("mla", r3(mq), r3(mk), r3(mv), ctx_args=(cache_mla_ckv[:, li], kr128, p["w_ukv_p"]))
        f2 = lambda a: a.reshape(bl * ll, BRANCH_W)
        xs = _post(xs, mod_l, g_mix[li], (o_ret, f2(o_win), f2(o_diff), f2(o_mla)),
                   p["w_gate"], b_gate[li], p["w_branch"], p["w_out"], seq_len=ll)
        xs = _ffn(xs, mod_l, g_ffn[li], p["w_up"], p["b_up"], p["w_conv"], p["b_conv"], p["w_down"],
                  final_g, seq_len=ll)

    y_prompt = xp.reshape(bc, lc, d)
    y_sample = xs.reshape(bl, ll, d)
    stack = lambda lst: jnp.stack(lst, axis=1)
    return (y_prompt, y_sample, stack(produced[0]),
            stack(produced[1]), stack(produced[2]), stack(produced[3]), stack(produced[4]),
            stack(produced[5]), stack(produced[6]))
```

```python
import functools
import math

import numpy as np
import jax
import jax.numpy as jnp
from jax import lax
from jax.experimental import pallas as pl
from jax.experimental.pallas import tpu as pltpu

F32 = jnp.float32
BF16 = jnp.bfloat16

D_MODEL = 1024
DEPTH = 2
GRID_W = 64
ROPE_BASE = 10000.0
NORM_EPS = 1e-6
NEG_INF = -1e30
LOG2E = 1.4426950408889634
RET_H = 4
RET_DK = 64
RET_CHUNK = 128
WIN_HD = 64
WINDOW = 128
DIFF_H = 4
DIFF_D = 32
MLA_H = 4
MLA_NOPE = 64
MLA_ROPE = 32
MLA_V = 64
MLA_QRANK = 256
MLA_KVRANK = 128
D_FF = 2816
IN_SPLITS = (256, 256, 256, 256, 256, 128, 128, 256, 256, 256, 256, 128, 32)

LANES = 128
BF16_ROWS = 16
BRANCH_W = 256
HEAD_W = 64

_C_RET, _C_WQ, _C_WK, _C_WV, _C_DQ, _C_DK, _C_DV, _C_MCQ, _C_MCKV, _C_MKR, _C_END = (
    0, 1024, 1280, 1536, 1792, 2048, 2304, 2560, 2816, 2944, 3072)

TM_DENSE = 256
TM_FFN = 256
FF_CHUNK = 256
TQ_ATTN = 256
KC_ATTN = 512
CONV_HALO = BF16_ROWS
RET_SEQS_PER_STEP = 8
WIN_BLOCKS_PER_STEP = 4
ATTN_SEQS_PER_STEP = 4
ATTN_UNROLL = 4
VMEM_LIMIT = 48 * 1024 * 1024


def _cparams(sem):
    return pltpu.CompilerParams(dimension_semantics=sem, vmem_limit_bytes=VMEM_LIMIT)


def _dot(a, b):
    return jnp.dot(a, b, preferred_element_type=F32)


def _dot_nt(a, b):
    return lax.dot_general(a, b, (((1,), (1,)), ((), ())), preferred_element_type=F32)


def _sigmoid(x):
    return 1.0 / (1.0 + jnp.exp(-x))


def _rmsnorm(x, gain):
    ms = jnp.mean(x * x, axis=-1, keepdims=True)
    return x * lax.rsqrt(ms + NORM_EPS) * gain


def _round_robin(bodies):
    bodies = list(bodies)
    while bodies:
        bodies = [body for body in bodies if next(body, "done") != "done"]


def _resident(shape):
    zeros = (0,) * len(shape)
    return pl.BlockSpec(shape, lambda i: zeros, pipeline_mode=pl.Buffered(1))


def _mod_spec(mod, tiles_per_seq):
    blk = (1,) + mod.shape[1:]
    if mod.shape[0] == 1:
        return pl.BlockSpec(blk, lambda i: (0, 0, 0))
    return pl.BlockSpec(blk, lambda i: (i // tiles_per_seq, 0, 0))


def _lane_group_mask(shape, group_width, group):
    lane = lax.broadcasted_iota(jnp.int32, shape, len(shape) - 1)
    return (lane // group_width) == group


def _rope_rows(x, cos, sin_up, sin_dn, half):
    width = x.shape[-1]
    up = pltpu.roll(x, width - half, axis=1)
    dn = pltpu.roll(x, half, axis=1)
    return x * cos + up * sin_up + dn * sin_dn


def _mod_kernel(c_ref, w_ref, b_ref, o_ref):
    c = c_ref[...]
    s = c * _sigmoid(c)
    o_ref[0] = _dot(s.astype(BF16), w_ref[0].astype(BF16)) + b_ref[0]


def _modulation(cond_rows, w_mod, b_mod):
    depth, d, n = w_mod.shape
    tn = 1024
    return pl.pallas_call(
        _mod_kernel,
        grid=(depth, n // tn),
        in_specs=[
            pl.BlockSpec((8, d), lambda l, j: (0, 0)),
            pl.BlockSpec((1, d, tn), lambda l, j: (l, 0, j)),
            pl.BlockSpec((1, 1, tn), lambda l, j: (l, 0, j)),
        ],
        out_specs=pl.BlockSpec((1, 8, tn), lambda l, j: (l, 0, j)),
        out_shape=jax.ShapeDtypeStruct((depth, 8, n), F32),
        compiler_params=_cparams(("arbitrary", "arbitrary")),
        name="modulation",
    )(cond_rows, w_mod, b_mod.reshape(depth, 1, n))


_ROPE_HALF = (16, 8, 8)


def _pre_kernel(*refs, rope, ctx):
    it = iter(refs)
    x_ref, mod_ref, g_ref, w_ref, wuq_ref, wukv_ref, gq_ref, gkv_ref = (next(it) for _ in range(8))
    tab_ref = next(it) if rope else None
    (ret_ref, wq_ref, wke_ref, wve_ref, dq_ref, dk_ref, dv_ref,
     mq_ref, mk_ref, mv_ref) = (next(it) for _ in range(10))
    if ctx:
        wke32_ref, wve32_ref, dk32_ref, dv32_ref, ckv32_ref, mkr32_ref = (next(it) for _ in range(6))

    x = x_ref[...]
    mod = mod_ref[0]
    h = _rmsnorm(x, g_ref[...]) * (1.0 + mod[1:2]) + mod[0:1]
    hb = h.astype(BF16)

    def proj(a, b):
        return _dot(hb, w_ref[:, a:b])

    def rot(v, t):
        if not rope:
            return v
        return _rope_rows(v, tab_ref[3 * t], tab_ref[3 * t + 1], tab_ref[3 * t + 2], _ROPE_HALF[t])

    def store(o_ref, val, t=None, o32_ref=None):
        for j in range(val.shape[1] // LANES):
            v = val[:, LANES * j:LANES * (j + 1)]
            if o32_ref is not None:
                o32_ref[:, LANES * j:LANES * (j + 1)] = v
            if t is not None:
                v = rot(v, t)
            o_ref[:, LANES * j:LANES * (j + 1)] = v.astype(o_ref.dtype)

    ret_ref[...] = proj(_C_RET, _C_WQ)
    store(wq_ref, proj(_C_WQ, _C_WK), 0)
    store(wke_ref, proj(_C_WK, _C_WV), 0, wke32_ref if ctx else None)
    store(wve_ref, proj(_C_WV, _C_DQ), None, wve32_ref if ctx else None)
    store(dq_ref, proj(_C_DQ, _C_DK), 1)
    store(dk_ref, proj(_C_DK, _C_DV), 1, dk32_ref if ctx else None)
    store(dv_ref, proj(_C_DV, _C_MCQ), None, dv32_ref if ctx else None)

    cq = _rmsnorm(proj(_C_MCQ, _C_MCKV), gq_ref[...])
    store(mq_ref, _dot(cq.astype(BF16), wuq_ref[...]), 2)
    ckv = _rmsnorm(proj(_C_MCKV, _C_MKR), gkv_ref[...])
    kvp = _dot(ckv.astype(BF16), wukv_ref[...])
    mkr = proj(_C_MKR, _C_END)
    mkr_rot = rot(mkr, 2)
    for hd in range(MLA_H):
        mk_ref[:, LANES * hd:LANES * (hd + 1)] = (kvp[:, LANES * hd:LANES * (hd + 1)] + mkr_rot).astype(BF16)
    mv_ref[...] = kvp[:, MLA_H * LANES:].astype(BF16)
    if ctx:
        ckv32_ref[...] = ckv
        mkr32_ref[...] = mkr


def _pre(x2d, mod, g_mix, w_in_p, w_uq_p, w_ukv_p, g_q, g_kv, tabs, *, seq_len, ctx):
    t, d = x2d.shape
    tm = TM_DENSE
    tiles_per_seq = seq_len // tm
    rope = tabs is not None
    const = lambda i: (0, 0)
    row = lambda i: (i, 0)
    in_specs = [
        pl.BlockSpec((tm, d), row),
        _mod_spec(mod, tiles_per_seq),
        pl.BlockSpec((1, d), const),
        _resident(w_in_p.shape),
        _resident(w_uq_p.shape),
        _resident(w_ukv_p.shape),
        pl.BlockSpec((1, MLA_QRANK), const),
        pl.BlockSpec((1, MLA_KVRANK), const),
    ]
    args = [x2d, mod, g_mix.reshape(1, d), w_in_p, w_uq_p, w_ukv_p,
            g_q.reshape(1, MLA_QRANK), g_kv.reshape(1, MLA_KVRANK)]
    if rope:
        in_specs.append(pl.BlockSpec((9, tm, LANES), lambda i: (0, i % tiles_per_seq, 0)))
        args.append(tabs)
    widths = [(1024, F32)] + [(256, BF16)] * 6 + [(512, BF16), (512, BF16), (256, BF16)]
    if ctx:
        widths += [(256, F32)] * 4 + [(128, F32)] * 2
    out_specs = [pl.BlockSpec((tm, w), row) for w, _ in widths]
    out_shape = [jax.ShapeDtypeStruct((t, w), dt) for w, dt in widths]
    return pl.pallas_call(
        functools.partial(_pre_kernel, rope=rope, ctx=ctx),
        grid=(t // tm,),
        in_specs=in_specs,
        out_specs=out_specs,
        out_shape=out_shape,
        compiler_params=_cparams(("arbitrary",)),
        name="pre_ctx" if ctx else "pre_lat",
    )(*args)


def _log_sigmoid(z):
    return jnp.minimum(z, 0.0) - jnp.log(1.0 + jnp.exp(-jnp.abs(z)))


def _ret_kernel(r_ref, s0_ref, dl_ref, ds_ref, dh_ref, o_ref, st_ref,
                s_sc, ob_sc, qdec_sc, kdec_sc, cdec_sc, intra_sc, *, n_chunks):
    p = pl.program_id(1)
    c = pl.program_id(2)
    fwd = p == 1
    chunk = RET_CHUNK
    w = BRANCH_W

    @pl.when(c == 0)
    def _init():
        lgl = _log_sigmoid(dl_ref[0])[0:1, :]
        i = lax.broadcasted_iota(jnp.int32, (chunk, w), 0).astype(F32)
        qe = jnp.where(fwd, i + 1.0, chunk - i)
        ke = jnp.where(fwd, chunk - 1.0 - i, i)
        qdec_sc[...] = jnp.exp(lgl * qe)
        kdec_sc[...] = jnp.exp(lgl * ke)
        rr = lax.broadcasted_iota(jnp.int32, (w, w), 0) // HEAD_W
        cc = lax.broadcasted_iota(jnp.int32, (w, w), 1) // HEAD_W
        cdec_sc[...] = jnp.where(rr == cc, jnp.exp(_log_sigmoid(ds_ref[0]) * float(chunk)), 0.0)
        ii = lax.broadcasted_iota(jnp.int32, (chunk, chunk), 0)
        jj = lax.broadcasted_iota(jnp.int32, (chunk, chunk), 1)
        dist = jnp.where(fwd, ii - jj, jj - ii)
        for hd in range(RET_H):
            lgh = _log_sigmoid(dh_ref[0, hd])[0:1, :]
            intra_sc[hd] = jnp.where(dist >= 0, jnp.exp(lgh * jnp.maximum(dist, 0).astype(F32)), 0.0)

    seqs = r_ref.shape[0]
    head_mask = [_lane_group_mask((chunk, w), HEAD_W, hd) for hd in range(RET_H)]
    rr = lax.broadcasted_iota(jnp.int32, (w, w), 0) // HEAD_W
    cc = lax.broadcasted_iota(jnp.int32, (w, w), 1) // HEAD_W
    block_diag = rr == cc
    outs = [None] * seqs

    def scan_step(sq):
        blk = r_ref[sq]
        q = blk[:, 0:w]
        k = blk[:, w:2 * w] * (RET_DK ** -0.5)
        v = blk[:, 2 * w:3 * w]
        qb = q.astype(BF16)
        kb = k.astype(BF16)
        vb = v.astype(BF16)
        s = jnp.where(c == 0, s0_ref[sq, 0], s_sc[sq])
        yield
        o = _dot(qb, s.astype(BF16)) * qdec_sc[...]
        for hd in range(RET_H):
            sc = _dot_nt(jnp.where(head_mask[hd], qb, jnp.zeros_like(qb)), kb) * intra_sc[hd]
            yield
            o = o + jnp.where(head_mask[hd], _dot(sc.astype(BF16), vb), 0.0)
        outs[sq] = o
        yield
        kd_t = jnp.transpose(k * kdec_sc[...]).astype(BF16)
        s_new = s * cdec_sc[...] + jnp.where(block_diag, _dot(kd_t, vb), 0.0)
        s_sc[sq] = s_new
        yield
        for hd in range(RET_H):
            st_ref[sq, 0, hd] = s_new[HEAD_W * hd:HEAD_W * (hd + 1), HEAD_W * hd:HEAD_W * (hd + 1)]

    def finish(sq):
        tot = outs[sq] + ob_sc[sq, c]
        mu = jnp.zeros_like(tot)
        for hd in range(RET_H):
            m1 = jnp.sum(jnp.where(head_mask[hd], tot, 0.0), axis=-1, keepdims=True) * (1.0 / HEAD_W)
            mu = jnp.where(head_mask[hd], m1, mu)
        yield
        xc = tot - mu
        var = jnp.zeros_like(tot)
        for hd in range(RET_H):
            v1 = jnp.sum(jnp.where(head_mask[hd], xc * xc, 0.0), axis=-1, keepdims=True) * (1.0 / HEAD_W)
            var = jnp.where(head_mask[hd], v1, var)
        yield
        g = r_ref[sq][:, 3 * w:4 * w]
        o_ref[sq] = (xc * lax.rsqrt(var + NORM_EPS) * (g * _sigmoid(g))).astype(o_ref.dtype)

    _round_robin([scan_step(sq) for sq in range(seqs)])

    @pl.when(p == 0)
    def _bwd():
        for sq in range(seqs):
            ob_sc[sq, n_chunks - 1 - c] = outs[sq]

    @pl.when(p == 1)
    def _fwd():
        _round_robin([finish(sq) for sq in range(seqs)])


def _retention(ret4, s0_bd, decay, *, batch, seq_len):
    n = seq_len // RET_CHUNK
    w = BRANCH_W
    seqs = RET_SEQS_PER_STEP if batch % RET_SEQS_PER_STEP == 0 else batch
    dlane = jnp.broadcast_to(jnp.repeat(decay, HEAD_W, axis=1)[:, None, :], (2, 8, w))
    dsub = jnp.broadcast_to(jnp.repeat(decay, HEAD_W, axis=1)[:, :, None], (2, w, w))
    dhead = jnp.broadcast_to(decay[:, :, None, None], (2, RET_H, 8, LANES))
    chunk_of = lambda p, c: c * p + (n - 1 - c) * (1 - p)
    o, st = pl.pallas_call(
        functools.partial(_ret_kernel, n_chunks=n),
        grid=(batch // seqs, 2, n),
        in_specs=[
            pl.BlockSpec((seqs, RET_CHUNK, 4 * w), lambda b, p, c: (b, chunk_of(p, c), 0)),
            pl.BlockSpec((seqs, 1, w, w), lambda b, p, c: (b, 1 - p, 0, 0)),
            pl.BlockSpec((1, 8, w), lambda b, p, c: (1 - p, 0, 0)),
            pl.BlockSpec((1, w, w), lambda b, p, c: (1 - p, 0, 0)),
            pl.BlockSpec((1, RET_H, 8, LANES), lambda b, p, c: (1 - p, 0, 0, 0)),
        ],
        out_specs=[
            pl.BlockSpec((seqs, RET_CHUNK, w), lambda b, p, c: (b, c * p, 0)),
            pl.BlockSpec((seqs, 1, RET_H, RET_DK, RET_DK), lambda b, p, c: (b, 1 - p, 0, 0, 0)),
        ],
        out_shape=[
            jax.ShapeDtypeStruct((batch, seq_len, w), BF16),
            jax.ShapeDtypeStruct((batch, 2, RET_H, RET_DK, RET_DK), F32),
        ],
        scratch_shapes=[
            pltpu.VMEM((seqs, w, w), F32),
            pltpu.VMEM((seqs, n, RET_CHUNK, w), F32),
            pltpu.VMEM((RET_CHUNK, w), F32),
            pltpu.VMEM((RET_CHUNK, w), F32),
            pltpu.VMEM((w, w), F32),
            pltpu.VMEM((RET_H, RET_CHUNK, RET_CHUNK), F32),
        ],
        compiler_params=_cparams(("arbitrary", "arbitrary", "arbitrary")),
        name="retention",
    )(ret4.reshape(batch, seq_len, 4 * w), s0_bd, dlane, dsub, dhead)
    return o.reshape(batch * seq_len, w), st


def _col_fold(acc, x, op):
    for j in range(x.shape[1] // LANES):
        acc = op(acc, x[:, LANES * j:LANES * (j + 1)])
    return acc


_N_STACK = 4


def _attn_body(*refs, kind, n_chunks, kc, has_ctx, lam_init):
    it = iter(refs)
    q_ref, k_ref, v_ref = next(it), next(it), next(it)
    if has_ctx:
        if kind == "mla":
            cckv_ref, ckr_ref, wukv_ref = next(it), next(it), next(it)
        else:
            kx_ref, vx_ref = next(it), next(it)
    if kind == "sink":
        sink_ref = next(it)
    if kind == "diff":
        lam_ref, gd_ref = next(it), next(it)
    o_ref = next(it)
    s_sc, m_sc, l_sc, acc_sc = next(it), next(it), next(it), next(it)
    qst_sc = next(it) if kind != "mla" else None
    sx_sc = next(it) if has_ctx else None
    if has_ctx and kind == "mla":
        kx_sc, vx_sc = next(it), next(it)

    tq = q_ref.shape[1]
    w = BRANCH_W
    nv = _N_STACK
    rows = nv * tq
    unroll = ATTN_UNROLL if n_chunks % ATTN_UNROLL == 0 else 1
    q = q_ref[0]
    scale = {"sink": WIN_HD ** -0.5, "diff": DIFF_D ** -0.5, "mla": (MLA_NOPE + MLA_ROPE) ** -0.5}[kind]

    if has_ctx and kind == "mla":
        kvc = _dot(cckv_ref[0].astype(BF16), wukv_ref[...])
        kr = ckr_ref[0]
        for hd in range(MLA_H):
            kx_sc[:, LANES * hd:LANES * (hd + 1)] = (kvc[:, LANES * hd:LANES * (hd + 1)] + kr).astype(BF16)
        vx_sc[...] = kvc[:, MLA_H * LANES:].astype(BF16)

    def k_chunk(ch):
        return k_ref[0, pl.ds(pl.multiple_of(ch * kc, kc), kc), :]

    def v_chunk(ch):
        return v_ref[0, pl.ds(pl.multiple_of(ch * kc, kc), kc), :]

    def kx_tile():
        return kx_sc[...] if kind == "mla" else kx_ref[0]

    def vx_tile():
        return vx_sc[...] if kind == "mla" else vx_ref[0]

    def scores(kt):
        if kind == "mla":
            return jnp.concatenate(
                [_dot_nt(q[:, LANES * hd:LANES * (hd + 1)], kt[:, LANES * hd:LANES * (hd + 1)])
                 for hd in range(MLA_H)], axis=0)
        return _dot_nt(qst_sc[...], kt)

    def probs(s):
        mb = m_sc[...]
        cols = [jnp.exp2(s[:, LANES * j:LANES * (j + 1)] * (scale * LOG2E) - mb)
                for j in range(s.shape[1] // LANES)]
        tot = l_sc[...]
        for col in cols:
            tot = tot + col
        l_sc[...] = tot
        return jnp.concatenate(cols, axis=1)

    out = jnp.zeros((tq, w), F32)
    for grp in range(2 if kind == "diff" else 1):
        if kind != "mla":
            for v in range(nv):
                mask = (_lane_group_mask((tq, w), HEAD_W, v) if kind == "sink"
                        else _lane_group_mask((tq, w), DIFF_D, nv * grp + v))
                qst_sc[v * tq:(v + 1) * tq, :] = jnp.where(mask, q, jnp.zeros_like(q))
        yield

        m_sc[...] = jnp.full((rows, LANES), -jnp.inf, F32)

        def score_body(ch, carry):
            s = scores(k_chunk(ch))
            s_sc[ch] = s
            m_sc[...] = _col_fold(m_sc[...], s, jnp.maximum)
            return carry

        lax.fori_loop(0, n_chunks, score_body, 0, unroll=unroll)
        if has_ctx:
            s = scores(kx_tile())
            sx_sc[...] = s
            m_sc[...] = _col_fold(m_sc[...], s, jnp.maximum)
        yield
        m = jnp.max(m_sc[...], axis=-1, keepdims=True) * scale
        if kind == "sink":
            sink_col = jnp.concatenate(
                [jnp.broadcast_to(sink_ref[v:v + 1, 0:1], (tq, 1)) for v in range(nv)], axis=0)
            m = jnp.maximum(m, sink_col)
        m_sc[...] = jnp.broadcast_to(m * LOG2E, (rows, LANES))
        l_sc[...] = jnp.zeros((rows, LANES), F32)
        acc_sc[...] = jnp.zeros(acc_sc.shape, F32)
        yield

        if kind != "diff":
            def pv_body(ch, carry):
                acc_sc[...] += _dot(probs(s_sc[ch]).astype(BF16), v_chunk(ch))
                return carry

            lax.fori_loop(0, n_chunks, pv_body, 0, unroll=unroll)
            if has_ctx:
                acc_sc[...] += _dot(probs(sx_sc[...]).astype(BF16), vx_tile())
            yield
            l = jnp.sum(l_sc[...], axis=-1, keepdims=True)
            if kind == "sink":
                l = l + jnp.exp(sink_col - m)
            pv = acc_sc[...] * (1.0 / l)
            for v in range(nv):
                out = out + jnp.where(_lane_group_mask((tq, w), HEAD_W, v), pv[v * tq:(v + 1) * tq], 0.0)
        else:
            def exp_body(ch, carry):
                s_sc[ch] = probs(s_sc[ch])
                return carry

            lax.fori_loop(0, n_chunks, exp_body, 0, unroll=unroll)
            if has_ctx:
                sx_sc[...] = probs(sx_sc[...])
            yield
            dl = lam_ref[...]
            lam = (jnp.exp(jnp.sum(dl[0:1] * dl[1:2], axis=-1, keepdims=True))
                   - jnp.exp(jnp.sum(dl[2:3] * dl[3:4], axis=-1, keepdims=True)) + lam_init)
            l = jnp.sum(l_sc[...], axis=-1, keepdims=True)
            second = (lax.broadcasted_iota(jnp.int32, (rows, 1), 0) // tq) % 2 == 1
            norm = jnp.where(second, lam, 1.0) * (1.0 / l)
            l_sc[...] = jnp.broadcast_to(norm, (rows, LANES))
            yield

            def weights(pr):
                nb = l_sc[...]
                cols = []
                for j in range(pr.shape[1] // LANES):
                    x = pr[:, LANES * j:LANES * (j + 1)] * nb
                    cols.append(jnp.concatenate([x[0:tq] - x[tq:2 * tq], x[2 * tq:3 * tq] - x[3 * tq:4 * tq]],
                                                axis=0))
                return jnp.concatenate(cols, axis=1).astype(BF16)

            def comb_body(ch, carry):
                acc_sc[...] += _dot(weights(s_sc[ch]), v_chunk(ch))
                return carry

            lax.fori_loop(0, n_chunks, comb_body, 0, unroll=unroll)
            if has_ctx:
                acc_sc[...] += _dot(weights(sx_sc[...]), vx_tile())
            yield
            acc = acc_sc[...]
            for v in range(2):
                out = out + jnp.where(_lane_group_mask((tq, w), HEAD_W, 2 * grp + v), acc[v * tq:(v + 1) * tq], 0.0)

    if kind == "diff":
        ms = jnp.zeros_like(out)
        for hd in range(DIFF_H):
            mh = _lane_group_mask((tq, w), HEAD_W, hd)
            m1 = jnp.sum(jnp.where(mh, out * out, 0.0), axis=-1, keepdims=True) * (1.0 / HEAD_W)
            ms = jnp.where(mh, m1, ms)
        out = out * lax.rsqrt(ms + NORM_EPS) * gd_ref[...] * (1.0 - lam_init)
    o_ref[0] = out.astype(o_ref.dtype)


def _attn_multi_kernel(*refs, seqs, batched, **kw):
    n_io = len(batched)
    bodies = []
    for sq in range(seqs):
        view = [r.at[pl.ds(sq, 1)] if flag else r for r, flag in zip(refs[:n_io], batched)]
        view += [r.at[sq] for r in refs[n_io:]]
        bodies.append(_attn_body(*view, **kw))
    _round_robin(bodies)


def _attention(kind, q, k, v, *, ctx_args=(), params=(), lam_init=0.0):
    b, n, wq = q.shape
    nk = k.shape[1]
    tq = min(TQ_ATTN, n)
    kc = min(KC_ATTN, nk)
    n_chunks = nk // kc
    has_ctx = len(ctx_args) > 0
    w = BRANCH_W
    rows = _N_STACK * tq
    seqs = ATTN_SEQS_PER_STEP if (n == tq and b % ATTN_SEQS_PER_STEP == 0) else 1
    per_b = lambda bb, i: (bb, 0, 0)
    shared = lambda bb, i: (0, 0)
    in_specs = [
        pl.BlockSpec((seqs, tq, wq), lambda bb, i: (bb, i, 0)),
        pl.BlockSpec((seqs, nk, k.shape[2]), per_b),
        pl.BlockSpec((seqs, nk, w), per_b),
    ]
    args = [q, k, v]
    batched = [True, True, True]
    for a in ctx_args:
        if a.ndim == 3:
            in_specs.append(pl.BlockSpec((seqs,) + a.shape[1:], per_b))
        else:
            in_specs.append(pl.BlockSpec(a.shape, shared))
        batched.append(a.ndim == 3)
        args.append(a)
    for a in params:
        in_specs.append(pl.BlockSpec(a.shape, shared))
        batched.append(False)
        args.append(a)
    batched.append(True)
    scratch = [
        (n_chunks, rows, kc, F32),
        (rows, LANES, F32),
        (rows, LANES, F32),
        (2 * tq if kind == "diff" else rows, w, F32),
    ]
    if kind != "mla":
        scratch.append((rows, w, BF16))
    if has_ctx:
        m_ctx = ctx_args[0].shape[1]
        scratch.append((rows, m_ctx, F32))
        if kind == "mla":
            scratch += [(m_ctx, MLA_H * LANES, BF16), (m_ctx, w, BF16)]
    return pl.pallas_call(
        functools.partial(_attn_multi_kernel, seqs=seqs, batched=tuple(batched), kind=kind, n_chunks=n_chunks,
                          kc=kc, has_ctx=has_ctx, lam_init=lam_init),
        grid=(b // seqs, n // tq),
        in_specs=in_specs,
        out_specs=pl.BlockSpec((seqs, tq, w), lambda bb, i: (bb, i, 0)),
        out_shape=jax.ShapeDtypeStruct((b, n, w), BF16),
        scratch_shapes=[pltpu.VMEM((seqs,) + sh[:-1], sh[-1]) for sh in scratch],
        compiler_params=_cparams(("arbitrary", "arbitrary")),
        name="attn_" + kind + ("_lat" if has_ctx else "_ctx"),
    )(*args)


def _win_kernel(q_ref, kp_ref, kc_ref, kn_ref, vp_ref, vc_ref, vn_ref, kx_ref, vx_ref, sink_ref, o_ref):
    n_steps = pl.num_programs(1)
    step = pl.program_id(1)
    tq = WINDOW
    nblk = q_ref.shape[1] // tq
    w = BRANCH_W
    nh = 4
    rows = nh * tq
    scale = WIN_HD ** -0.5
    head_mask = [_lane_group_mask((tq, w), HEAD_W, hd) for hd in range(nh)]
    ii = lax.broadcasted_iota(jnp.int32, (rows, tq), 0) % tq
    jj = lax.broadcasted_iota(jnp.int32, (rows, tq), 1)
    sink_col = jnp.concatenate([jnp.broadcast_to(sink_ref[hd:hd + 1, 0:1], (tq, 1)) for hd in range(nh)], axis=0)
    kx = kx_ref[0]
    vx = vx_ref[0]

    def rows_of(ref, blk):
        return ref[0, blk * tq:(blk + 1) * tq, :]

    def block(blk):
        q = rows_of(q_ref, blk)
        qst = jnp.concatenate([jnp.where(head_mask[hd], q, jnp.zeros_like(q)) for hd in range(nh)], axis=0)
        kp, vp = (kp_ref[0], vp_ref[0]) if blk == 0 else (rows_of(kc_ref, blk - 1), rows_of(vc_ref, blk - 1))
        kn, vn = (kn_ref[0], vn_ref[0]) if blk == nblk - 1 else (rows_of(kc_ref, blk + 1), rows_of(vc_ref, blk + 1))
        mask_prev = jj >= ii
        mask_next = jj <= ii
        if blk == 0:
            mask_prev = mask_prev & (step > 0)
        if blk == nblk - 1:
            mask_next = mask_next & (step < n_steps - 1)
        yield
        sp = jnp.where(mask_prev, _dot_nt(qst, kp) * scale, NEG_INF)
        sc = _dot_nt(qst, rows_of(kc_ref, blk)) * scale
        sn = jnp.where(mask_next, _dot_nt(qst, kn) * scale, NEG_INF)
        sx = _dot_nt(qst, kx) * scale
        yield
        m = jnp.maximum(jnp.maximum(jnp.max(sp, axis=-1, keepdims=True), jnp.max(sc, axis=-1, keepdims=True)),
                        jnp.maximum(jnp.max(sn, axis=-1, keepdims=True), jnp.max(sx, axis=-1, keepdims=True)))
        m = jnp.maximum(m, sink_col)
        yield
        pp, pc, pn, px = (jnp.exp(s - m) for s in (sp, sc, sn, sx))
        l = (jnp.sum(pp, axis=-1, keepdims=True) + jnp.sum(pc, axis=-1, keepdims=True)
             + jnp.sum(pn, axis=-1, keepdims=True) + jnp.sum(px, axis=-1, keepdims=True) + jnp.exp(sink_col - m))
        yield
        pv = (_dot(pp.astype(BF16), vp) + _dot(pc.astype(BF16), rows_of(vc_ref, blk))
              + _dot(pn.astype(BF16), vn) + _dot(px.astype(BF16), vx)) * (1.0 / l)
        yield
        out = jnp.zeros((tq, w), F32)
        for hd in range(nh):
            out = out + jnp.where(head_mask[hd], pv[hd * tq:(hd + 1) * tq], 0.0)
        o_ref[0, blk * tq:(blk + 1) * tq, :] = out.astype(o_ref.dtype)

    _round_robin([block(blk) for blk in range(nblk)])


def _window_attention(q, k, v, kx, vx, sink_tile):
    b, n, w = q.shape
    tq = WINDOW
    span = WIN_BLOCKS_PER_STEP * tq
    nb = n // tq
    m_ctx = kx.shape[1]
    edge = lambda f: pl.BlockSpec((1, tq, w), f)
    prev = lambda bb, i: (bb, jnp.maximum(i * WIN_BLOCKS_PER_STEP - 1, 0), 0)
    nxt = lambda bb, i: (bb, jnp.minimum((i + 1) * WIN_BLOCKS_PER_STEP, nb - 1), 0)
    mid = pl.BlockSpec((1, span, w), lambda bb, i: (bb, i, 0))
    per_b = lambda bb, i: (bb, 0, 0)
    return pl.pallas_call(
        _win_kernel,
        grid=(b, n // span),
        in_specs=[mid, edge(prev), mid, edge(nxt), edge(prev), mid, edge(nxt),
                  pl.BlockSpec((1, m_ctx, w), per_b), pl.BlockSpec((1, m_ctx, w), per_b),
                  pl.BlockSpec(sink_tile.shape, lambda bb, i: (0, 0))],
        out_specs=mid,
        out_shape=jax.ShapeDtypeStruct((b, n, w), BF16),
        compiler_params=_cparams(("arbitrary", "arbitrary")),
        name="attn_window",
    )(q, k, k, k, v, v, v, kx, vx, sink_tile)


def _post_kernel(x_ref, mod_ref, g_ref, o0_ref, o1_ref, o2_ref, o3_ref,
                 wg_ref, bg_ref, wb_ref, wo_ref, out_ref, mix_sc):
    d = D_MODEL
    x = x_ref[...]
    mod = mod_ref[0]
    h = _rmsnorm(x, g_ref[...]) * (1.0 + mod[1:2]) + mod[0:1]
    hb = h.astype(BF16)
    cw = 256
    for j in range(d // cw):
        mixed = None
        for nbr, o_ref in enumerate((o0_ref, o1_ref, o2_ref, o3_ref)):
            lo = nbr * d + j * cw
            gate = _sigmoid(_dot(hb, wg_ref[:, lo:lo + cw]) + bg_ref[:, lo:lo + cw])
            term = gate * _dot(o_ref[...], wb_ref[nbr, :, j * cw:(j + 1) * cw])
            mixed = term if mixed is None else mixed + term
        mix_sc[:, j * cw:(j + 1) * cw] = mixed.astype(BF16)
    out_ref[...] = x + mod[2:3] * _dot(mix_sc[...], wo_ref[...])


def _post(x2d, mod, g_mix, outs, w_gate, b_gate, w_branch, w_out, *, seq_len):
    t, d = x2d.shape
    tm = TM_DENSE
    tiles_per_seq = seq_len // tm
    const2 = lambda i: (0, 0)
    row = lambda i: (i, 0)
    return pl.pallas_call(
        _post_kernel,
        grid=(t // tm,),
        in_specs=[
            pl.BlockSpec((tm, d), row),
            _mod_spec(mod, tiles_per_seq),
            pl.BlockSpec((1, d), const2),
        ] + [pl.BlockSpec((tm, BRANCH_W), row)] * 4 + [
            _resident(w_gate.shape),
            pl.BlockSpec((1, 4 * d), const2),
            _resident(w_branch.shape),
            _resident(w_out.shape),
        ],
        out_specs=pl.BlockSpec((tm, d), row),
        out_shape=jax.ShapeDtypeStruct((t, d), F32),
        scratch_shapes=[pltpu.VMEM((tm, d), BF16)],
        compiler_params=_cparams(("arbitrary",)),
        name="post",
    )(x2d, mod, g_mix.reshape(1, d), *outs, w_gate, b_gate.reshape(1, 4 * d), w_branch, w_out)


def _ffn_kernel(*refs, tiles_per_seq, final):
    halo = tiles_per_seq > 1
    it = iter(refs)
    x_ref = next(it)
    if halo:
        xp_ref, xn_ref = next(it), next(it)
    mod_ref, g_ref, wup_ref, bup_ref, wcv_ref, bcv_ref, wdn_ref = (next(it) for _ in range(7))
    gf_ref = next(it) if final else None
    out_ref = next(it)
    h_sc, u_sc, acc_sc = next(it), next(it), next(it)

    tm = x_ref.shape[0]
    hr = CONV_HALO
    n_ff = D_FF // FF_CHUNK
    mod = mod_ref[0]
    g = g_ref[...]

    def norm_mod(rows):
        return (_rmsnorm(rows, g) * (1.0 + mod[4:5]) + mod[3:4]).astype(BF16)

    x = x_ref[...]
    h_sc[hr:hr + tm, :] = norm_mod(x)
    if halo:
        h_sc[0:hr, :] = norm_mod(xp_ref[...])
        h_sc[hr + tm:, :] = norm_mod(xn_ref[...])
        i = pl.program_id(0)
        keep_lo = jnp.where((i % tiles_per_seq) == 0, 0.0, 1.0)
        keep_hi = jnp.where((i % tiles_per_seq) == tiles_per_seq - 1, 0.0, 1.0)
        rid = lax.broadcasted_iota(jnp.int32, (8, 1), 0)
        edge_lo = jnp.where(rid == 7, keep_lo, 1.0)
        edge_hi = jnp.where(rid == 0, keep_hi, 1.0)
    else:
        for slot in range(4):
            u_sc[slot, hr - 8:hr, :] = jnp.zeros((8, FF_CHUNK), F32)
            u_sc[slot, hr + tm:hr + tm + 8, :] = jnp.zeros((8, FF_CHUNK), F32)

    def up(ch, slot):
        lo = ch * FF_CHUNK
        if halo:
            u = _dot(h_sc[...], wup_ref[:, lo:lo + FF_CHUNK]) + bup_ref[:, lo:lo + FF_CHUNK]
            u_sc[slot] = u
            u_sc[slot, hr - 8:hr, :] = u[hr - 8:hr] * edge_lo
            u_sc[slot, hr + tm:hr + tm + 8, :] = u[hr + tm:hr + tm + 8] * edge_hi
        else:
            u_sc[slot, hr:hr + tm, :] = (_dot(h_sc[hr:hr + tm, :], wup_ref[:, lo:lo + FF_CHUNK])
                                         + bup_ref[:, lo:lo + FF_CHUNK])

    def conv(ch, slot):
        lo = ch * FF_CHUNK
        wc = wcv_ref[:, lo:lo + FF_CHUNK]
        return (bcv_ref[:, lo:lo + FF_CHUNK] + wc[0:1] * u_sc[slot, hr - 1:hr - 1 + tm, :]
                + wc[1:2] * u_sc[slot, hr:hr + tm, :] + wc[2:3] * u_sc[slot, hr + 1:hr + 1 + tm, :])

    up(0, 0)
    up(n_ff, 1)
    for c in range(n_ff):
        pair = 2 * (c % 2)
        if c + 1 < n_ff:
            up(c + 1, 2 - pair)
            up(c + 1 + n_ff, 3 - pair)
        a = conv(c, pair)
        gg = conv(c + n_ff, pair + 1)
        act = (gg * _sigmoid(gg) * a).astype(BF16)
        term = _dot(act, wdn_ref[c * FF_CHUNK:(c + 1) * FF_CHUNK, :])
        acc = term if c == 0 else acc + term
    y = x + mod[5:6] * acc
    if final:
        y = _rmsnorm(y, gf_ref[...])
    out_ref[...] = y


def _ffn(x2d, mod, g_ffn, w_up, b_up, w_conv, b_conv, w_down, g_final, *, seq_len):
    t, d = x2d.shape
    tm = min(TM_FFN, seq_len)
    hr = CONV_HALO
    tiles_per_seq = seq_len // tm
    halo = tiles_per_seq > 1
    final = g_final is not None
    hb = tm // hr
    n_hblocks = t // hr
    const2 = lambda i: (0, 0)
    row = lambda i: (i, 0)
    in_specs = [pl.BlockSpec((tm, d), row)]
    args = [x2d]
    if halo:
        in_specs += [pl.BlockSpec((hr, d), lambda i: (jnp.maximum(i * hb - 1, 0), 0)),
                     pl.BlockSpec((hr, d), lambda i: (jnp.minimum((i + 1) * hb, n_hblocks - 1), 0))]
        args += [x2d, x2d]
    in_specs += [
        _mod_spec(mod, tiles_per_seq),
        pl.BlockSpec((1, d), const2),
        _resident(w_up.shape),
        pl.BlockSpec(b_up.shape, const2),
        pl.BlockSpec(w_conv.shape, const2),
        pl.BlockSpec(b_conv.shape, const2),
        _resident(w_down.shape),
    ]
    args += [mod, g_ffn.reshape(1, d), w_up, b_up, w_conv, b_conv, w_down]
    if final:
        in_specs.append(pl.BlockSpec((1, d), const2))
        args.append(g_final.reshape(1, d))
    return pl.pallas_call(
        functools.partial(_ffn_kernel, tiles_per_seq=tiles_per_seq, final=final),
        grid=(t // tm,),
        in_specs=in_specs,
        out_specs=pl.BlockSpec((tm, d), row),
        out_shape=jax.ShapeDtypeStruct((t, d), F32),
        scratch_shapes=[
            pltpu.VMEM((tm + 2 * hr, d), BF16),
            pltpu.VMEM((4, tm + 2 * hr, FF_CHUNK), F32),
            pltpu.VMEM((tm, d), F32),
        ],
        compiler_params=_cparams(("arbitrary",)),
        name=("ffn_final" if final else "ffn") + ("_lat" if halo else "_ctx"),
    )(*args)


def _rope_tables(n_tok):
    rows = n_tok // GRID_W

    def axis(n_pos, half):
        freqs = ROPE_BASE ** (-jnp.arange(half, dtype=F32) / half)
        ang = jnp.arange(n_pos).astype(F32)[:, None] * freqs[None, :]
        return jnp.cos(ang), jnp.sin(ang)

    def unit(width):
        half = width // 4
        cr, sr = (jnp.repeat(t, GRID_W, axis=0) for t in axis(rows, half))
        cc, sc = (jnp.tile(t, (rows, 1)) for t in axis(GRID_W, half))
        z = jnp.zeros_like(sr)
        return (jnp.concatenate([cr, cr, cc, cc], axis=1),
                jnp.concatenate([-sr, z, -sc, z], axis=1),
                jnp.concatenate([z, sr, z, sc], axis=1))

    a = [jnp.tile(t, (1, 2)) for t in unit(64)]
    b = [jnp.tile(t, (1, 4)) for t in unit(32)]
    cu = unit(32)
    ones = jnp.ones((n_tok, 64), F32)
    z64 = jnp.zeros((n_tok, 64), F32)
    z32 = jnp.zeros((n_tok, 32), F32)
    c = [jnp.concatenate([ones, cu[0], ones[:, :32]], axis=1),
         jnp.concatenate([z64, cu[1], z32], axis=1),
         jnp.concatenate([z64, cu[2], z32], axis=1)]
    return jnp.stack(a + b + c, axis=0)


def _dup_groups(w):
    g0, g1 = w[..., :HEAD_W], w[..., HEAD_W:]
    return jnp.concatenate([g0, g0, g1, g1], axis=-1)


def _prep_layer(p):
    d = D_MODEL
    cuts = np.cumsum((0,) + IN_SPLITS)
    parts = [p["w_in"][:, cuts[i]:cuts[i + 1]] for i in range(len(IN_SPLITS))]
    rq, rk, rv, rg, wq, wk, wv, dq, dk, dv, mcq, mckv, mkr = parts
    mkr128 = jnp.pad(mkr, ((0, 0), (MLA_NOPE, LANES - MLA_NOPE - MLA_ROPE)))
    w_in_p = jnp.concatenate([rq, rk, rv, rg, wq, _dup_groups(wk), _dup_groups(wv), dq, dk, dv,
                              mcq, mckv, mkr128], axis=1).astype(BF16)
    w_uq = p["w_mla_uq"].reshape(MLA_QRANK, MLA_H, MLA_NOPE + MLA_ROPE)
    w_uq_p = jnp.pad(w_uq, ((0, 0), (0, 0), (0, LANES - MLA_NOPE - MLA_ROPE))).reshape(MLA_QRANK, MLA_H * LANES)
    w_ukv = p["w_mla_ukv"].reshape(MLA_KVRANK, MLA_H, MLA_NOPE + MLA_V)
    w_uk = jnp.pad(w_ukv[:, :, :MLA_NOPE], ((0, 0), (0, 0), (0, LANES - MLA_NOPE))).reshape(MLA_KVRANK, MLA_H * LANES)
    w_uv = w_ukv[:, :, MLA_NOPE:].reshape(MLA_KVRANK, MLA_H * MLA_V)
    return dict(
        w_in_p=w_in_p,
        w_uq_p=w_uq_p.astype(BF16),
        w_ukv_p=jnp.concatenate([w_uk, w_uv], axis=1).astype(BF16),
        w_gate=p["w_gate"].astype(BF16),
        w_branch=p["w_branch"].astype(BF16),
        w_out=p["w_out"].astype(BF16),
        w_up=p["w_up"].astype(BF16),
        b_up=p["b_up"].reshape(1, -1),
        w_conv=p["w_conv"],
        b_conv=p["b_conv"].reshape(1, -1),
        w_down=p["w_down"].astype(BF16),
        sink_tile8=jnp.pad(jnp.broadcast_to(p["win_sink"].astype(F32)[:, None], (4, LANES)), ((0, 4), (0, 0))),
        g_diff4=jnp.tile(p["g_diff"], DIFF_H).reshape(1, BRANCH_W),
    )


def _block_diag_state(s):
    b = s.shape[0]
    eye = jnp.eye(RET_H, dtype=s.dtype)
    bd = s[:, :, :, :, None, :] * eye[None, None, :, None, :, None]
    return bd.reshape(b, 2, RET_H * RET_DK, RET_H * RET_DK)


def _mod_rows(mods_l, start, count):
    m = mods_l[start:start + count].reshape(count, 6, D_MODEL)
    return jnp.pad(m, ((0, 0), (0, 2), (0, 0)))


def kernel(x_prompt, x_sample, state_ret, cache_win_k, cache_win_v, cache_diff_k, cache_diff_v, cache_mla_ckv, cache_mla_krope, c, c_ctx, w_mod, b_mod, g_mix, w_in, ret_decay, win_sink, diff_lambda, g_diff, g_mla_q, w_mla_uq, g_mla_kv, w_mla_ukv, w_branch, w_gate, b_gate, w_out, g_ffn, w_up, b_up, w_conv, b_conv, w_down, g_final):
    d = D_MODEL
    bc, lc, _ = x_prompt.shape
    bl, ll, _ = x_sample.shape
    m_ctx = cache_win_k.shape[2]

    cond_rows = jnp.zeros((8, d), F32).at[0].set(c_ctx).at[1:1 + bl].set(c)
    mods = _modulation(cond_rows, w_mod, b_mod)
    tabs = _rope_tables(ll)

    xp = x_prompt.reshape(bc * lc, d)
    xs = x_sample.reshape(bl * ll, d)
    produced = [[] for _ in range(7)]
    for li in range(DEPTH):
        p = _prep_layer(dict(w_in=w_in[li], w_mla_uq=w_mla_uq[li], w_mla_ukv=w_mla_ukv[li], w_gate=w_gate[li],
                             w_branch=w_branch[li], w_out=w_out[li], w_up=w_up[li], b_up=b_up[li],
                             w_conv=w_conv[li], b_conv=b_conv[li], w_down=w_down[li], win_sink=win_sink[li],
                             g_diff=g_diff[li]))
        lam_init = 0.8 - 0.6 * math.exp(-0.3 * li)
        mod_c = _mod_rows(mods[li], 0, 1)
        mod_l = _mod_rows(mods[li], 1, bl)
        final_g = g_final if li == DEPTH - 1 else None

        (ret4, wq, wke, wve, dq, dk, dv, mq, mk, mv,
         wke32, wve32, dk32, dv32, ckv32, mkr32) = _pre(
            xp, mod_c, g_mix[li], p["w_in_p"], p["w_uq_p"], p["w_ukv_p"], g_mla_q[li], g_mla_kv[li], None,
            seq_len=lc, ctx=True)
        zero_state = jnp.zeros((bc, 2, BRANCH_W, BRANCH_W), F32)
        o_ret, st = _retention(ret4, zero_state, ret_decay[li], batch=bc, seq_len=lc)
        r3 = lambda a: a.reshape(bc, lc, a.shape[-1])
        o_win = _attention("sink", r3(wq), r3(wke), r3(wve), params=(p["sink_tile8"],))
        o_diff = _attention("diff", r3(dq), r3(dk), r3(dv), params=(diff_lambda[li], p["g_diff4"]), lam_init=lam_init)
        o_mla = _attention("mla", r3(mq), r3(mk), r3(mv))
        f2 = lambda a: a.reshape(bc * lc, BRANCH_W)
        xp = _post(xp, mod_c, g_mix[li], (o_ret, f2(o_win), f2(o_diff), f2(o_mla)),
                   p["w_gate"], b_gate[li], p["w_branch"], p["w_out"], seq_len=lc)
        xp = _ffn(xp, mod_c, g_ffn[li], p["w_up"], p["b_up"], p["w_conv"], p["b_conv"], p["w_down"],
                  final_g, seq_len=lc)
        undup = lambda a: a.reshape(bc, lc, 2, 2, HEAD_W)[:, :, :, 0, :]
        produced[0].append(st)
        produced[1].append(undup(wke32))
        produced[2].append(undup(wve32))
        produced[3].append(dk32.reshape(bc, lc, DIFF_H, 2 * DIFF_D))
        produced[4].append(dv32.reshape(bc, lc, DIFF_H, 2 * DIFF_D))
        produced[5].append(ckv32.reshape(bc, lc, MLA_KVRANK))
        produced[6].append(mkr32.reshape(bc, lc, LANES)[:, :, MLA_NOPE:MLA_NOPE + MLA_ROPE])

        (ret4, wq, wke, wve, dq, dk, dv, mq, mk, mv) = _pre(
            xs, mod_l, g_mix[li], p["w_in_p"], p["w_uq_p"], p["w_ukv_p"], g_mla_q[li], g_mla_kv[li], tabs,
            seq_len=ll, ctx=False)
        o_ret, _ = _retention(ret4, _block_diag_state(state_ret[:, li]), ret_decay[li], batch=bl, seq_len=ll)
        r3 = lambda a: a.reshape(bl, ll, a.shape[-1])
        kx_win = _dup_groups(cache_win_k[:, li].reshape(bl, m_ctx, 2 * HEAD_W)).astype(BF16)
        vx_win = _dup_groups(cache_win_v[:, li].reshape(bl, m_ctx, 2 * HEAD_W)).astype(BF16)
        o_win = _window_attention(r3(wq), r3(wke), r3(wve), kx_win, vx_win, p["sink_tile8"])
        kx_diff = cache_diff_k[:, li].reshape(bl, m_ctx, BRANCH_W).astype(BF16)
        vx_diff = cache_diff_v[:, li].reshape(bl, m_ctx, BRANCH_W).astype(BF16)
        o_diff = _attention("diff", r3(dq), r3(dk), r3(dv), ctx_args=(kx_diff, vx_diff),
                            params=(diff_lambda[li], p["g_diff4"]), lam_init=lam_init)
        kr128 = jnp.pad(cache_mla_krope[:, li], ((0, 0), (0, 0), (MLA_NOPE, LANES - MLA_NOPE - MLA_ROPE)))
        o_mla = _attention("mla", r3(mq), r3(mk), r3(mv), ctx_args=(cache_mla_ckv[:, li], kr128, p["w_ukv_p"]))
        f2 = lambda a: a.reshape(bl * ll, BRANCH_W)
        xs = _post(xs, mod_l, g_mix[li], (o_ret, f2(o_win), f2(o_diff), f2(o_mla)),
                   p["w_gate"], b_gate[li], p["w_branch"], p["w_out"], seq_len=ll)
        xs = _ffn(xs, mod_l, g_ffn[li], p["w_up"], p["b_up"], p["w_conv"], p["b_conv"], p["w_down"],
                  final_g, seq_len=ll)

    y_prompt = xp.reshape(bc, lc, d)
    y_sample = xs.reshape(bl, ll, d)
    stack = lambda lst: jnp.stack(lst, axis=1)
    return (y_prompt, y_sample, stack(produced[0]),
            stack(produced[1]), stack(produced[2]), stack(produced[3]), stack(produced[4]),
            stack(produced[5]), stack(produced[6]))
```

```python
import functools
import math

import numpy as np
import jax
import jax.numpy as jnp
from jax import lax
from jax.experimental import pallas as pl
from jax.experimental.pallas import tpu as pltpu

F32 = jnp.float32
BF16 = jnp.bfloat16

D_MODEL = 1024
DEPTH = 2
GRID_W = 64
ROPE_BASE = 10000.0
NORM_EPS = 1e-6
NEG_INF = -1e30
LOG2E = 1.4426950408889634
RET_H = 4
RET_DK = 64
RET_CHUNK = 128
WIN_HD = 64
WINDOW = 128
DIFF_H = 4
DIFF_D = 32
MLA_H = 4
MLA_NOPE = 64
MLA_ROPE = 32
MLA_V = 64
MLA_QRANK = 256
MLA_KVRANK = 128
D_FF = 2816
IN_SPLITS = (256, 256, 256, 256, 256, 128, 128, 256, 256, 256, 256, 128, 32)

LANES = 128
BF16_ROWS = 16
BRANCH_W = 256
HEAD_W = 64

_C_RET, _C_WQ, _C_WK, _C_WV, _C_DQ, _C_DK, _C_DV, _C_MCQ, _C_MCKV, _C_MKR, _C_END = (
    0, 1024, 1280, 1536, 1792, 2048, 2304, 2560, 2816, 2944, 3072)

TM_DENSE = 256
TM_FFN = 256
FF_CHUNK = 256
TQ_ATTN = 256
KC_ATTN = 512
CONV_HALO = BF16_ROWS
RET_SEQS_PER_STEP = 8
WIN_BLOCKS_PER_STEP = 4
ATTN_SEQS_PER_STEP = 4
ATTN_UNROLL = 4
VMEM_LIMIT = 48 * 1024 * 1024
ATTN_VMEM_EXTRA = 16 * 1024 * 1024


def _cparams(sem, vmem=VMEM_LIMIT):
    return pltpu.CompilerParams(dimension_semantics=sem, vmem_limit_bytes=vmem)


def _dot(a, b):
    return jnp.dot(a, b, preferred_element_type=F32)


def _dot_nt(a, b):
    return lax.dot_general(a, b, (((1,), (1,)), ((), ())), preferred_element_type=F32)


def _sigmoid(x):
    return 1.0 / (1.0 + jnp.exp(-x))


def _rmsnorm(x, gain):
    ms = jnp.mean(x * x, axis=-1, keepdims=True)
    return x * lax.rsqrt(ms + NORM_EPS) * gain


def _interleave(bodies, delays):
    active = list(zip(bodies, delays))
    rnd = 0
    while active:
        active = [(body, d) for body, d in active if d > rnd or next(body, "done") != "done"]
        rnd += 1
        yield


def _round_robin(bodies):
    bodies = list(bodies)
    for _ in _interleave(bodies, [0] * len(bodies)):
        pass


def _resident(stacked_shape, layer):
    zeros = (0,) * (len(stacked_shape) - 1)
    return pl.BlockSpec((None,) + tuple(stacked_shape[1:]), lambda i: (layer,) + zeros,
                        pipeline_mode=pl.Buffered(1))


def _mod_spec(mod, tiles_per_seq):
    blk = (1,) + mod.shape[1:]
    if mod.shape[0] == 1:
        return pl.BlockSpec(blk, lambda i: (0, 0, 0))
    return pl.BlockSpec(blk, lambda i: (i // tiles_per_seq, 0, 0))


def _lane_group_mask(shape, group_width, group):
    lane = lax.broadcasted_iota(jnp.int32, shape, len(shape) - 1)
    return (lane // group_width) == group


def _rope_rows(x, cos, sin_up, sin_dn, half):
    width = x.shape[-1]
    up = pltpu.roll(x, width - half, axis=1)
    dn = pltpu.roll(x, half, axis=1)
    return x * cos + up * sin_up + dn * sin_dn


def _mod_kernel(c_ref, w_ref, b_ref, o_ref):
    c = c_ref[...]
    s = c * _sigmoid(c)
    o_ref[0] = _dot(s.astype(BF16), w_ref[0].astype(BF16)) + b_ref[0]


def _modulation(cond_rows, w_mod, b_mod):
    depth, d, n = w_mod.shape
    tn = 1024
    return pl.pallas_call(
        _mod_kernel,
        grid=(depth, n // tn),
        in_specs=[
            pl.BlockSpec((8, d), lambda l, j: (0, 0)),
            pl.BlockSpec((1, d, tn), lambda l, j: (l, 0, j)),
            pl.BlockSpec((1, 1, tn), lambda l, j: (l, 0, j)),
        ],
        out_specs=pl.BlockSpec((1, 8, tn), lambda l, j: (l, 0, j)),
        out_shape=jax.ShapeDtypeStruct((depth, 8, n), F32),
        compiler_params=_cparams(("arbitrary", "arbitrary")),
        name="modulation",
    )(cond_rows, w_mod, b_mod.reshape(depth, 1, n))


_ROPE_HALF = (16, 8, 8)


def _pre_kernel(*refs, rope, ctx):
    it = iter(refs)
    x_ref, mod_ref, g_ref, w_ref, wuq_ref, wukv_ref, gq_ref, gkv_ref = (next(it) for _ in range(8))
    tab_ref = next(it) if rope else None
    (ret_ref, wq_ref, wke_ref, wve_ref, dq_ref, dk_ref, dv_ref,
     mq_ref, mk_ref, mv_ref) = (next(it) for _ in range(10))
    if ctx:
        wke32_ref, wve32_ref, dk32_ref, dv32_ref, ckv32_ref, mkr32_ref = (next(it) for _ in range(6))

    x = x_ref[...]
    mod = mod_ref[0]
    h = _rmsnorm(x, g_ref[...]) * (1.0 + mod[1:2]) + mod[0:1]
    hb = h.astype(BF16)

    def proj(a, b):
        return _dot(hb, w_ref[:, a:b])

    def rot(v, t):
        if not rope:
            return v
        return _rope_rows(v, tab_ref[3 * t], tab_ref[3 * t + 1], tab_ref[3 * t + 2], _ROPE_HALF[t])

    def store(o_ref, val, t=None, o32_ref=None):
        for j in range(val.shape[1] // LANES):
            v = val[:, LANES * j:LANES * (j + 1)]
            if o32_ref is not None:
                o32_ref[:, LANES * j:LANES * (j + 1)] = v
            if t is not None:
                v = rot(v, t)
            o_ref[:, LANES * j:LANES * (j + 1)] = v.astype(o_ref.dtype)

    cq_in = proj(_C_MCQ, _C_MCKV)
    ckv_in = proj(_C_MCKV, _C_MKR)
    mkr = proj(_C_MKR, _C_END)
    ret_ref[...] = proj(_C_RET, _C_WQ)
    cq = _rmsnorm(cq_in, gq_ref[...])
    ckv = _rmsnorm(ckv_in, gkv_ref[...])
    store(wq_ref, proj(_C_WQ, _C_WK), 0)
    store(wke_ref, proj(_C_WK, _C_WV), 0, wke32_ref if ctx else None)
    mq = _dot(cq.astype(BF16), wuq_ref[...])
    kvp = _dot(ckv.astype(BF16), wukv_ref[...])
    store(wve_ref, proj(_C_WV, _C_DQ), None, wve32_ref if ctx else None)
    store(dq_ref, proj(_C_DQ, _C_DK), 1)
    store(mq_ref, mq, 2)
    store(dk_ref, proj(_C_DK, _C_DV), 1, dk32_ref if ctx else None)
    store(dv_ref, proj(_C_DV, _C_MCQ), None, dv32_ref if ctx else None)
    mkr_rot = rot(mkr, 2)
    for hd in range(MLA_H):
        mk_ref[:, LANES * hd:LANES * (hd + 1)] = (kvp[:, LANES * hd:LANES * (hd + 1)] + mkr_rot).astype(BF16)
    mv_ref[...] = kvp[:, MLA_H * LANES:].astype(BF16)
    if ctx:
        ckv32_ref[...] = ckv
        mkr32_ref[...] = mkr


def _pre(x2d, mod, g_mix, w_in_p, w_uq_p, w_ukv_p, g_q, g_kv, tabs, *, seq_len, ctx, layer):
    t, d = x2d.shape
    tm = TM_DENSE
    tiles_per_seq = seq_len // tm
    rope = tabs is not None
    const = lambda i: (0, 0)
    row = lambda i: (i, 0)
    in_specs = [
        pl.BlockSpec((tm, d), row),
        _mod_spec(mod, tiles_per_seq),
        pl.BlockSpec((1, d), const),
        _resident(w_in_p.shape, layer),
        _resident(w_uq_p.shape, layer),
        _resident(w_ukv_p.shape, layer),
        pl.BlockSpec((1, MLA_QRANK), const),
        pl.BlockSpec((1, MLA_KVRANK), const),
    ]
    args = [x2d, mod, g_mix.reshape(1, d), w_in_p, w_uq_p, w_ukv_p,
            g_q.reshape(1, MLA_QRANK), g_kv.reshape(1, MLA_KVRANK)]
    if rope:
        in_specs.append(pl.BlockSpec((9, tm, LANES), lambda i: (0, i % tiles_per_seq, 0)))
        args.append(tabs)
    widths = [(1024, F32)] + [(256, BF16)] * 6 + [(512, BF16), (512, BF16), (256, BF16)]
    if ctx:
        widths += [(256, F32)] * 4 + [(128, F32)] * 2
    out_specs = [pl.BlockSpec((tm, w), row) for w, _ in widths]
    out_shape = [jax.ShapeDtypeStruct((t, w), dt) for w, dt in widths]
    return pl.pallas_call(
        functools.partial(_pre_kernel, rope=rope, ctx=ctx),
        grid=(t // tm,),
        in_specs=in_specs,
        out_specs=out_specs,
        out_shape=out_shape,
        compiler_params=_cparams(("arbitrary",)),
        name="pre_ctx" if ctx else "pre_lat",
    )(*args)


def _log_sigmoid(z):
    return jnp.minimum(z, 0.0) - jnp.log(1.0 + jnp.exp(-jnp.abs(z)))


def _ret_kernel(r_ref, s0_ref, dl_ref, ds_ref, dh_ref, o_ref, st_ref,
                s_sc, ob_sc, qdec_sc, kdec_sc, cdec_sc, intra_sc, *, n_chunks):
    p = pl.program_id(1)
    c = pl.program_id(2)
    fwd = p == 1
    chunk = RET_CHUNK
    w = BRANCH_W

    @pl.when(c == 0)
    def _init():
        lgl = _log_sigmoid(dl_ref[0])[0:1, :]
        i = lax.broadcasted_iota(jnp.int32, (chunk, w), 0).astype(F32)
        qe = jnp.where(fwd, i + 1.0, chunk - i)
        ke = jnp.where(fwd, chunk - 1.0 - i, i)
        qdec_sc[...] = jnp.exp(lgl * qe)
        kdec_sc[...] = jnp.exp(lgl * ke)
        rr = lax.broadcasted_iota(jnp.int32, (w, w), 0) // HEAD_W
        cc = lax.broadcasted_iota(jnp.int32, (w, w), 1) // HEAD_W
        cdec_sc[...] = jnp.where(rr == cc, jnp.exp(_log_sigmoid(ds_ref[0]) * float(chunk)), 0.0)
        ii = lax.broadcasted_iota(jnp.int32, (chunk, chunk), 0)
        jj = lax.broadcasted_iota(jnp.int32, (chunk, chunk), 1)
        dist = jnp.where(fwd, ii - jj, jj - ii)
        for hd in range(RET_H):
            lgh = _log_sigmoid(dh_ref[0, hd])[0:1, :]
            intra_sc[hd] = jnp.where(dist >= 0, jnp.exp(lgh * jnp.maximum(dist, 0).astype(F32)), 0.0)

    seqs = r_ref.shape[0]
    head_mask = [_lane_group_mask((chunk, w), HEAD_W, hd) for hd in range(RET_H)]
    rr = lax.broadcasted_iota(jnp.int32, (w, w), 0) // HEAD_W
    cc = lax.broadcasted_iota(jnp.int32, (w, w), 1) // HEAD_W
    block_diag = rr == cc
    outs = [None] * seqs

    def scan_step(sq):
        blk = r_ref[sq]
        q = blk[:, 0:w]
        k = blk[:, w:2 * w] * (RET_DK ** -0.5)
        v = blk[:, 2 * w:3 * w]
        qb = q.astype(BF16)
        kb = k.astype(BF16)
        vb = v.astype(BF16)
        s = jnp.where(c == 0, s0_ref[sq, 0], s_sc[sq])
        yield
        o = _dot(qb, s.astype(BF16)) * qdec_sc[...]
        for hd in range(RET_H):
            sc = _dot_nt(jnp.where(head_mask[hd], qb, jnp.zeros_like(qb)), kb) * intra_sc[hd]
            yield
            o = o + jnp.where(head_mask[hd], _dot(sc.astype(BF16), vb), 0.0)
        outs[sq] = o
        yield
        kd_t = jnp.transpose(k * kdec_sc[...]).astype(BF16)
        s_new = s * cdec_sc[...] + jnp.where(block_diag, _dot(kd_t, vb), 0.0)
        s_sc[sq] = s_new
        yield
        for hd in range(RET_H):
            st_ref[sq, 0, hd] = s_new[HEAD_W * hd:HEAD_W * (hd + 1), HEAD_W * hd:HEAD_W * (hd + 1)]

    def finish(sq):
        tot = outs[sq] + ob_sc[sq, c]
        mu = jnp.zeros_like(tot)
        for hd in range(RET_H):
            m1 = jnp.sum(jnp.where(head_mask[hd], tot, 0.0), axis=-1, keepdims=True) * (1.0 / HEAD_W)
            mu = jnp.where(head_mask[hd], m1, mu)
        yield
        xc = tot - mu
        var = jnp.zeros_like(tot)
        for hd in range(RET_H):
            v1 = jnp.sum(jnp.where(head_mask[hd], xc * xc, 0.0), axis=-1, keepdims=True) * (1.0 / HEAD_W)
            var = jnp.where(head_mask[hd], v1, var)
        yield
        g = r_ref[sq][:, 3 * w:4 * w]
        o_ref[sq] = (xc * lax.rsqrt(var + NORM_EPS) * (g * _sigmoid(g))).astype(o_ref.dtype)

    _round_robin([scan_step(sq) for sq in range(seqs)])

    @pl.when(p == 0)
    def _bwd():
        for sq in range(seqs):
            ob_sc[sq, n_chunks - 1 - c] = outs[sq]

    @pl.when(p == 1)
    def _fwd():
        _round_robin([finish(sq) for sq in range(seqs)])


def _retention(ret4, s0_bd, decay, *, batch, seq_len):
    n = seq_len // RET_CHUNK
    w = BRANCH_W
    seqs = RET_SEQS_PER_STEP if batch % RET_SEQS_PER_STEP == 0 else batch
    dlane = jnp.broadcast_to(jnp.repeat(decay, HEAD_W, axis=1)[:, None, :], (2, 8, w))
    dsub = jnp.broadcast_to(jnp.repeat(decay, HEAD_W, axis=1)[:, :, None], (2, w, w))
    dhead = jnp.broadcast_to(decay[:, :, None, None], (2, RET_H, 8, LANES))
    chunk_of = lambda p, c: c * p + (n - 1 - c) * (1 - p)
    o, st = pl.pallas_call(
        functools.partial(_ret_kernel, n_chunks=n),
        grid=(batch // seqs, 2, n),
        in_specs=[
            pl.BlockSpec((seqs, RET_CHUNK, 4 * w), lambda b, p, c: (b, chunk_of(p, c), 0)),
            pl.BlockSpec((seqs, 1, w, w), lambda b, p, c: (b, 1 - p, 0, 0)),
            pl.BlockSpec((1, 8, w), lambda b, p, c: (1 - p, 0, 0)),
            pl.BlockSpec((1, w, w), lambda b, p, c: (1 - p, 0, 0)),
            pl.BlockSpec((1, RET_H, 8, LANES), lambda b, p, c: (1 - p, 0, 0, 0)),
        ],
        out_specs=[
            pl.BlockSpec((seqs, RET_CHUNK, w), lambda b, p, c: (b, c * p, 0)),
            pl.BlockSpec((seqs, 1, RET_H, RET_DK, RET_DK), lambda b, p, c: (b, 1 - p, 0, 0, 0)),
        ],
        out_shape=[
            jax.ShapeDtypeStruct((batch, seq_len, w), BF16),
            jax.ShapeDtypeStruct((batch, 2, RET_H, RET_DK, RET_DK), F32),
        ],
        scratch_shapes=[
            pltpu.VMEM((seqs, w, w), F32),
            pltpu.VMEM((seqs, n, RET_CHUNK, w), F32),
            pltpu.VMEM((RET_CHUNK, w), F32),
            pltpu.VMEM((RET_CHUNK, w), F32),
            pltpu.VMEM((w, w), F32),
            pltpu.VMEM((RET_H, RET_CHUNK, RET_CHUNK), F32),
        ],
        compiler_params=_cparams(("arbitrary", "arbitrary", "arbitrary")),
        name="retention",
    )(ret4.reshape(batch, seq_len, 4 * w), s0_bd, dlane, dsub, dhead)
    return o.reshape(batch * seq_len, w), st


def _col_fold(acc, x, op):
    for j in range(x.shape[1] // LANES):
        acc = op(acc, x[:, LANES * j:LANES * (j + 1)])
    return acc


_ATTN_GROUPING = {"sink": (4, 1), "diff": (4, 2), "mla": (4, 1)}


def _attn_body(*refs, kind, n_chunks, kc, has_ctx, lam_init, static_chunks):
    it = iter(refs)
    q_ref, k_ref, v_ref = next(it), next(it), next(it)
    if has_ctx:
        if kind == "mla":
            cckv_ref, ckr_ref, wukv_ref = next(it), next(it), next(it)
        else:
            kx_ref, vx_ref = next(it), next(it)
    if kind == "sink":
        sink_ref = next(it)
    if kind == "diff":
        lam_ref, gd_ref = next(it), next(it)
    o_ref = next(it)
    s_all, m_all, l_all = next(it), next(it), next(it)
    qst_all = next(it) if kind != "mla" else None
    sx_all = next(it) if has_ctx else None
    if has_ctx and kind == "mla":
        kx_sc, vx_sc = next(it), next(it)

    tq = q_ref.shape[1]
    w = BRANCH_W
    nv, n_groups = _ATTN_GROUPING[kind]
    rows = nv * tq
    unroll = ATTN_UNROLL if n_chunks % ATTN_UNROLL == 0 else 1
    q = q_ref[0]
    scale = {"sink": WIN_HD ** -0.5, "diff": DIFF_D ** -0.5, "mla": (MLA_NOPE + MLA_ROPE) ** -0.5}[kind]

    if has_ctx and kind == "mla":
        kvc = _dot(cckv_ref[0].astype(BF16), wukv_ref[...])
        kr = ckr_ref[0]
        for hd in range(MLA_H):
            kx_sc[:, LANES * hd:LANES * (hd + 1)] = (kvc[:, LANES * hd:LANES * (hd + 1)] + kr).astype(BF16)
        vx_sc[...] = kvc[:, MLA_H * LANES:].astype(BF16)

    def rows_of(ref, ch):
        if static_chunks:
            return ref[0, ch * kc:(ch + 1) * kc, :]
        return ref[0, pl.ds(pl.multiple_of(ch * kc, kc), kc), :]

    def k_chunk(ch):
        return rows_of(k_ref, ch)

    def v_chunk(ch):
        return rows_of(v_ref, ch)

    def kx_tile():
        return kx_sc[...] if kind == "mla" else kx_ref[0]

    def vx_tile():
        return vx_sc[...] if kind == "mla" else vx_ref[0]

    def chunk_loop(body, carry):
        if static_chunks:
            for ch in range(n_chunks):
                carry = body(ch, carry)
                yield
        else:
            carry = lax.fori_loop(0, n_chunks, body, carry, unroll=unroll)
            yield
        return carry

    def weighted_values(make_weights, s_sc, sx_sc, n_rows):
        if not static_chunks:
            def body(ch, carry):
                return carry + _dot(make_weights(s_sc[ch]), v_chunk(ch))

            acc = lax.fori_loop(0, n_chunks, body, jnp.zeros((n_rows, w), F32), unroll=unroll)
            yield
            if has_ctx:
                acc = acc + _dot(make_weights(sx_sc[...]), vx_tile())
            return acc
        score_of = [functools.partial(lambda ch: s_sc[ch], ch) for ch in range(n_chunks)]
        value_of = [functools.partial(v_chunk, ch) for ch in range(n_chunks)]
        if has_ctx:
            score_of.append(lambda: sx_sc[...])
            value_of.append(vx_tile)
        acc = None
        ahead = make_weights(score_of[0]())
        for i in range(len(score_of)):
            cur = ahead
            if i + 1 < len(score_of):
                ahead = make_weights(score_of[i + 1]())
            term = _dot(cur, value_of[i]())
            acc = term if acc is None else acc + term
            yield
        return acc

    parts = []

    def group_body(grp):
        s_sc, m_sc, l_sc = s_all.at[grp], m_all.at[grp], l_all.at[grp]
        qst_sc = qst_all.at[grp] if kind != "mla" else None
        sx_sc = sx_all.at[grp] if has_ctx else None

        def scores(kt):
            if kind == "mla":
                heads = range(nv * grp, nv * (grp + 1))
                return jnp.concatenate(
                    [_dot_nt(q[:, LANES * hd:LANES * (hd + 1)], kt[:, LANES * hd:LANES * (hd + 1)])
                     for hd in heads], axis=0)
            return _dot_nt(qst_sc[...], kt)

        def probs(s):
            mb = m_sc[...]
            cols = [jnp.exp2(s[:, LANES * j:LANES * (j + 1)] * (scale * LOG2E) - mb)
                    for j in range(s.shape[1] // LANES)]
            tot = l_sc[...]
            for col in cols:
                tot = tot + col
            l_sc[...] = tot
            return jnp.concatenate(cols, axis=1)

        if kind != "mla":
            for v in range(nv):
                mask = (_lane_group_mask((tq, w), HEAD_W, v) if kind == "sink"
                        else _lane_group_mask((tq, w), DIFF_D, nv * grp + v))
                qst_sc[v * tq:(v + 1) * tq, :] = jnp.where(mask, q, jnp.zeros_like(q))
        yield

        m_sc[...] = jnp.full((rows, LANES), -jnp.inf, F32)

        def score_body(ch, carry):
            s = scores(k_chunk(ch))
            s_sc[ch] = s
            m_sc[...] = _col_fold(m_sc[...], s, jnp.maximum)
            return carry

        yield from chunk_loop(score_body, 0)
        if has_ctx:
            s = scores(kx_tile())
            sx_sc[...] = s
            m_sc[...] = _col_fold(m_sc[...], s, jnp.maximum)
        yield
        m = jnp.max(m_sc[...], axis=-1, keepdims=True) * scale
        if kind == "sink":
            sink_col = jnp.concatenate(
                [jnp.broadcast_to(sink_ref[v:v + 1, 0:1], (tq, 1)) for v in range(nv)], axis=0)
            m = jnp.maximum(m, sink_col)
        heads = range(nv * grp, nv * (grp + 1))
        m_sc[...] = jnp.broadcast_to(m * LOG2E, (rows, LANES))
        l_sc[...] = jnp.zeros((rows, LANES), F32)
        yield

        if kind != "diff":
            acc = yield from weighted_values(lambda s: probs(s).astype(BF16), s_sc, sx_sc, rows)
            yield
            l = jnp.sum(l_sc[...], axis=-1, keepdims=True)
            if kind == "sink":
                l = l + jnp.exp(sink_col - m)
            pv = acc * (1.0 / l)
            for v, hd in enumerate(heads):
                parts.append(jnp.where(_lane_group_mask((tq, w), HEAD_W, hd), pv[v * tq:(v + 1) * tq], 0.0))
        else:
            def exp_body(ch, carry):
                s_sc[ch] = probs(s_sc[ch])
                return carry

            yield from chunk_loop(exp_body, 0)
            if has_ctx:
                sx_sc[...] = probs(sx_sc[...])
            yield
            dl = lam_ref[...]
            lam = (jnp.exp(jnp.sum(dl[0:1] * dl[1:2], axis=-1, keepdims=True))
                   - jnp.exp(jnp.sum(dl[2:3] * dl[3:4], axis=-1, keepdims=True)) + lam_init)
            l = jnp.sum(l_sc[...], axis=-1, keepdims=True)
            second = (lax.broadcasted_iota(jnp.int32, (rows, 1), 0) // tq) % 2 == 1
            norm = jnp.where(second, lam, 1.0) * (1.0 / l)
            l_sc[...] = jnp.broadcast_to(norm, (rows, LANES))
            yield

            def weights(pr):
                nb = l_sc[...]
                cols = []
                for j in range(pr.shape[1] // LANES):
                    x = pr[:, LANES * j:LANES * (j + 1)] * nb
                    cols.append(jnp.concatenate([x[0:tq] - x[tq:2 * tq], x[2 * tq:3 * tq] - x[3 * tq:4 * tq]],
                                                axis=0))
                return jnp.concatenate(cols, axis=1).astype(BF16)

            acc = yield from weighted_values(weights, s_sc, sx_sc, 2 * tq)
            yield
            for v in range(2):
                parts.append(jnp.where(_lane_group_mask((tq, w), HEAD_W, 2 * grp + v), acc[v * tq:(v + 1) * tq], 0.0))

    lag = (n_chunks + 2) if static_chunks else 0
    yield from _interleave([group_body(grp) for grp in range(n_groups)], [lag * grp for grp in range(n_groups)])
    out = parts[0]
    for part in parts[1:]:
        out = out + part
    if kind == "diff":
        ms = jnp.zeros_like(out)
        for hd in range(DIFF_H):
            mh = _lane_group_mask((tq, w), HEAD_W, hd)
            m1 = jnp.sum(jnp.where(mh, out * out, 0.0), axis=-1, keepdims=True) * (1.0 / HEAD_W)
            ms = jnp.where(mh, m1, ms)
        out = out * lax.rsqrt(ms + NORM_EPS) * gd_ref[...] * (1.0 - lam_init)
    o_ref[0] = out.astype(o_ref.dtype)


def _attn_multi_kernel(*refs, seqs, batched, **kw):
    n_io = len(batched)
    bodies = []
    for sq in range(seqs):
        view = [r.at[pl.ds(sq, 1)] if flag else r for r, flag in zip(refs[:n_io], batched)]
        view += [r.at[sq] for r in refs[n_io:]]
        bodies.append(_attn_body(*view, **kw))
    _round_robin(bodies)


def _attention(kind, q, k, v, *, ctx_args=(), params=(), lam_init=0.0):
    b, n, wq = q.shape
    nk = k.shape[1]
    tq = min(TQ_ATTN, n)
    kc = min(KC_ATTN, nk)
    n_chunks = nk // kc
    has_ctx = len(ctx_args) > 0
    w = BRANCH_W
    nv, n_groups = _ATTN_GROUPING[kind]
    rows = nv * tq
    static_chunks = n_chunks > 1
    seqs = ATTN_SEQS_PER_STEP if (n == tq and b % ATTN_SEQS_PER_STEP == 0) else 1
    per_b = lambda bb, i: (bb, 0, 0)
    shared = lambda bb, i: (0, 0)
    in_specs = [
        pl.BlockSpec((seqs, tq, wq), lambda bb, i: (bb, i, 0)),
        pl.BlockSpec((seqs, nk, k.shape[2]), per_b),
        pl.BlockSpec((seqs, nk, w), per_b),
    ]
    args = [q, k, v]
    batched = [True, True, True]
    for a in ctx_args:
        if a.ndim == 3:
            in_specs.append(pl.BlockSpec((seqs,) + a.shape[1:], per_b))
        else:
            in_specs.append(pl.BlockSpec(a.shape, shared))
        batched.append(a.ndim == 3)
        args.append(a)
    for a in params:
        in_specs.append(pl.BlockSpec(a.shape, shared))
        batched.append(False)
        args.append(a)
    batched.append(True)
    scratch = [
        (n_groups, n_chunks, rows, kc, F32),
        (n_groups, rows, LANES, F32),
        (n_groups, rows, LANES, F32),
    ]
    if kind != "mla":
        scratch.append((n_groups, rows, w, BF16))
    if has_ctx:
        m_ctx = ctx_args[0].shape[1]
        scratch.append((n_groups, rows, m_ctx, F32))
        if kind == "mla":
            scratch += [(m_ctx, MLA_H * LANES, BF16), (m_ctx, w, BF16)]
    scratch_bytes = sum(seqs * math.prod(sh[:-1]) * jnp.dtype(sh[-1]).itemsize for sh in scratch)
    return pl.pallas_call(
        functools.partial(_attn_multi_kernel, seqs=seqs, batched=tuple(batched), kind=kind, n_chunks=n_chunks,
                          kc=kc, has_ctx=has_ctx, lam_init=lam_init, static_chunks=static_chunks),
        grid=(b // seqs, n // tq),
        in_specs=in_specs,
        out_specs=pl.BlockSpec((seqs, tq, w), lambda bb, i: (bb, i, 0)),
        out_shape=jax.ShapeDtypeStruct((b, n, w), BF16),
        scratch_shapes=[pltpu.VMEM((seqs,) + sh[:-1], sh[-1]) for sh in scratch],
        compiler_params=_cparams(("arbitrary", "arbitrary"), vmem=max(VMEM_LIMIT, scratch_bytes + ATTN_VMEM_EXTRA)),
        name="attn_" + kind + ("_lat" if has_ctx else "_ctx"),
    )(*args)


def _win_kernel(q_ref, kp_ref, kc_ref, kn_ref, vp_ref, vc_ref, vn_ref, kx_ref, vx_ref, sink_ref, o_ref):
    n_steps = pl.num_programs(1)
    step = pl.program_id(1)
    tq = WINDOW
    nblk = q_ref.shape[1] // tq
    w = BRANCH_W
    nh = 4
    rows = nh * tq
    scale = WIN_HD ** -0.5
    head_mask = [_lane_group_mask((tq, w), HEAD_W, hd) for hd in range(nh)]
    ii = lax.broadcasted_iota(jnp.int32, (rows, tq), 0) % tq
    jj = lax.broadcasted_iota(jnp.int32, (rows, tq), 1)
    sink_col = jnp.concatenate([jnp.broadcast_to(sink_ref[hd:hd + 1, 0:1], (tq, 1)) for hd in range(nh)], axis=0)
    kx = kx_ref[0]
    vx = vx_ref[0]

    def rows_of(ref, blk):
        return ref[0, blk * tq:(blk + 1) * tq, :]

    def block(blk):
        q = rows_of(q_ref, blk)
        qst = jnp.concatenate([jnp.where(head_mask[hd], q, jnp.zeros_like(q)) for hd in range(nh)], axis=0)
        kp, vp = (kp_ref[0], vp_ref[0]) if blk == 0 else (rows_of(kc_ref, blk - 1), rows_of(vc_ref, blk - 1))
        kn, vn = (kn_ref[0], vn_ref[0]) if blk == nblk - 1 else (rows_of(kc_ref, blk + 1), rows_of(vc_ref, blk + 1))
        mask_prev = jj >= ii
        mask_next = jj <= ii
        if blk == 0:
            mask_prev = mask_prev & (step > 0)
        if blk == nblk - 1:
            mask_next = mask_next & (step < n_steps - 1)
        yield
        sp = jnp.where(mask_prev, _dot_nt(qst, kp) * scale, NEG_INF)
        sc = _dot_nt(qst, rows_of(kc_ref, blk)) * scale
        sn = jnp.where(mask_next, _dot_nt(qst, kn) * scale, NEG_INF)
        sx = _dot_nt(qst, kx) * scale
        yield
        m = jnp.maximum(jnp.maximum(jnp.max(sp, axis=-1, keepdims=True), jnp.max(sc, axis=-1, keepdims=True)),
                        jnp.maximum(jnp.max(sn, axis=-1, keepdims=True), jnp.max(sx, axis=-1, keepdims=True)))
        m = jnp.maximum(m, sink_col)
        yield
        pp, pc, pn, px = (jnp.exp(s - m) for s in (sp, sc, sn, sx))
        l = (jnp.sum(pp, axis=-1, keepdims=True) + jnp.sum(pc, axis=-1, keepdims=True)
             + jnp.sum(pn, axis=-1, keepdims=True) + jnp.sum(px, axis=-1, keepdims=True) + jnp.exp(sink_col - m))
        yield
        pv = (_dot(pp.astype(BF16), vp) + _dot(pc.astype(BF16), rows_of(vc_ref, blk))
              + _dot(pn.astype(BF16), vn) + _dot(px.astype(BF16), vx)) * (1.0 / l)
        yield
        out = jnp.zeros((tq, w), F32)
        for hd in range(nh):
            out = out + jnp.where(head_mask[hd], pv[hd * tq:(hd + 1) * tq], 0.0)
        o_ref[0, blk * tq:(blk + 1) * tq, :] = out.astype(o_ref.dtype)

    _round_robin([block(blk) for blk in range(nblk)])


def _window_attention(q, k, v, kx, vx, sink_tile):
    b, n, w = q.shape
    tq = WINDOW
    span = WIN_BLOCKS_PER_STEP * tq
    nb = n // tq
    m_ctx = kx.shape[1]
    edge = lambda f: pl.BlockSpec((1, tq, w), f)
    prev = lambda bb, i: (bb, jnp.maximum(i * WIN_BLOCKS_PER_STEP - 1, 0), 0)
    nxt = lambda bb, i: (bb, jnp.minimum((i + 1) * WIN_BLOCKS_PER_STEP, nb - 1), 0)
    mid = pl.BlockSpec((1, span, w), lambda bb, i: (bb, i, 0))
    per_b = lambda bb, i: (bb, 0, 0)
    return pl.pallas_call(
        _win_kernel,
        grid=(b, n // span),
        in_specs=[mid, edge(prev), mid, edge(nxt), edge(prev), mid, edge(nxt),
                  pl.BlockSpec((1, m_ctx, w), per_b), pl.BlockSpec((1, m_ctx, w), per_b),
                  pl.BlockSpec(sink_tile.shape, lambda bb, i: (0, 0))],
        out_specs=mid,
        out_shape=jax.ShapeDtypeStruct((b, n, w), BF16),
        compiler_params=_cparams(("arbitrary", "arbitrary")),
        name="attn_window",
    )(q, k, k, k, v, v, v, kx, vx, sink_tile)


def _post_kernel(x_ref, mod_ref, g_ref, o0_ref, o1_ref, o2_ref, o3_ref,
                 wg_ref, bg_ref, wb_ref, wo_ref, out_ref, mix_sc):
    d = D_MODEL
    x = x_ref[...]
    mod = mod_ref[0]
    h = _rmsnorm(x, g_ref[...]) * (1.0 + mod[1:2]) + mod[0:1]
    hb = h.astype(BF16)
    cw = 256
    for j in range(d // cw):
        mixed = None
        for nbr, o_ref in enumerate((o0_ref, o1_ref, o2_ref, o3_ref)):
            lo = nbr * d + j * cw
            gate = _sigmoid(_dot(hb, wg_ref[:, lo:lo + cw]) + bg_ref[:, lo:lo + cw])
            term = gate * _dot(o_ref[...], wb_ref[nbr, :, j * cw:(j + 1) * cw])
            mixed = term if mixed is None else mixed + term
        mix_sc[:, j * cw:(j + 1) * cw] = mixed.astype(BF16)
    out_ref[...] = x + mod[2:3] * _dot(mix_sc[...], wo_ref[...])


def _post(x2d, mod, g_mix, outs, w_gate, b_gate, w_branch, w_out, *, seq_len, layer):
    t, d = x2d.shape
    tm = TM_DENSE
    tiles_per_seq = seq_len // tm
    const2 = lambda i: (0, 0)
    row = lambda i: (i, 0)
    return pl.pallas_call(
        _post_kernel,
        grid=(t // tm,),
        in_specs=[
            pl.BlockSpec((tm, d), row),
            _mod_spec(mod, tiles_per_seq),
            pl.BlockSpec((1, d), const2),
        ] + [pl.BlockSpec((tm, BRANCH_W), row)] * 4 + [
            _resident(w_gate.shape, layer),
            pl.BlockSpec((1, 4 * d), const2),
            _resident(w_branch.shape, layer),
            _resident(w_out.shape, layer),
        ],
        out_specs=pl.BlockSpec((tm, d), row),
        out_shape=jax.ShapeDtypeStruct((t, d), F32),
        scratch_shapes=[pltpu.VMEM((tm, d), BF16)],
        compiler_params=_cparams(("arbitrary",)),
        name="post",
    )(x2d, mod, g_mix.reshape(1, d), *outs, w_gate, b_gate.reshape(1, 4 * d), w_branch, w_out)


def _ffn_kernel(*refs, tiles_per_seq, final):
    halo = tiles_per_seq > 1
    it = iter(refs)
    x_ref = next(it)
    if halo:
        xp_ref, xn_ref = next(it), next(it)
    mod_ref, g_ref, wup_ref, bup_ref, wcv_ref, bcv_ref, wdn_ref = (next(it) for _ in range(7))
    gf_ref = next(it) if final else None
    out_ref = next(it)
    h_sc, u_sc, acc_sc = next(it), next(it), next(it)

    tm = x_ref.shape[0]
    hr = CONV_HALO
    n_ff = D_FF // FF_CHUNK
    mod = mod_ref[0]
    g = g_ref[...]

    def norm_mod(rows):
        return (_rmsnorm(rows, g) * (1.0 + mod[4:5]) + mod[3:4]).astype(BF16)

    x = x_ref[...]
    h_sc[hr:hr + tm, :] = norm_mod(x)
    if halo:
        h_sc[0:hr, :] = norm_mod(xp_ref[...])
        h_sc[hr + tm:, :] = norm_mod(xn_ref[...])
        i = pl.program_id(0)
        keep_lo = jnp.where((i % tiles_per_seq) == 0, 0.0, 1.0)
        keep_hi = jnp.where((i % tiles_per_seq) == tiles_per_seq - 1, 0.0, 1.0)
        rid = lax.broadcasted_iota(jnp.int32, (8, 1), 0)
        edge_lo = jnp.where(rid == 7, keep_lo, 1.0)
        edge_hi = jnp.where(rid == 0, keep_hi, 1.0)
    else:
        for slot in range(4):
            u_sc[slot, hr - 8:hr, :] = jnp.zeros((8, FF_CHUNK), F32)
            u_sc[slot, hr + tm:hr + tm + 8, :] = jnp.zeros((8, FF_CHUNK), F32)

    def up(ch, slot):
        lo = ch * FF_CHUNK
        if halo:
            u = _dot(h_sc[...], wup_ref[:, lo:lo + FF_CHUNK]) + bup_ref[:, lo:lo + FF_CHUNK]
            u_sc[slot] = u
            u_sc[slot, hr - 8:hr, :] = u[hr - 8:hr] * edge_lo
            u_sc[slot, hr + tm:hr + tm + 8, :] = u[hr + tm:hr + tm + 8] * edge_hi
        else:
            u_sc[slot, hr:hr + tm, :] = (_dot(h_sc[hr:hr + tm, :], wup_ref[:, lo:lo + FF_CHUNK])
                                         + bup_ref[:, lo:lo + FF_CHUNK])

    def conv(ch, slot, r0, r1):
        lo = ch * FF_CHUNK
        wc = wcv_ref[:, lo:lo + FF_CHUNK]
        return (bcv_ref[:, lo:lo + FF_CHUNK] + wc[0:1] * u_sc[slot, hr - 1 + r0:hr - 1 + r1, :]
                + wc[1:2] * u_sc[slot, hr + r0:hr + r1, :] + wc[2:3] * u_sc[slot, hr + 1 + r0:hr + 1 + r1, :])

    def activation(c, pair, r0, r1):
        a = conv(c, pair, r0, r1)
        gg = conv(c + n_ff, pair + 1, r0, r1)
        return (gg * _sigmoid(gg) * a).astype(BF16)

    up(0, 0)
    up(n_ff, 1)
    act = None
    for c in range(n_ff + 1):
        pair = 2 * (c % 2)
        if c + 1 < n_ff:
            up(c + 1, 2 - pair)
            up(c + 1 + n_ff, 3 - pair)
        if c >= 1:
            term = _dot(act, wdn_ref[(c - 1) * FF_CHUNK:c * FF_CHUNK, :])
            acc = term if c == 1 else acc + term
        if c < n_ff:
            act = activation(c, pair, 0, tm)
    y = x + mod[5:6] * acc
    if final:
        y = _rmsnorm(y, gf_ref[...])
    out_ref[...] = y


def _ffn(x2d, mod, g_ffn, w_up, b_up, w_conv, b_conv, w_down, g_final, *, seq_len, layer):
    t, d = x2d.shape
    tm = min(TM_FFN, seq_len)
    hr = CONV_HALO
    tiles_per_seq = seq_len // tm
    halo = tiles_per_seq > 1
    final = g_final is not None
    hb = tm // hr
    n_hblocks = t // hr
    const2 = lambda i: (0, 0)
    row = lambda i: (i, 0)
    in_specs = [pl.BlockSpec((tm, d), row)]
    args = [x2d]
    if halo:
        in_specs += [pl.BlockSpec((hr, d), lambda i: (jnp.maximum(i * hb - 1, 0), 0)),
                     pl.BlockSpec((hr, d), lambda i: (jnp.minimum((i + 1) * hb, n_hblocks - 1), 0))]
        args += [x2d, x2d]
    in_specs += [
        _mod_spec(mod, tiles_per_seq),
        pl.BlockSpec((1, d), const2),
        _resident(w_up.shape, layer),
        pl.BlockSpec(b_up.shape, const2),
        pl.BlockSpec(w_conv.shape, const2),
        pl.BlockSpec(b_conv.shape, const2),
        _resident(w_down.shape, layer),
    ]
    args += [mod, g_ffn.reshape(1, d), w_up, b_up, w_conv, b_conv, w_down]
    if final:
        in_specs.append(pl.BlockSpec((1, d), const2))
        args.append(g_final.reshape(1, d))
    return pl.pallas_call(
        functools.partial(_ffn_kernel, tiles_per_seq=tiles_per_seq, final=final),
        grid=(t // tm,),
        in_specs=in_specs,
        out_specs=pl.BlockSpec((tm, d), row),
        out_shape=jax.ShapeDtypeStruct((t, d), F32),
        scratch_shapes=[
            pltpu.VMEM((tm + 2 * hr, d), BF16),
            pltpu.VMEM((4, tm + 2 * hr, FF_CHUNK), F32),
            pltpu.VMEM((tm, d), F32),
        ],
        compiler_params=_cparams(("arbitrary",)),
        name=("ffn_final" if final else "ffn") + ("_lat" if halo else "_ctx"),
    )(*args)


def _rope_tables(n_tok):
    rows = n_tok // GRID_W

    def axis(n_pos, half):
        freqs = ROPE_BASE ** (-jnp.arange(half, dtype=F32) / half)
        ang = jnp.arange(n_pos).astype(F32)[:, None] * freqs[None, :]
        return jnp.cos(ang), jnp.sin(ang)

    def unit(width):
        half = width // 4
        cr, sr = (jnp.repeat(t, GRID_W, axis=0) for t in axis(rows, half))
        cc, sc = (jnp.tile(t, (rows, 1)) for t in axis(GRID_W, half))
        z = jnp.zeros_like(sr)
        return (jnp.concatenate([cr, cr, cc, cc], axis=1),
                jnp.concatenate([-sr, z, -sc, z], axis=1),
                jnp.concatenate([z, sr, z, sc], axis=1))

    a = [jnp.tile(t, (1, 2)) for t in unit(64)]
    b = [jnp.tile(t, (1, 4)) for t in unit(32)]
    cu = unit(32)
    ones = jnp.ones((n_tok, 64), F32)
    z64 = jnp.zeros((n_tok, 64), F32)
    z32 = jnp.zeros((n_tok, 32), F32)
    c = [jnp.concatenate([ones, cu[0], ones[:, :32]], axis=1),
         jnp.concatenate([z64, cu[1], z32], axis=1),
         jnp.concatenate([z64, cu[2], z32], axis=1)]
    return jnp.stack(a + b + c, axis=0)


def _dup_groups(w):
    g0, g1 = w[..., :HEAD_W], w[..., HEAD_W:]
    return jnp.concatenate([g0, g0, g1, g1], axis=-1)


def _prep_weights(p):
    depth = p["w_in"].shape[0]
    cuts = np.cumsum((0,) + IN_SPLITS)
    parts = [p["w_in"][:, :, cuts[i]:cuts[i + 1]] for i in range(len(IN_SPLITS))]
    rq, rk, rv, rg, wq, wk, wv, dq, dk, dv, mcq, mckv, mkr = parts
    mkr128 = jnp.pad(mkr, ((0, 0), (0, 0), (MLA_NOPE, LANES - MLA_NOPE - MLA_ROPE)))
    w_in_p = jnp.concatenate([rq, rk, rv, rg, wq, _dup_groups(wk), _dup_groups(wv), dq, dk, dv,
                              mcq, mckv, mkr128], axis=2).astype(BF16)
    w_uq = p["w_mla_uq"].reshape(depth, MLA_QRANK, MLA_H, MLA_NOPE + MLA_ROPE)
    w_uq_p = jnp.pad(w_uq, ((0, 0), (0, 0), (0, 0), (0, LANES - MLA_NOPE - MLA_ROPE)))
    w_ukv = p["w_mla_ukv"].reshape(depth, MLA_KVRANK, MLA_H, MLA_NOPE + MLA_V)
    w_uk = jnp.pad(w_ukv[..., :MLA_NOPE], ((0, 0), (0, 0), (0, 0), (0, LANES - MLA_NOPE)))
    w_uv = w_ukv[..., MLA_NOPE:].reshape(depth, MLA_KVRANK, MLA_H * MLA_V)
    sink = jnp.broadcast_to(p["win_sink"].astype(F32)[:, :, None], (depth, 4, LANES))
    return dict(
        w_in_p=w_in_p,
        w_uq_p=w_uq_p.reshape(depth, MLA_QRANK, MLA_H * LANES).astype(BF16),
        w_ukv_p=jnp.concatenate([w_uk.reshape(depth, MLA_KVRANK, MLA_H * LANES), w_uv], axis=2).astype(BF16),
        w_gate=p["w_gate"].astype(BF16),
        w_branch=p["w_branch"].astype(BF16),
        w_out=p["w_out"].astype(BF16),
        w_up=p["w_up"].astype(BF16),
        w_down=p["w_down"].astype(BF16),
        sink_tile8=jnp.pad(sink, ((0, 0), (0, 4), (0, 0))),
        g_diff4=jnp.tile(p["g_diff"], (1, DIFF_H)).reshape(depth, 1, BRANCH_W),
    )


def _block_diag_state(s):
    b = s.shape[0]
    eye = jnp.eye(RET_H, dtype=s.dtype)
    bd = s[:, :, :, :, None, :] * eye[None, None, :, None, :, None]
    return bd.reshape(b, 2, RET_H * RET_DK, RET_H * RET_DK)


def _mod_rows(mods_l, start, count):
    m = mods_l[start:start + count].reshape(count, 6, D_MODEL)
    return jnp.pad(m, ((0, 0), (0, 2), (0, 0)))


def kernel(x_prompt, x_sample, state_ret, cache_win_k, cache_win_v, cache_diff_k, cache_diff_v, cache_mla_ckv, cache_mla_krope, c, c_ctx, w_mod, b_mod, g_mix, w_in, ret_decay, win_sink, diff_lambda, g_diff, g_mla_q, w_mla_uq, g_mla_kv, w_mla_ukv, w_branch, w_gate, b_gate, w_out, g_ffn, w_up, b_up, w_conv, b_conv, w_down, g_final):
    d = D_MODEL
    bc, lc, _ = x_prompt.shape
    bl, ll, _ = x_sample.shape
    m_ctx = cache_win_k.shape[2]

    cond_rows = jnp.zeros((8, d), F32).at[0].set(c_ctx).at[1:1 + bl].set(c)
    mods = _modulation(cond_rows, w_mod, b_mod)
    tabs = _rope_tables(ll)

    xp = x_prompt.reshape(bc * lc, d)
    xs = x_sample.reshape(bl * ll, d)
    produced = [[] for _ in range(7)]
    p = _prep_weights(dict(w_in=w_in, w_mla_uq=w_mla_uq, w_mla_ukv=w_mla_ukv, w_gate=w_gate, w_branch=w_branch,
                           w_out=w_out, w_up=w_up, w_down=w_down, win_sink=win_sink, g_diff=g_diff))
    zero_state = jnp.zeros((bc, 2, BRANCH_W, BRANCH_W), F32)
    for li in range(DEPTH):
        sink_tile8, g_diff4 = p["sink_tile8"][li], p["g_diff4"][li]
        ffn_small = (b_up[li].reshape(1, -1), w_conv[li], b_conv[li].reshape(1, -1))
        lam_init = 0.8 - 0.6 * math.exp(-0.3 * li)
        mod_c = _mod_rows(mods[li], 0, 1)
        mod_l = _mod_rows(mods[li], 1, bl)
        final_g = g_final if li == DEPTH - 1 else None

        (ret4, wq, wke, wve, dq, dk, dv, mq, mk, mv,
         wke32, wve32, dk32, dv32, ckv32, mkr32) = _pre(
            xp, mod_c, g_mix[li], p["w_in_p"], p["w_uq_p"], p["w_ukv_p"], g_mla_q[li], g_mla_kv[li], None,
            seq_len=lc, ctx=True, layer=li)
        o_ret, st = _retention(ret4, zero_state, ret_decay[li], batch=bc, seq_len=lc)
        r3 = lambda a: a.reshape(bc, lc, a.shape[-1])
        o_win = _attention("sink", r3(wq), r3(wke), r3(wve), params=(sink_tile8,))
        o_diff = _attention("diff", r3(dq), r3(dk), r3(dv), params=(diff_lambda[li], g_diff4), lam_init=lam_init)
        o_mla = _attention("mla", r3(mq), r3(mk), r3(mv))
        f2 = lambda a: a.reshape(bc * lc, BRANCH_W)
        xp = _post(xp, mod_c, g_mix[li], (o_ret, f2(o_win), f2(o_diff), f2(o_mla)),
                   p["w_gate"], b_gate[li], p["w_branch"], p["w_out"], seq_len=lc, layer=li)
        xp = _ffn(xp, mod_c, g_ffn[li], p["w_up"], *ffn_small, p["w_down"], final_g, seq_len=lc, layer=li)
        undup = lambda a: a.reshape(bc, lc, 2, 2, HEAD_W)[:, :, :, 0, :]
        produced[0].append(st)
        produced[1].append(undup(wke32))
        produced[2].append(undup(wve32))
        produced[3].append(dk32.reshape(bc, lc, DIFF_H, 2 * DIFF_D))
        produced[4].append(dv32.reshape(bc, lc, DIFF_H, 2 * DIFF_D))
        produced[5].append(ckv32.reshape(bc, lc, MLA_KVRANK))
        produced[6].append(mkr32.reshape(bc, lc, LANES)[:, :, MLA_NOPE:MLA_NOPE + MLA_ROPE])

        (ret4, wq, wke, wve, dq, dk, dv, mq, mk, mv) = _pre(
            xs, mod_l, g_mix[li], p["w_in_p"], p["w_uq_p"], p["w_ukv_p"], g_mla_q[li], g_mla_kv[li], tabs,
            seq_len=ll, ctx=False, layer=li)
        o_ret, _ = _retention(ret4, _block_diag_state(state_ret[:, li]), ret_decay[li], batch=bl, seq_len=ll)
        r3 = lambda a: a.reshape(bl, ll, a.shape[-1])
        kx_win = _dup_groups(cache_win_k[:, li].reshape(bl, m_ctx, 2 * HEAD_W)).astype(BF16)
        vx_win = _dup_groups(cache_win_v[:, li].reshape(bl, m_ctx, 2 * HEAD_W)).astype(BF16)
        o_win = _window_attention(r3(wq), r3(wke), r3(wve), kx_win, vx_win, sink_tile8)
        kx_diff = cache_diff_k[:, li].reshape(bl, m_ctx, BRANCH_W).astype(BF16)
        vx_diff = cache_diff_v[:, li].reshape(bl, m_ctx, BRANCH_W).astype(BF16)
        o_diff = _attention("diff", r3(dq), r3(dk), r3(dv), ctx_args=(kx_diff, vx_diff),
                            params=(diff_lambda[li], g_diff4), lam_init=lam_init)
        kr128 = jnp.pad(cache_mla_krope[:, li], ((0, 0), (0, 0), (MLA_NOPE, LANES - MLA_NOPE - MLA_ROPE)))
        o_mla = _attention("mla", r3(mq), r3(mk), r3(mv), ctx_args=(cache_mla_ckv[:, li], kr128, p["w_ukv_p"][li]))
        f2 = lambda a: a.reshape(bl * ll, BRANCH_W)
        xs = _post(xs, mod_l, g_mix[li], (o_ret, f2(o_win), f2(o_diff), f2(o_mla)),
                   p["w_gate"], b_gate[li], p["w_branch"], p["w_out"], seq_len=ll, layer=li)
        xs = _ffn(xs, mod_l, g_ffn[li], p["w_up"], *ffn_small, p["w_down"], final_g, seq_len=ll, layer=li)

    y_prompt = xp.reshape(bc, lc, d)
    y_sample = xs.reshape(bl, ll, d)
    stack = lambda lst: jnp.stack(lst, axis=1)
    return (y_prompt, y_sample, stack(produced[0]),
            stack(produced[1]), stack(produced[2]), stack(produced[3]), stack(produced[4]),
            stack(produced[5]), stack(produced[6]))
```

```python
import functools
import math

import numpy as np
import jax
import jax.numpy as jnp
from jax import lax
from jax.experimental import pallas as pl
from jax.experimental.pallas import tpu as pltpu

F32 = jnp.float32
BF16 = jnp.bfloat16

D_MODEL = 1024
DEPTH = 2
GRID_W = 64
ROPE_BASE = 10000.0
NORM_EPS = 1e-6
NEG_INF = -1e30
LOG2E = 1.4426950408889634
RET_H = 4
RET_DK = 64
RET_CHUNK = 128
WIN_HD = 64
WINDOW = 128
DIFF_H = 4
DIFF_D = 32
MLA_H = 4
MLA_NOPE = 64
MLA_ROPE = 32
MLA_V = 64
MLA_QRANK = 256
MLA_KVRANK = 128
D_FF = 2816
IN_SPLITS = (256, 256, 256, 256, 256, 128, 128, 256, 256, 256, 256, 128, 32)

LANES = 128
BF16_ROWS = 16
BRANCH_W = 256
HEAD_W = 64

_C_RET, _C_WQ, _C_WK, _C_WV, _C_DQ, _C_DK, _C_DV, _C_MCQ, _C_MCKV, _C_MKR, _C_END = (
    0, 1024, 1280, 1536, 1792, 2048, 2304, 2560, 2816, 2944, 3072)

TM_DENSE = 256
TM_FFN = 256
FF_CHUNK = 256
TQ_ATTN = 256
KC_ATTN = 512
CONV_HALO = BF16_ROWS
RET_SEQS_PER_STEP = 8
WIN_BLOCKS_PER_STEP = 4
ATTN_SEQS_PER_STEP = 4
ATTN_UNROLL = 4
VMEM_LIMIT = 48 * 1024 * 1024
ATTN_VMEM_EXTRA = 16 * 1024 * 1024


def _cparams(sem, vmem=VMEM_LIMIT):
    return pltpu.CompilerParams(dimension_semantics=sem, vmem_limit_bytes=vmem)


def _dot(a, b):
    return jnp.dot(a, b, preferred_element_type=F32)


def _dot_nt(a, b):
    return lax.dot_general(a, b, (((1,), (1,)), ((), ())), preferred_element_type=F32)


def _sigmoid(x):
    return 1.0 / (1.0 + jnp.exp(-x))


def _rmsnorm(x, gain):
    ms = jnp.mean(x * x, axis=-1, keepdims=True)
    return x * lax.rsqrt(ms + NORM_EPS) * gain


def _interleave(bodies, delays):
    active = list(zip(bodies, delays))
    rnd = 0
    while active:
        active = [(body, d) for body, d in active if d > rnd or next(body, "done") != "done"]
        rnd += 1
        yield


def _round_robin(bodies):
    bodies = list(bodies)
    for _ in _interleave(bodies, [0] * len(bodies)):
        pass


def _resident(stacked_shape, layer):
    zeros = (0,) * (len(stacked_shape) - 1)
    return pl.BlockSpec((None,) + tuple(stacked_shape[1:]), lambda i: (layer,) + zeros,
                        pipeline_mode=pl.Buffered(1))


def _mod_spec(mod, tiles_per_seq):
    blk = (1,) + mod.shape[1:]
    if mod.shape[0] == 1:
        return pl.BlockSpec(blk, lambda i: (0, 0, 0))
    return pl.BlockSpec(blk, lambda i: (i // tiles_per_seq, 0, 0))


def _lane_group_mask(shape, group_width, group):
    lane = lax.broadcasted_iota(jnp.int32, shape, len(shape) - 1)
    return (lane // group_width) == group


def _rope_rows(x, cos, sin_up, sin_dn, half):
    width = x.shape[-1]
    up = pltpu.roll(x, width - half, axis=1)
    dn = pltpu.roll(x, half, axis=1)
    return x * cos + up * sin_up + dn * sin_dn


def _mod_kernel(c_ref, w_ref, b_ref, o_ref):
    c = c_ref[...]
    s = c * _sigmoid(c)
    o_ref[0] = _dot(s.astype(BF16), w_ref[0].astype(BF16)) + b_ref[0]


def _modulation(cond_rows, w_mod, b_mod):
    depth, d, n = w_mod.shape
    tn = 1024
    return pl.pallas_call(
        _mod_kernel,
        grid=(depth, n // tn),
        in_specs=[
            pl.BlockSpec((8, d), lambda l, j: (0, 0)),
            pl.BlockSpec((1, d, tn), lambda l, j: (l, 0, j)),
            pl.BlockSpec((1, 1, tn), lambda l, j: (l, 0, j)),
        ],
        out_specs=pl.BlockSpec((1, 8, tn), lambda l, j: (l, 0, j)),
        out_shape=jax.ShapeDtypeStruct((depth, 8, n), F32),
        compiler_params=_cparams(("arbitrary", "arbitrary")),
        name="modulation",
    )(cond_rows, w_mod, b_mod.reshape(depth, 1, n))


_ROPE_HALF = (16, 8, 8)


def _pre_kernel(*refs, rope, ctx):
    it = iter(refs)
    x_ref, mod_ref, g_ref, w_ref, wuq_ref, wukv_ref, gq_ref, gkv_ref = (next(it) for _ in range(8))
    tab_ref = next(it) if rope else None
    (ret_ref, wq_ref, wke_ref, wve_ref, dq_ref, dk_ref, dv_ref,
     mq_ref, mk_ref, mv_ref) = (next(it) for _ in range(10))
    if ctx:
        wke32_ref, wve32_ref, dk32_ref, dv32_ref, ckv32_ref, mkr32_ref = (next(it) for _ in range(6))

    x = x_ref[...]
    mod = mod_ref[0]
    h = _rmsnorm(x, g_ref[...]) * (1.0 + mod[1:2]) + mod[0:1]
    hb = h.astype(BF16)

    def proj(a, b):
        return _dot(hb, w_ref[:, a:b])

    def rot(v, t):
        if not rope:
            return v
        return _rope_rows(v, tab_ref[3 * t], tab_ref[3 * t + 1], tab_ref[3 * t + 2], _ROPE_HALF[t])

    def store(o_ref, val, t=None, o32_ref=None):
        for j in range(val.shape[1] // LANES):
            v = val[:, LANES * j:LANES * (j + 1)]
            if o32_ref is not None:
                o32_ref[:, LANES * j:LANES * (j + 1)] = v
            if t is not None:
                v = rot(v, t)
            o_ref[:, LANES * j:LANES * (j + 1)] = v.astype(o_ref.dtype)

    cq_in = proj(_C_MCQ, _C_MCKV)
    ckv_in = proj(_C_MCKV, _C_MKR)
    mkr = proj(_C_MKR, _C_END)
    ret_ref[...] = proj(_C_RET, _C_WQ)
    cq = _rmsnorm(cq_in, gq_ref[...])
    ckv = _rmsnorm(ckv_in, gkv_ref[...])
    store(wq_ref, proj(_C_WQ, _C_WK), 0)
    store(wke_ref, proj(_C_WK, _C_WV), 0, wke32_ref if ctx else None)
    mq = _dot(cq.astype(BF16), wuq_ref[...]) * ((MLA_NOPE + MLA_ROPE) ** -0.5 * LOG2E)
    kvp = _dot(ckv.astype(BF16), wukv_ref[...])
    store(wve_ref, proj(_C_WV, _C_DQ), None, wve32_ref if ctx else None)
    store(dq_ref, proj(_C_DQ, _C_DK) * (DIFF_D ** -0.5 * LOG2E), 1)
    store(mq_ref, mq, 2)
    store(dk_ref, proj(_C_DK, _C_DV), 1, dk32_ref if ctx else None)
    store(dv_ref, proj(_C_DV, _C_MCQ), None, dv32_ref if ctx else None)
    mkr_rot = rot(mkr, 2)
    for hd in range(MLA_H):
        mk_ref[:, LANES * hd:LANES * (hd + 1)] = (kvp[:, LANES * hd:LANES * (hd + 1)] + mkr_rot).astype(BF16)
    mv_ref[...] = kvp[:, MLA_H * LANES:].astype(BF16)
    if ctx:
        ckv32_ref[...] = ckv
        mkr32_ref[...] = mkr


def _pre(x2d, mod, g_mix, w_in_p, w_uq_p, w_ukv_p, g_q, g_kv, tabs, *, seq_len, ctx, layer):
    t, d = x2d.shape
    tm = TM_DENSE
    tiles_per_seq = seq_len // tm
    rope = tabs is not None
    const = lambda i: (0, 0)
    row = lambda i: (i, 0)
    in_specs = [
        pl.BlockSpec((tm, d), row),
        _mod_spec(mod, tiles_per_seq),
        pl.BlockSpec((1, d), const),
        _resident(w_in_p.shape, layer),
        _resident(w_uq_p.shape, layer),
        _resident(w_ukv_p.shape, layer),
        pl.BlockSpec((1, MLA_QRANK), const),
        pl.BlockSpec((1, MLA_KVRANK), const),
    ]
    args = [x2d, mod, g_mix.reshape(1, d), w_in_p, w_uq_p, w_ukv_p,
            g_q.reshape(1, MLA_QRANK), g_kv.reshape(1, MLA_KVRANK)]
    if rope:
        in_specs.append(pl.BlockSpec((9, tm, LANES), lambda i: (0, i % tiles_per_seq, 0)))
        args.append(tabs)
    widths = [(1024, F32)] + [(256, BF16)] * 6 + [(512, BF16), (512, BF16), (256, BF16)]
    if ctx:
        widths += [(256, F32)] * 4 + [(128, F32)] * 2
    out_specs = [pl.BlockSpec((tm, w), row) for w, _ in widths]
    out_shape = [jax.ShapeDtypeStruct((t, w), dt) for w, dt in widths]
    return pl.pallas_call(
        functools.partial(_pre_kernel, rope=rope, ctx=ctx),
        grid=(t // tm,),
        in_specs=in_specs,
        out_specs=out_specs,
        out_shape=out_shape,
        compiler_params=_cparams(("arbitrary",)),
        name="pre_ctx" if ctx else "pre_lat",
    )(*args)


def _log_sigmoid(z):
    return jnp.minimum(z, 0.0) - jnp.log(1.0 + jnp.exp(-jnp.abs(z)))


def _ret_kernel(r_ref, s0_ref, dl_ref, ds_ref, dh_ref, o_ref, st_ref,
                s_sc, ob_sc, qdec_sc, kdec_sc, cdec_sc, intra_sc, *, n_chunks):
    p = pl.program_id(1)
    c = pl.program_id(2)
    fwd = p == 1
    chunk = RET_CHUNK
    w = BRANCH_W

    @pl.when(c == 0)
    def _init():
        lgl = _log_sigmoid(dl_ref[0])[0:1, :]
        i = lax.broadcasted_iota(jnp.int32, (chunk, w), 0).astype(F32)
        qe = jnp.where(fwd, i + 1.0, chunk - i)
        ke = jnp.where(fwd, chunk - 1.0 - i, i)
        qdec_sc[...] = jnp.exp(lgl * qe)
        kdec_sc[...] = jnp.exp(lgl * ke)
        rr = lax.broadcasted_iota(jnp.int32, (w, w), 0) // HEAD_W
        cc = lax.broadcasted_iota(jnp.int32, (w, w), 1) // HEAD_W
        cdec_sc[...] = jnp.where(rr == cc, jnp.exp(_log_sigmoid(ds_ref[0]) * float(chunk)), 0.0)
        ii = lax.broadcasted_iota(jnp.int32, (chunk, chunk), 0)
        jj = lax.broadcasted_iota(jnp.int32, (chunk, chunk), 1)
        dist = jnp.where(fwd, ii - jj, jj - ii)
        for hd in range(RET_H):
            lgh = _log_sigmoid(dh_ref[0, hd])[0:1, :]
            intra_sc[hd] = jnp.where(dist >= 0, jnp.exp(lgh * jnp.maximum(dist, 0).astype(F32)), 0.0)

    seqs = r_ref.shape[0]
    head_mask = [_lane_group_mask((chunk, w), HEAD_W, hd) for hd in range(RET_H)]
    rr = lax.broadcasted_iota(jnp.int32, (w, w), 0) // HEAD_W
    cc = lax.broadcasted_iota(jnp.int32, (w, w), 1) // HEAD_W
    block_diag = rr == cc
    outs = [None] * seqs

    def scan_step(sq):
        blk = r_ref[sq]
        q = blk[:, 0:w]
        k = blk[:, w:2 * w] * (RET_DK ** -0.5)
        v = blk[:, 2 * w:3 * w]
        qb = q.astype(BF16)
        kb = k.astype(BF16)
        vb = v.astype(BF16)
        s = jnp.where(c == 0, s0_ref[sq, 0], s_sc[sq])
        yield
        o = _dot(qb, s.astype(BF16)) * qdec_sc[...]
        for hd in range(RET_H):
            sc = _dot_nt(jnp.where(head_mask[hd], qb, jnp.zeros_like(qb)), kb) * intra_sc[hd]
            yield
            o = o + jnp.where(head_mask[hd], _dot(sc.astype(BF16), vb), 0.0)
        outs[sq] = o
        yield
        kd_t = jnp.transpose(k * kdec_sc[...]).astype(BF16)
        s_new = s * cdec_sc[...] + jnp.where(block_diag, _dot(kd_t, vb), 0.0)
        s_sc[sq] = s_new
        yield
        for hd in range(RET_H):
            st_ref[sq, 0, hd] = s_new[HEAD_W * hd:HEAD_W * (hd + 1), HEAD_W * hd:HEAD_W * (hd + 1)]

    def finish(sq):
        tot = outs[sq] + ob_sc[sq, c]
        mu = jnp.zeros_like(tot)
        for hd in range(RET_H):
            m1 = jnp.sum(jnp.where(head_mask[hd], tot, 0.0), axis=-1, keepdims=True) * (1.0 / HEAD_W)
            mu = jnp.where(head_mask[hd], m1, mu)
        yield
        xc = tot - mu
        var = jnp.zeros_like(tot)
        for hd in range(RET_H):
            v1 = jnp.sum(jnp.where(head_mask[hd], xc * xc, 0.0), axis=-1, keepdims=True) * (1.0 / HEAD_W)
            var = jnp.where(head_mask[hd], v1, var)
        yield
        g = r_ref[sq][:, 3 * w:4 * w]
        o_ref[sq] = (xc * lax.rsqrt(var + NORM_EPS) * (g * _sigmoid(g))).astype(o_ref.dtype)

    _round_robin([scan_step(sq) for sq in range(seqs)])

    @pl.when(p == 0)
    def _bwd():
        for sq in range(seqs):
            ob_sc[sq, n_chunks - 1 - c] = outs[sq]

    @pl.when(p == 1)
    def _fwd():
        _round_robin([finish(sq) for sq in range(seqs)])


def _retention(ret4, s0_bd, decay, *, batch, seq_len):
    n = seq_len // RET_CHUNK
    w = BRANCH_W
    seqs = RET_SEQS_PER_STEP if batch % RET_SEQS_PER_STEP == 0 else batch
    dlane = jnp.broadcast_to(jnp.repeat(decay, HEAD_W, axis=1)[:, None, :], (2, 8, w))
    dsub = jnp.broadcast_to(jnp.repeat(decay, HEAD_W, axis=1)[:, :, None], (2, w, w))
    dhead = jnp.broadcast_to(decay[:, :, None, None], (2, RET_H, 8, LANES))
    chunk_of = lambda p, c: c * p + (n - 1 - c) * (1 - p)
    o, st = pl.pallas_call(
        functools.partial(_ret_kernel, n_chunks=n),
        grid=(batch // seqs, 2, n),
        in_specs=[
            pl.BlockSpec((seqs, RET_CHUNK, 4 * w), lambda b, p, c: (b, chunk_of(p, c), 0)),
            pl.BlockSpec((seqs, 1, w, w), lambda b, p, c: (b, 1 - p, 0, 0)),
            pl.BlockSpec((1, 8, w), lambda b, p, c: (1 - p, 0, 0)),
            pl.BlockSpec((1, w, w), lambda b, p, c: (1 - p, 0, 0)),
            pl.BlockSpec((1, RET_H, 8, LANES), lambda b, p, c: (1 - p, 0, 0, 0)),
        ],
        out_specs=[
            pl.BlockSpec((seqs, RET_CHUNK, w), lambda b, p, c: (b, c * p, 0)),
            pl.BlockSpec((seqs, 1, RET_H, RET_DK, RET_DK), lambda b, p, c: (b, 1 - p, 0, 0, 0)),
        ],
        out_shape=[
            jax.ShapeDtypeStruct((batch, seq_len, w), BF16),
            jax.ShapeDtypeStruct((batch, 2, RET_H, RET_DK, RET_DK), F32),
        ],
        scratch_shapes=[
            pltpu.VMEM((seqs, w, w), F32),
            pltpu.VMEM((seqs, n, RET_CHUNK, w), F32),
            pltpu.VMEM((RET_CHUNK, w), F32),
            pltpu.VMEM((RET_CHUNK, w), F32),
            pltpu.VMEM((w, w), F32),
            pltpu.VMEM((RET_H, RET_CHUNK, RET_CHUNK), F32),
        ],
        compiler_params=_cparams(("arbitrary", "arbitrary", "arbitrary")),
        name="retention",
    )(ret4.reshape(batch, seq_len, 4 * w), s0_bd, dlane, dsub, dhead)
    return o.reshape(batch * seq_len, w), st


def _col_fold(acc, x, op):
    for j in range(x.shape[1] // LANES):
        acc = op(acc, x[:, LANES * j:LANES * (j + 1)])
    return acc


_ATTN_GROUPING = {"sink": (4, 1), "diff": (4, 2), "mla": (4, 1)}


def _attn_body(*refs, kind, n_chunks, kc, has_ctx, lam_init, static_chunks):
    it = iter(refs)
    q_ref, k_ref, v_ref = next(it), next(it), next(it)
    if has_ctx:
        if kind == "mla":
            cckv_ref, ckr_ref, wukv_ref = next(it), next(it), next(it)
        else:
            kx_ref, vx_ref = next(it), next(it)
    if kind == "sink":
        sink_ref = next(it)
    if kind == "diff":
        lam_ref, gd_ref = next(it), next(it)
    o_ref = next(it)
    nv, n_groups = _ATTN_GROUPING[kind]
    per_group = lambda: [next(it) for _ in range(n_groups)]
    s_all, m_all, l_all = per_group(), per_group(), per_group()
    qst_all = per_group() if kind != "mla" else None
    sx_all = per_group() if has_ctx else None
    if has_ctx and kind == "mla":
        kx_sc, vx_sc = next(it), next(it)

    tq = q_ref.shape[1]
    w = BRANCH_W
    rows = nv * tq
    unroll = ATTN_UNROLL if n_chunks % ATTN_UNROLL == 0 else 1
    q = q_ref[0]
    log2_scale = WIN_HD ** -0.5 * LOG2E if kind == "sink" else None

    if has_ctx and kind == "mla":
        kvc = _dot(cckv_ref[0].astype(BF16), wukv_ref[...])
        kr = ckr_ref[0]
        for hd in range(MLA_H):
            kx_sc[:, LANES * hd:LANES * (hd + 1)] = (kvc[:, LANES * hd:LANES * (hd + 1)] + kr).astype(BF16)
        vx_sc[...] = kvc[:, MLA_H * LANES:].astype(BF16)

    def rows_of(ref, ch):
        if static_chunks:
            return ref[0, ch * kc:(ch + 1) * kc, :]
        return ref[0, pl.ds(pl.multiple_of(ch * kc, kc), kc), :]

    def k_chunk(ch):
        return rows_of(k_ref, ch)

    def v_chunk(ch):
        return rows_of(v_ref, ch)

    def kx_tile():
        return kx_sc[...] if kind == "mla" else kx_ref[0]

    def vx_tile():
        return vx_sc[...] if kind == "mla" else vx_ref[0]

    def chunk_loop(body, carry):
        if static_chunks:
            for ch in range(n_chunks):
                carry = body(ch, carry)
                yield
        else:
            carry = lax.fori_loop(0, n_chunks, body, carry, unroll=unroll)
            yield
        return carry

    def weighted_values(make_weights, s_sc, sx_sc, n_rows):
        if not static_chunks:
            def body(ch, carry):
                return carry + _dot(make_weights(s_sc[ch]), v_chunk(ch))

            acc = lax.fori_loop(0, n_chunks, body, jnp.zeros((n_rows, w), F32), unroll=unroll)
            yield
            if has_ctx:
                acc = acc + _dot(make_weights(sx_sc[...]), vx_tile())
            return acc
        score_of = [functools.partial(lambda ch: s_sc[ch], ch) for ch in range(n_chunks)]
        value_of = [functools.partial(v_chunk, ch) for ch in range(n_chunks)]
        if has_ctx:
            score_of.append(lambda: sx_sc[...])
            value_of.append(vx_tile)
        acc = None
        ahead = make_weights(score_of[0]())
        for i in range(len(score_of)):
            cur = ahead
            if i + 1 < len(score_of):
                ahead = make_weights(score_of[i + 1]())
            term = _dot(cur, value_of[i]())
            acc = term if acc is None else acc + term
            yield
        return acc

    parts = []

    def group_body(grp):
        s_sc, m_sc, l_sc = s_all[grp], m_all[grp], l_all[grp]
        qst_sc = qst_all[grp] if kind != "mla" else None
        sx_sc = sx_all[grp] if has_ctx else None

        def scores(kt):
            if kind == "mla":
                heads = range(nv * grp, nv * (grp + 1))
                return jnp.concatenate(
                    [_dot_nt(q[:, LANES * hd:LANES * (hd + 1)], kt[:, LANES * hd:LANES * (hd + 1)])
                     for hd in heads], axis=0)
            return _dot_nt(qst_sc[...], kt)

        def probs(s):
            mb = m_sc[...]
            logit = (lambda x: x) if log2_scale is None else (lambda x: x * log2_scale)
            cols = [jnp.exp2(logit(s[:, LANES * j:LANES * (j + 1)]) - mb) for j in range(s.shape[1] // LANES)]
            tot = l_sc[...]
            for col in cols:
                tot = tot + col
            l_sc[...] = tot
            return jnp.concatenate(cols, axis=1)

        if kind != "mla":
            for v in range(nv):
                mask = (_lane_group_mask((tq, w), HEAD_W, v) if kind == "sink"
                        else _lane_group_mask((tq, w), DIFF_D, nv * grp + v))
                qst_sc[v * tq:(v + 1) * tq, :] = jnp.where(mask, q, jnp.zeros_like(q))
        yield

        m_sc[...] = jnp.full((rows, LANES), -jnp.inf, F32)

        def score_body(ch, carry):
            s = scores(k_chunk(ch))
            s_sc[ch] = s
            m_sc[...] = _col_fold(m_sc[...], s, jnp.maximum)
            return carry

        yield from chunk_loop(score_body, 0)
        if has_ctx:
            s = scores(kx_tile())
            sx_sc[...] = s
            m_sc[...] = _col_fold(m_sc[...], s, jnp.maximum)
        yield
        m2 = jnp.max(m_sc[...], axis=-1, keepdims=True)
        if log2_scale is not None:
            m2 = m2 * log2_scale
        if kind == "sink":
            sink2 = LOG2E * jnp.concatenate(
                [jnp.broadcast_to(sink_ref[v:v + 1, 0:1], (tq, 1)) for v in range(nv)], axis=0)
            m2 = jnp.maximum(m2, sink2)
        heads = range(nv * grp, nv * (grp + 1))
        m_sc[...] = jnp.broadcast_to(m2, (rows, LANES))
        l_sc[...] = jnp.zeros((rows, LANES), F32)
        yield

        if kind != "diff":
            acc = yield from weighted_values(lambda s: probs(s).astype(BF16), s_sc, sx_sc, rows)
            yield
            l = jnp.sum(l_sc[...], axis=-1, keepdims=True)
            if kind == "sink":
                l = l + jnp.exp2(sink2 - m2)
            pv = acc * (1.0 / l)
            for v, hd in enumerate(heads):
                parts.append(jnp.where(_lane_group_mask((tq, w), HEAD_W, hd), pv[v * tq:(v + 1) * tq], 0.0))
        else:
            def exp_body(ch, carry):
                s_sc[ch] = probs(s_sc[ch])
                return carry

            yield from chunk_loop(exp_body, 0)
            if has_ctx:
                sx_sc[...] = probs(sx_sc[...])
            yield
            dl = lam_ref[...]
            lam = (jnp.exp(jnp.sum(dl[0:1] * dl[1:2], axis=-1, keepdims=True))
                   - jnp.exp(jnp.sum(dl[2:3] * dl[3:4], axis=-1, keepdims=True)) + lam_init)
            l = jnp.sum(l_sc[...], axis=-1, keepdims=True)
            inv_l = 1.0 / l
            for v in range(2):
                first, second = slice(2 * v * tq, (2 * v + 1) * tq), slice((2 * v + 1) * tq, (2 * v + 2) * tq)
                l_sc[second, :] = jnp.broadcast_to(lam * l[first] * inv_l[second], (tq, LANES))
            yield

            def weights(pr):
                cols = []
                for j in range(pr.shape[1] // LANES):
                    x = pr[:, LANES * j:LANES * (j + 1)]
                    cols.append(jnp.concatenate(
                        [x[0:tq] - x[tq:2 * tq] * l_sc[tq:2 * tq, :],
                         x[2 * tq:3 * tq] - x[3 * tq:4 * tq] * l_sc[3 * tq:4 * tq, :]], axis=0))
                return jnp.concatenate(cols, axis=1).astype(BF16)

            acc = yield from weighted_values(weights, s_sc, sx_sc, 2 * tq)
            yield
            for v in range(2):
                head_out = acc[v * tq:(v + 1) * tq] * inv_l[2 * v * tq:(2 * v + 1) * tq]
                parts.append(jnp.where(_lane_group_mask((tq, w), HEAD_W, 2 * grp + v), head_out, 0.0))

    lag = (n_chunks + 2) if static_chunks else 0
    yield from _interleave([group_body(grp) for grp in range(n_groups)], [lag * grp for grp in range(n_groups)])
    out = parts[0]
    for part in parts[1:]:
        out = out + part
    if kind == "diff":
        ms = jnp.zeros_like(out)
        for hd in range(DIFF_H):
            mh = _lane_group_mask((tq, w), HEAD_W, hd)
            m1 = jnp.sum(jnp.where(mh, out * out, 0.0), axis=-1, keepdims=True) * (1.0 / HEAD_W)
            ms = jnp.where(mh, m1, ms)
        out = out * lax.rsqrt(ms + NORM_EPS) * gd_ref[...] * (1.0 - lam_init)
    o_ref[0] = out.astype(o_ref.dtype)


def _attn_multi_kernel(*refs, seqs, batched, **kw):
    n_io = len(batched)
    bodies = []
    for sq in range(seqs):
        view = [r.at[pl.ds(sq, 1)] if flag else r for r, flag in zip(refs[:n_io], batched)]
        view += [r.at[sq] for r in refs[n_io:]]
        bodies.append(_attn_body(*view, **kw))
    _round_robin(bodies)


def _attention(kind, q, k, v, *, ctx_args=(), params=(), lam_init=0.0):
    b, n, wq = q.shape
    nk = k.shape[1]
    tq = min(TQ_ATTN, n)
    kc = min(KC_ATTN, nk)
    n_chunks = nk // kc
    has_ctx = len(ctx_args) > 0
    w = BRANCH_W
    nv, n_groups = _ATTN_GROUPING[kind]
    rows = nv * tq
    static_chunks = n_chunks > 1
    seqs = ATTN_SEQS_PER_STEP if (n == tq and b % ATTN_SEQS_PER_STEP == 0) else 1
    per_b = lambda bb, i: (bb, 0, 0)
    shared = lambda bb, i: (0, 0)
    in_specs = [
        pl.BlockSpec((seqs, tq, wq), lambda bb, i: (bb, i, 0)),
        pl.BlockSpec((seqs, nk, k.shape[2]), per_b),
        pl.BlockSpec((seqs, nk, w), per_b),
    ]
    args = [q, k, v]
    batched = [True, True, True]
    for a in ctx_args:
        if a.ndim == 3:
            in_specs.append(pl.BlockSpec((seqs,) + a.shape[1:], per_b))
        else:
            in_specs.append(pl.BlockSpec(a.shape, shared))
        batched.append(a.ndim == 3)
        args.append(a)
    for a in params:
        in_specs.append(pl.BlockSpec(a.shape, shared))
        batched.append(False)
        args.append(a)
    batched.append(True)
    grouped = [(n_chunks, rows, kc, F32), (rows, LANES, F32), (rows, LANES, F32)]
    if kind != "mla":
        grouped.append((rows, w, BF16))
    single = []
    if has_ctx:
        m_ctx = ctx_args[0].shape[1]
        grouped.append((rows, m_ctx, F32))
        if kind == "mla":
            single = [(m_ctx, MLA_H * LANES, BF16), (m_ctx, w, BF16)]
    scratch = [sh for sh in grouped for _ in range(n_groups)] + single
    scratch_bytes = sum(seqs * math.prod(sh[:-1]) * jnp.dtype(sh[-1]).itemsize for sh in scratch)
    return pl.pallas_call(
        functools.partial(_attn_multi_kernel, seqs=seqs, batched=tuple(batched), kind=kind, n_chunks=n_chunks,
                          kc=kc, has_ctx=has_ctx, lam_init=lam_init, static_chunks=static_chunks),
        grid=(b // seqs, n // tq),
        in_specs=in_specs,
        out_specs=pl.BlockSpec((seqs, tq, w), lambda bb, i: (bb, i, 0)),
        out_shape=jax.ShapeDtypeStruct((b, n, w), BF16),
        scratch_shapes=[pltpu.VMEM((seqs,) + sh[:-1], sh[-1]) for sh in scratch],
        compiler_params=_cparams(("arbitrary", "arbitrary"), vmem=max(VMEM_LIMIT, scratch_bytes + ATTN_VMEM_EXTRA)),
        name="attn_" + kind + ("_lat" if has_ctx else "_ctx"),
    )(*args)


def _win_kernel(q_ref, kp_ref, kc_ref, kn_ref, vp_ref, vc_ref, vn_ref, kx_ref, vx_ref, sink_ref, o_ref):
    n_steps = pl.num_programs(1)
    step = pl.program_id(1)
    tq = WINDOW
    nblk = q_ref.shape[1] // tq
    w = BRANCH_W
    nh = 4
    rows = nh * tq
    scale = WIN_HD ** -0.5
    head_mask = [_lane_group_mask((tq, w), HEAD_W, hd) for hd in range(nh)]
    ii = lax.broadcasted_iota(jnp.int32, (rows, tq), 0) % tq
    jj = lax.broadcasted_iota(jnp.int32, (rows, tq), 1)
    sink_col = jnp.concatenate([jnp.broadcast_to(sink_ref[hd:hd + 1, 0:1], (tq, 1)) for hd in range(nh)], axis=0)
    kx = kx_ref[0]
    vx = vx_ref[0]

    def rows_of(ref, blk):
        return ref[0, blk * tq:(blk + 1) * tq, :]

    def block(blk):
        q = rows_of(q_ref, blk)
        qst = jnp.concatenate([jnp.where(head_mask[hd], q, jnp.zeros_like(q)) for hd in range(nh)], axis=0)
        kp, vp = (kp_ref[0], vp_ref[0]) if blk == 0 else (rows_of(kc_ref, blk - 1), rows_of(vc_ref, blk - 1))
        kn, vn = (kn_ref[0], vn_ref[0]) if blk == nblk - 1 else (rows_of(kc_ref, blk + 1), rows_of(vc_ref, blk + 1))
        mask_prev = jj >= ii
        mask_next = jj <= ii
        if blk == 0:
            mask_prev = mask_prev & (step > 0)
        if blk == nblk - 1:
            mask_next = mask_next & (step < n_steps - 1)
        yield
        sp = jnp.where(mask_prev, _dot_nt(qst, kp) * scale, NEG_INF)
        sc = _dot_nt(qst, rows_of(kc_ref, blk)) * scale
        sn = jnp.where(mask_next, _dot_nt(qst, kn) * scale, NEG_INF)
        sx = _dot_nt(qst, kx) * scale
        yield
        m = jnp.maximum(jnp.maximum(jnp.max(sp, axis=-1, keepdims=True), jnp.max(sc, axis=-1, keepdims=True)),
                        jnp.maximum(jnp.max(sn, axis=-1, keepdims=True), jnp.max(sx, axis=-1, keepdims=True)))
        m = jnp.maximum(m, sink_col)
        yield
        pp, pc, pn, px = (jnp.exp(s - m) for s in (sp, sc, sn, sx))
        l = (jnp.sum(pp, axis=-1, keepdims=True) + jnp.sum(pc, axis=-1, keepdims=True)
             + jnp.sum(pn, axis=-1, keepdims=True) + jnp.sum(px, axis=-1, keepdims=True) + jnp.exp(sink_col - m))
        yield
        pv = (_dot(pp.astype(BF16), vp) + _dot(pc.astype(BF16), rows_of(vc_ref, blk))
              + _dot(pn.astype(BF16), vn) + _dot(px.astype(BF16), vx)) * (1.0 / l)
        yield
        out = jnp.zeros((tq, w), F32)
        for hd in range(nh):
            out = out + jnp.where(head_mask[hd], pv[hd * tq:(hd + 1) * tq], 0.0)
        o_ref[0, blk * tq:(blk + 1) * tq, :] = out.astype(o_ref.dtype)

    _round_robin([block(blk) for blk in range(nblk)])


def _window_attention(q, k, v, kx, vx, sink_tile):
    b, n, w = q.shape
    tq = WINDOW
    span = WIN_BLOCKS_PER_STEP * tq
    nb = n // tq
    m_ctx = kx.shape[1]
    edge = lambda f: pl.BlockSpec((1, tq, w), f)
    prev = lambda bb, i: (bb, jnp.maximum(i * WIN_BLOCKS_PER_STEP - 1, 0), 0)
    nxt = lambda bb, i: (bb, jnp.minimum((i + 1) * WIN_BLOCKS_PER_STEP, nb - 1), 0)
    mid = pl.BlockSpec((1, span, w), lambda bb, i: (bb, i, 0))
    per_b = lambda bb, i: (bb, 0, 0)
    return pl.pallas_call(
        _win_kernel,
        grid=(b, n // span),
        in_specs=[mid, edge(prev), mid, edge(nxt), edge(prev), mid, edge(nxt),
                  pl.BlockSpec((1, m_ctx, w), per_b), pl.BlockSpec((1, m_ctx, w), per_b),
                  pl.BlockSpec(sink_tile.shape, lambda bb, i: (0, 0))],
        out_specs=mid,
        out_shape=jax.ShapeDtypeStruct((b, n, w), BF16),
        compiler_params=_cparams(("arbitrary", "arbitrary")),
        name="attn_window",
    )(q, k, k, k, v, v, v, kx, vx, sink_tile)


def _post_kernel(x_ref, mod_ref, g_ref, o0_ref, o1_ref, o2_ref, o3_ref,
                 wg_ref, bg_ref, wb_ref, wo_ref, out_ref, mix_sc):
    d = D_MODEL
    x = x_ref[...]
    mod = mod_ref[0]
    h = _rmsnorm(x, g_ref[...]) * (1.0 + mod[1:2]) + mod[0:1]
    hb = h.astype(BF16)
    cw = 256
    for j in range(d // cw):
        mixed = None
        for nbr, o_ref in enumerate((o0_ref, o1_ref, o2_ref, o3_ref)):
            lo = nbr * d + j * cw
            gate = _sigmoid(_dot(hb, wg_ref[:, lo:lo + cw]) + bg_ref[:, lo:lo + cw])
            term = gate * _dot(o_ref[...], wb_ref[nbr, :, j * cw:(j + 1) * cw])
            mixed = term if mixed is None else mixed + term
        mix_sc[:, j * cw:(j + 1) * cw] = mixed.astype(BF16)
    out_ref[...] = x + mod[2:3] * _dot(mix_sc[...], wo_ref[...])


def _post(x2d, mod, g_mix, outs, w_gate, b_gate, w_branch, w_out, *, seq_len, layer):
    t, d = x2d.shape
    tm = TM_DENSE
    tiles_per_seq = seq_len // tm
    const2 = lambda i: (0, 0)
    row = lambda i: (i, 0)
    return pl.pallas_call(
        _post_kernel,
        grid=(t // tm,),
        in_specs=[
            pl.BlockSpec((tm, d), row),
            _mod_spec(mod, tiles_per_seq),
            pl.BlockSpec((1, d), const2),
        ] + [pl.BlockSpec((tm, BRANCH_W), row)] * 4 + [
            _resident(w_gate.shape, layer),
            pl.BlockSpec((1, 4 * d), const2),
            _resident(w_branch.shape, layer),
            _resident(w_out.shape, layer),
        ],
        out_specs=pl.BlockSpec((tm, d), row),
        out_shape=jax.ShapeDtypeStruct((t, d), F32),
        scratch_shapes=[pltpu.VMEM((tm, d), BF16)],
        compiler_params=_cparams(("arbitrary",)),
        name="post",
    )(x2d, mod, g_mix.reshape(1, d), *outs, w_gate, b_gate.reshape(1, 4 * d), w_branch, w_out)


def _ffn_kernel(*refs, tiles_per_seq, final):
    halo = tiles_per_seq > 1
    it = iter(refs)
    x_ref = next(it)
    if halo:
        xp_ref, xn_ref = next(it), next(it)
    mod_ref, g_ref, wup_ref, bup_ref, wcv_ref, bcv_ref, wdn_ref = (next(it) for _ in range(7))
    gf_ref = next(it) if final else None
    out_ref = next(it)
    h_sc = next(it)

    tm = x_ref.shape[0]
    hr = CONV_HALO
    n_ff = D_FF // FF_CHUNK
    mod = mod_ref[0]
    g = g_ref[...]

    def norm_mod(rows):
        return (_rmsnorm(rows, g) * (1.0 + mod[4:5]) + mod[3:4]).astype(BF16)

    x = x_ref[...]
    h_sc[hr:hr + tm, :] = norm_mod(x)
    if halo:
        h_sc[0:hr, :] = norm_mod(xp_ref[...])
        h_sc[hr + tm:, :] = norm_mod(xn_ref[...])
        i = pl.program_id(0)
        keep_lo = jnp.where((i % tiles_per_seq) == 0, 0.0, 1.0)
        keep_hi = jnp.where((i % tiles_per_seq) == tiles_per_seq - 1, 0.0, 1.0)
        rid = lax.broadcasted_iota(jnp.int32, (8, 1), 0)
        edge_lo = jnp.where(rid == 7, keep_lo, 1.0)
        edge_hi = jnp.where(rid == 0, keep_hi, 1.0)

    def up(ch):
        lo = ch * FF_CHUNK
        if halo:
            u = _dot(h_sc[...], wup_ref[:, lo:lo + FF_CHUNK]) + bup_ref[:, lo:lo + FF_CHUNK]
            return jnp.concatenate([u[:hr - 8], u[hr - 8:hr] * edge_lo, u[hr:hr + tm],
                                    u[hr + tm:hr + tm + 8] * edge_hi, u[hr + tm + 8:]], axis=0)
        pad = jnp.zeros((hr, FF_CHUNK), F32)
        u = _dot(h_sc[hr:hr + tm, :], wup_ref[:, lo:lo + FF_CHUNK]) + bup_ref[:, lo:lo + FF_CHUNK]
        return jnp.concatenate([pad, u, pad], axis=0)

    def conv(ch, u):
        lo = ch * FF_CHUNK
        wc = wcv_ref[:, lo:lo + FF_CHUNK]
        before = pltpu.roll(u, 1, axis=0)[hr:hr + tm]
        after = pltpu.roll(u, u.shape[0] - 1, axis=0)[hr:hr + tm]
        return (bcv_ref[:, lo:lo + FF_CHUNK] + wc[0:1] * before + wc[1:2] * u[hr:hr + tm] + wc[2:3] * after)

    u_val, u_gate = up(0), up(n_ff)
    act = None
    for c in range(n_ff + 1):
        if c + 1 < n_ff:
            u_val_next, u_gate_next = up(c + 1), up(c + 1 + n_ff)
        if c >= 1:
            term = _dot(act, wdn_ref[(c - 1) * FF_CHUNK:c * FF_CHUNK, :])
            acc = term if c == 1 else acc + term
        if c < n_ff:
            a = conv(c, u_val)
            gg = conv(c + n_ff, u_gate)
            act = (gg * _sigmoid(gg) * a).astype(BF16)
            u_val, u_gate = u_val_next, u_gate_next
    y = x + mod[5:6] * acc
    if final:
        y = _rmsnorm(y, gf_ref[...])
    out_ref[...] = y


def _ffn(x2d, mod, g_ffn, w_up, b_up, w_conv, b_conv, w_down, g_final, *, seq_len, layer):
    t, d = x2d.shape
    tm = min(TM_FFN, seq_len)
    hr = CONV_HALO
    tiles_per_seq = seq_len // tm
    halo = tiles_per_seq > 1
    final = g_final is not None
    hb = tm // hr
    n_hblocks = t // hr
    const2 = lambda i: (0, 0)
    row = lambda i: (i, 0)
    in_specs = [pl.BlockSpec((tm, d), row)]
    args = [x2d]
    if halo:
        in_specs += [pl.BlockSpec((hr, d), lambda i: (jnp.maximum(i * hb - 1, 0), 0)),
                     pl.BlockSpec((hr, d), lambda i: (jnp.minimum((i + 1) * hb, n_hblocks - 1), 0))]
        args += [x2d, x2d]
    in_specs += [
        _mod_spec(mod, tiles_per_seq),
        pl.BlockSpec((1, d), const2),
        _resident(w_up.shape, layer),
        pl.BlockSpec(b_up.shape, const2),
        pl.BlockSpec(w_conv.shape, const2),
        pl.BlockSpec(b_conv.shape, const2),
        _resident(w_down.shape, layer),
    ]
    args += [mod, g_ffn.reshape(1, d), w_up, b_up, w_conv, b_conv, w_down]
    if final:
        in_specs.append(pl.BlockSpec((1, d), const2))
        args.append(g_final.reshape(1, d))
    return pl.pallas_call(
        functools.partial(_ffn_kernel, tiles_per_seq=tiles_per_seq, final=final),
        grid=(t // tm,),
        in_specs=in_specs,
        out_specs=pl.BlockSpec((tm, d), row),
        out_shape=jax.ShapeDtypeStruct((t, d), F32),
        scratch_shapes=[
            pltpu.VMEM((tm + 2 * hr, d), BF16),
        ],
        compiler_params=_cparams(("arbitrary",)),
        name=("ffn_final" if final else "ffn") + ("_lat" if halo else "_ctx"),
    )(*args)


def _rope_tables(n_tok):
    rows = n_tok // GRID_W

    def axis(n_pos, half):
        freqs = ROPE_BASE ** (-jnp.arange(half, dtype=F32) / half)
        ang = jnp.arange(n_pos).astype(F32)[:, None] * freqs[None, :]
        return jnp.cos(ang), jnp.sin(ang)

    def unit(width):
        half = width // 4
        cr, sr = (jnp.repeat(t, GRID_W, axis=0) for t in axis(rows, half))
        cc, sc = (jnp.tile(t, (rows, 1)) for t in axis(GRID_W, half))
        z = jnp.zeros_like(sr)
        return (jnp.concatenate([cr, cr, cc, cc], axis=1),
                jnp.concatenate([-sr, z, -sc, z], axis=1),
                jnp.concatenate([z, sr, z, sc], axis=1))

    a = [jnp.tile(t, (1, 2)) for t in unit(64)]
    b = [jnp.tile(t, (1, 4)) for t in unit(32)]
    cu = unit(32)
    ones = jnp.ones((n_tok, 64), F32)
    z64 = jnp.zeros((n_tok, 64), F32)
    z32 = jnp.zeros((n_tok, 32), F32)
    c = [jnp.concatenate([ones, cu[0], ones[:, :32]], axis=1),
         jnp.concatenate([z64, cu[1], z32], axis=1),
         jnp.concatenate([z64, cu[2], z32], axis=1)]
    return jnp.stack(a + b + c, axis=0)


def _dup_groups(w):
    g0, g1 = w[..., :HEAD_W], w[..., HEAD_W:]
    return jnp.concatenate([g0, g0, g1, g1], axis=-1)


def _prep_weights(p):
    depth = p["w_in"].shape[0]
    cuts = np.cumsum((0,) + IN_SPLITS)
    parts = [p["w_in"][:, :, cuts[i]:cuts[i + 1]] for i in range(len(IN_SPLITS))]
    rq, rk, rv, rg, wq, wk, wv, dq, dk, dv, mcq, mckv, mkr = parts
    mkr128 = jnp.pad(mkr, ((0, 0), (0, 0), (MLA_NOPE, LANES - MLA_NOPE - MLA_ROPE)))
    w_in_p = jnp.concatenate([rq, rk, rv, rg, wq, _dup_groups(wk), _dup_groups(wv), dq, dk, dv,
                              mcq, mckv, mkr128], axis=2).astype(BF16)
    w_uq = p["w_mla_uq"].reshape(depth, MLA_QRANK, MLA_H, MLA_NOPE + MLA_ROPE)
    w_uq_p = jnp.pad(w_uq, ((0, 0), (0, 0), (0, 0), (0, LANES - MLA_NOPE - MLA_ROPE)))
    w_ukv = p["w_mla_ukv"].reshape(depth, MLA_KVRANK, MLA_H, MLA_NOPE + MLA_V)
    w_uk = jnp.pad(w_ukv[..., :MLA_NOPE], ((0, 0), (0, 0), (0, 0), (0, LANES - MLA_NOPE)))
    w_uv = w_ukv[..., MLA_NOPE:].reshape(depth, MLA_KVRANK, MLA_H * MLA_V)
    sink = jnp.broadcast_to(p["win_sink"].astype(F32)[:, :, None], (depth, 4, LANES))
    return dict(
        w_in_p=w_in_p,
        w_uq_p=w_uq_p.reshape(depth, MLA_QRANK, MLA_H * LANES).astype(BF16),
        w_ukv_p=jnp.concatenate([w_uk.reshape(depth, MLA_KVRANK, MLA_H * LANES), w_uv], axis=2).astype(BF16),
        w_gate=p["w_gate"].astype(BF16),
        w_branch=p["w_branch"].astype(BF16),
        w_out=p["w_out"].astype(BF16),
        w_up=p["w_up"].astype(BF16),
        w_down=p["w_down"].astype(BF16),
        sink_tile8=jnp.pad(sink, ((0, 0), (0, 4), (0, 0))),
        g_diff4=jnp.tile(p["g_diff"], (1, DIFF_H)).reshape(depth, 1, BRANCH_W),
    )


def _block_diag_state(s):
    b = s.shape[0]
    eye = jnp.eye(RET_H, dtype=s.dtype)
    bd = s[:, :, :, :, None, :] * eye[None, None, :, None, :, None]
    return bd.reshape(b, 2, RET_H * RET_DK, RET_H * RET_DK)


def _mod_rows(mods_l, start, count):
    m = mods_l[start:start + count].reshape(count, 6, D_MODEL)
    return jnp.pad(m, ((0, 0), (0, 2), (0, 0)))


def kernel(x_prompt, x_sample, state_ret, cache_win_k, cache_win_v, cache_diff_k, cache_diff_v, cache_mla_ckv, cache_mla_krope, c, c_ctx, w_mod, b_mod, g_mix, w_in, ret_decay, win_sink, diff_lambda, g_diff, g_mla_q, w_mla_uq, g_mla_kv, w_mla_ukv, w_branch, w_gate, b_gate, w_out, g_ffn, w_up, b_up, w_conv, b_conv, w_down, g_final):
    d = D_MODEL
    bc, lc, _ = x_prompt.shape
    bl, ll, _ = x_sample.shape
    m_ctx = cache_win_k.shape[2]

    cond_rows = jnp.zeros((8, d), F32).at[0].set(c_ctx).at[1:1 + bl].set(c)
    mods = _modulation(cond_rows, w_mod, b_mod)
    tabs = _rope_tables(ll)

    xp = x_prompt.reshape(bc * lc, d)
    xs = x_sample.reshape(bl * ll, d)
    produced = [[] for _ in range(7)]
    p = _prep_weights(dict(w_in=w_in, w_mla_uq=w_mla_uq, w_mla_ukv=w_mla_ukv, w_gate=w_gate, w_branch=w_branch,
                           w_out=w_out, w_up=w_up, w_down=w_down, win_sink=win_sink, g_diff=g_diff))
    zero_state = jnp.zeros((bc, 2, BRANCH_W, BRANCH_W), F32)
    for li in range(DEPTH):
        sink_tile8, g_diff4 = p["sink_tile8"][li], p["g_diff4"][li]
        ffn_small = (b_up[li].reshape(1, -1), w_conv[li], b_conv[li].reshape(1, -1))
        lam_init = 0.8 - 0.6 * math.exp(-0.3 * li)
        mod_c = _mod_rows(mods[li], 0, 1)
        mod_l = _mod_rows(mods[li], 1, bl)
        final_g = g_final if li == DEPTH - 1 else None

        (ret4, wq, wke, wve, dq, dk, dv, mq, mk, mv,
         wke32, wve32, dk32, dv32, ckv32, mkr32) = _pre(
            xp, mod_c, g_mix[li], p["w_in_p"], p["w_uq_p"], p["w_ukv_p"], g_mla_q[li], g_mla_kv[li], None,
            seq_len=lc, ctx=True, layer=li)
        o_ret, st = _retention(ret4, zero_state, ret_decay[li], batch=bc, seq_len=lc)
        r3 = lambda a: a.reshape(bc, lc, a.shape[-1])
        o_win = _attention("sink", r3(wq), r3(wke), r3(wve), params=(sink_tile8,))
        o_diff = _attention("diff", r3(dq), r3(dk), r3(dv), params=(diff_lambda[li], g_diff4), lam_init=lam_init)
        o_mla = _attention("mla", r3(mq), r3(mk), r3(mv))
        f2 = lambda a: a.reshape(bc * lc, BRANCH_W)
        xp = _post(xp, mod_c, g_mix[li], (o_ret, f2(o_win), f2(o_diff), f2(o_mla)),
                   p["w_gate"], b_gate[li], p["w_branch"], p["w_out"], seq_len=lc, layer=li)
        xp = _ffn(xp, mod_c, g_ffn[li], p["w_up"], *ffn_small, p["w_down"], final_g, seq_len=lc, layer=li)
        undup = lambda a: a.reshape(bc, lc, 2, 2, HEAD_W)[:, :, :, 0, :]
        produced[0].append(st)
        produced[1].append(undup(wke32))
        produced[2].append(undup(wve32))
        produced[3].append(dk32.reshape(bc, lc, DIFF_H, 2 * DIFF_D))
        produced[4].append(dv32.reshape(bc, lc, DIFF_H, 2 * DIFF_D))
        produced[5].append(ckv32.reshape(bc, lc, MLA_KVRANK))
        produced[6].append(mkr32.reshape(bc, lc, LANES)[:, :, MLA_NOPE:MLA_NOPE + MLA_ROPE])

        (ret4, wq, wke, wve, dq, dk, dv, mq, mk, mv) = _pre(
            xs, mod_l, g_mix[li], p["w_in_p"], p["w_uq_p"], p["w_ukv_p"], g_mla_q[li], g_mla_kv[li], tabs,
            seq_len=ll, ctx=False, layer=li)
        o_ret, _ = _retention(ret4, _block_diag_state(state_ret[:, li]), ret_decay[li], batch=bl, seq_len=ll)
        r3 = lambda a: a.reshape(bl, ll, a.shape[-1])
        kx_win = _dup_groups(cache_win_k[:, li].reshape(bl, m_ctx, 2 * HEAD_W)).astype(BF16)
        vx_win = _dup_groups(cache_win_v[:, li].reshape(bl, m_ctx, 2 * HEAD_W)).astype(BF16)
        o_win = _window_attention(r3(wq), r3(wke), r3(wve), kx_win, vx_win, sink_tile8)
        kx_diff = cache_diff_k[:, li].reshape(bl, m_ctx, BRANCH_W).astype(BF16)
        vx_diff = cache_diff_v[:, li].reshape(bl, m_ctx, BRANCH_W).astype(BF16)
        o_diff = _attention("diff", r3(dq), r3(dk), r3(dv), ctx_args=(kx_diff, vx_diff),
                            params=(diff_lambda[li], g_diff4), lam_init=lam_init)
        kr128 = jnp.pad(cache_mla_krope[:, li], ((0, 0), (0, 0), (MLA_NOPE, LANES - MLA_NOPE - MLA_ROPE)))
        o_mla = _attention("mla", r3(mq), r3(mk), r3(mv), ctx_args=(cache_mla_ckv[:, li], kr128, p["w_ukv_p"][li]))
        f2 = lambda a: a.reshape(bl * ll, BRANCH_W)
        xs = _post(xs, mod_l, g_mix[li], (o_ret, f2(o_win), f2(o_diff), f2(o_mla)),
                   p["w_gate"], b_gate[li], p["w_branch"], p["w_out"], seq_len=ll, layer=li)
        xs = _ffn(xs, mod_l, g_ffn[li], p["w_up"], *ffn_small, p["w_down"], final_g, seq_len=ll, layer=li)

    y_prompt = xp.reshape(bc, lc, d)
    y_sample = xs.reshape(bl, ll, d)
    stack = lambda lst: jnp.stack(lst, axis=1)
    return (y_prompt, y_sample, stack(produced[0]),
            stack(produced[1]), stack(produced[2]), stack(produced[3]), stack(produced[4]),
            stack(produced[5]), stack(produced[6]))
```

```python
import functools
import math

import numpy as np
import jax
import jax.numpy as jnp
from jax import lax
from jax.experimental import pallas as pl
from jax.experimental.pallas import tpu as pltpu

F32 = jnp.float32
BF16 = jnp.bfloat16

D_MODEL = 1024
DEPTH = 2
GRID_W = 64
ROPE_BASE = 10000.0
NORM_EPS = 1e-6
NEG_INF = -1e30
LOG2E = 1.4426950408889634
RET_H = 4
RET_DK = 64
RET_CHUNK = 128
WIN_HD = 64
WINDOW = 128
DIFF_H = 4
DIFF_D = 32
MLA_H = 4
MLA_NOPE = 64
MLA_ROPE = 32
MLA_V = 64
MLA_QRANK = 256
MLA_KVRANK = 128
D_FF = 2816
IN_SPLITS = (256, 256, 256, 256, 256, 128, 128, 256, 256, 256, 256, 128, 32)

LANES = 128
BF16_ROWS = 16
BRANCH_W = 256
HEAD_W = 64

_C_RET, _C_WQ, _C_WK, _C_WV, _C_DQ, _C_DK, _C_DV, _C_MCQ, _C_MCKV, _C_MKR, _C_END = (
    0, 1024, 1280, 1536, 1792, 2048, 2304, 2560, 2816, 2944, 3072)

TM_DENSE = 512
TM_POST = 512
TM_FFN = 256
FF_CHUNK = 256
TQ_ATTN = 256
KC_ATTN = 512
CONV_HALO = BF16_ROWS
RET_SEQS_PER_STEP = 8
WIN_BLOCKS_PER_STEP = 4
ATTN_SEQS_PER_STEP = 8
ATTN_UNROLL = 4
VMEM_LIMIT = 48 * 1024 * 1024
ATTN_VMEM_EXTRA = 16 * 1024 * 1024


def _cparams(sem, vmem=VMEM_LIMIT):
    return pltpu.CompilerParams(dimension_semantics=sem, vmem_limit_bytes=vmem)


def _dot(a, b):
    return jnp.dot(a, b, preferred_element_type=F32)


def _dot_nt(a, b):
    return lax.dot_general(a, b, (((1,), (1,)), ((), ())), preferred_element_type=F32)


def _sigmoid(x):
    return 1.0 / (1.0 + jnp.exp(-x))


def _rmsnorm(x, gain):
    ms = jnp.mean(x * x, axis=-1, keepdims=True)
    return x * lax.rsqrt(ms + NORM_EPS) * gain


def _interleave(bodies, delays):
    active = list(zip(bodies, delays))
    rnd = 0
    while active:
        active = [(body, d) for body, d in active if d > rnd or next(body, "done") != "done"]
        rnd += 1
        yield


def _round_robin(bodies):
    bodies = list(bodies)
    for _ in _interleave(bodies, [0] * len(bodies)):
        pass


def _resident(stacked_shape, layer):
    zeros = (0,) * (len(stacked_shape) - 1)
    return pl.BlockSpec((None,) + tuple(stacked_shape[1:]), lambda i: (layer,) + zeros,
                        pipeline_mode=pl.Buffered(1))


def _mod_spec(mod, tiles_per_seq):
    blk = (1,) + mod.shape[1:]
    if mod.shape[0] == 1:
        return pl.BlockSpec(blk, lambda i: (0, 0, 0))
    return pl.BlockSpec(blk, lambda i: (i // tiles_per_seq, 0, 0))


def _lane_group_mask(shape, group_width, group):
    lane = lax.broadcasted_iota(jnp.int32, shape, len(shape) - 1)
    return (lane // group_width) == group


def _rope_rows(x, cos, sin_up, sin_dn, half):
    width = x.shape[-1]
    up = pltpu.roll(x, width - half, axis=1)
    dn = pltpu.roll(x, half, axis=1)
    return x * cos + up * sin_up + dn * sin_dn


def _mod_kernel(c_ref, w_ref, b_ref, o_ref):
    c = c_ref[...]
    s = c * _sigmoid(c)
    o_ref[0] = _dot(s.astype(BF16), w_ref[0].astype(BF16)) + b_ref[0]


def _modulation(cond_rows, w_mod, b_mod):
    depth, d, n = w_mod.shape
    tn = 1024
    return pl.pallas_call(
        _mod_kernel,
        grid=(depth, n // tn),
        in_specs=[
            pl.BlockSpec((8, d), lambda l, j: (0, 0)),
            pl.BlockSpec((1, d, tn), lambda l, j: (l, 0, j)),
            pl.BlockSpec((1, 1, tn), lambda l, j: (l, 0, j)),
        ],
        out_specs=pl.BlockSpec((1, 8, tn), lambda l, j: (l, 0, j)),
        out_shape=jax.ShapeDtypeStruct((depth, 8, n), F32),
        compiler_params=_cparams(("arbitrary", "arbitrary")),
        name="modulation",
    )(cond_rows, w_mod, b_mod.reshape(depth, 1, n))


_ROPE_HALF = (16, 8, 8)


def _pre_kernel(*refs, rope, ctx):
    it = iter(refs)
    x_ref, mod_ref, g_ref, w_ref, wuq_ref, wukv_ref, gq_ref, gkv_ref = (next(it) for _ in range(8))
    tab_ref = next(it) if rope else None
    (ret_ref, wq_ref, wke_ref, wve_ref, dq_ref, dk_ref, dv_ref,
     mq_ref, mk_ref, mv_ref) = (next(it) for _ in range(10))
    if ctx:
        wke32_ref, wve32_ref, dk32_ref, dv32_ref, ckv32_ref, mkr32_ref = (next(it) for _ in range(6))

    x = x_ref[...]
    mod = mod_ref[0]
    h = _rmsnorm(x, g_ref[...]) * (1.0 + mod[1:2]) + mod[0:1]
    hb = h.astype(BF16)

    def proj(a, b):
        return _dot(hb, w_ref[:, a:b])

    def rot(v, t):
        if not rope:
            return v
        return _rope_rows(v, tab_ref[3 * t], tab_ref[3 * t + 1], tab_ref[3 * t + 2], _ROPE_HALF[t])

    def store(o_ref, val, t=None, o32_ref=None):
        for j in range(val.shape[1] // LANES):
            v = val[:, LANES * j:LANES * (j + 1)]
            if o32_ref is not None:
                o32_ref[:, LANES * j:LANES * (j + 1)] = v
            if t is not None:
                v = rot(v, t)
            o_ref[:, LANES * j:LANES * (j + 1)] = v.astype(o_ref.dtype)

    cq_in = proj(_C_MCQ, _C_MCKV)
    ckv_in = proj(_C_MCKV, _C_MKR)
    mkr = proj(_C_MKR, _C_END)
    ret_ref[...] = proj(_C_RET, _C_WQ)
    cq = _rmsnorm(cq_in, gq_ref[...])
    ckv = _rmsnorm(ckv_in, gkv_ref[...])
    store(wq_ref, proj(_C_WQ, _C_WK), 0)
    store(wke_ref, proj(_C_WK, _C_WV), 0, wke32_ref if ctx else None)
    mq = _dot(cq.astype(BF16), wuq_ref[...]) * ((MLA_NOPE + MLA_ROPE) ** -0.5 * LOG2E)
    kvp = _dot(ckv.astype(BF16), wukv_ref[...])
    store(wve_ref, proj(_C_WV, _C_DQ), None, wve32_ref if ctx else None)
    store(dq_ref, proj(_C_DQ, _C_DK) * (DIFF_D ** -0.5 * LOG2E), 1)
    store(mq_ref, mq, 2)
    store(dk_ref, proj(_C_DK, _C_DV), 1, dk32_ref if ctx else None)
    store(dv_ref, proj(_C_DV, _C_MCQ), None, dv32_ref if ctx else None)
    mkr_rot = rot(mkr, 2)
    for hd in range(MLA_H):
        mk_ref[:, LANES * hd:LANES * (hd + 1)] = (kvp[:, LANES * hd:LANES * (hd + 1)] + mkr_rot).astype(BF16)
    mv_ref[...] = kvp[:, MLA_H * LANES:].astype(BF16)
    if ctx:
        ckv32_ref[...] = ckv
        mkr32_ref[...] = mkr


def _pre(x2d, mod, g_mix, w_in_p, w_uq_p, w_ukv_p, g_q, g_kv, tabs, *, seq_len, ctx, layer):
    t, d = x2d.shape
    tm = TM_DENSE
    assert seq_len % tm == 0 or (tm % seq_len == 0 and mod.shape[0] == 1 and tabs is None)
    tiles_per_seq = max(seq_len // tm, 1)
    rope = tabs is not None
    const = lambda i: (0, 0)
    row = lambda i: (i, 0)
    in_specs = [
        pl.BlockSpec((tm, d), row),
        _mod_spec(mod, tiles_per_seq),
        pl.BlockSpec((1, d), const),
        _resident(w_in_p.shape, layer),
        _resident(w_uq_p.shape, layer),
        _resident(w_ukv_p.shape, layer),
        pl.BlockSpec((1, MLA_QRANK), const),
        pl.BlockSpec((1, MLA_KVRANK), const),
    ]
    args = [x2d, mod, g_mix.reshape(1, d), w_in_p, w_uq_p, w_ukv_p,
            g_q.reshape(1, MLA_QRANK), g_kv.reshape(1, MLA_KVRANK)]
    if rope:
        in_specs.append(pl.BlockSpec((9, tm, LANES), lambda i: (0, i % tiles_per_seq, 0)))
        args.append(tabs)
    widths = [(1024, F32)] + [(256, BF16)] * 6 + [(512, BF16), (512, BF16), (256, BF16)]
    if ctx:
        widths += [(256, F32)] * 4 + [(128, F32)] * 2
    out_specs = [pl.BlockSpec((tm, w), row) for w, _ in widths]
    out_shape = [jax.ShapeDtypeStruct((t, w), dt) for w, dt in widths]
    return pl.pallas_call(
        functools.partial(_pre_kernel, rope=rope, ctx=ctx),
        grid=(t // tm,),
        in_specs=in_specs,
        out_specs=out_specs,
        out_shape=out_shape,
        compiler_params=_cparams(("arbitrary",)),
        name="pre_ctx" if ctx else "pre_lat",
    )(*args)


def _log_sigmoid(z):
    return jnp.minimum(z, 0.0) - jnp.log(1.0 + jnp.exp(-jnp.abs(z)))


def _ret_kernel(r_ref, s0_ref, dl_ref, ds_ref, dh_ref, o_ref, st_ref,
                s_sc, ob_sc, qdec_sc, kdec_sc, cdec_sc, intra_sc, *, n_chunks):
    p = pl.program_id(1)
    c = pl.program_id(2)
    fwd = p == 1
    chunk = RET_CHUNK
    w = BRANCH_W

    @pl.when(c == 0)
    def _init():
        lgl = _log_sigmoid(dl_ref[0])[0:1, :]
        i = lax.broadcasted_iota(jnp.int32, (chunk, w), 0).astype(F32)
        qe = jnp.where(fwd, i + 1.0, chunk - i)
        ke = jnp.where(fwd, chunk - 1.0 - i, i)
        qdec_sc[...] = jnp.exp(lgl * qe)
        kdec_sc[...] = jnp.exp(lgl * ke)
        rr = lax.broadcasted_iota(jnp.int32, (w, w), 0) // HEAD_W
        cc = lax.broadcasted_iota(jnp.int32, (w, w), 1) // HEAD_W
        cdec_sc[...] = jnp.where(rr == cc, jnp.exp(_log_sigmoid(ds_ref[0]) * float(chunk)), 0.0)
        ii = lax.broadcasted_iota(jnp.int32, (chunk, chunk), 0)
        jj = lax.broadcasted_iota(jnp.int32, (chunk, chunk), 1)
        dist = jnp.where(fwd, ii - jj, jj - ii)
        for hd in range(RET_H):
            lgh = _log_sigmoid(dh_ref[0, hd])[0:1, :]
            intra_sc[hd] = jnp.where(dist >= 0, jnp.exp(lgh * jnp.maximum(dist, 0).astype(F32)), 0.0)

    seqs = r_ref.shape[0]
    head_mask = [_lane_group_mask((chunk, w), HEAD_W, hd) for hd in range(RET_H)]
    rr = lax.broadcasted_iota(jnp.int32, (w, w), 0) // HEAD_W
    cc = lax.broadcasted_iota(jnp.int32, (w, w), 1) // HEAD_W
    block_diag = rr == cc
    outs = [None] * seqs

    def scan_step(sq):
        blk = r_ref[sq]
        q = blk[:, 0:w]
        k = blk[:, w:2 * w] * (RET_DK ** -0.5)
        v = blk[:, 2 * w:3 * w]
        qb = q.astype(BF16)
        kb = k.astype(BF16)
        vb = v.astype(BF16)
        s = jnp.where(c == 0, s0_ref[sq, 0], s_sc[sq])
        yield
        o = _dot(qb, s.astype(BF16)) * qdec_sc[...]
        for hd in range(RET_H):
            sc = _dot_nt(jnp.where(head_mask[hd], qb, jnp.zeros_like(qb)), kb) * intra_sc[hd]
            yield
            o = o + jnp.where(head_mask[hd], _dot(sc.astype(BF16), vb), 0.0)
        outs[sq] = o
        yield
        kd_t = jnp.transpose(k * kdec_sc[...]).astype(BF16)
        s_new = s * cdec_sc[...] + jnp.where(block_diag, _dot(kd_t, vb), 0.0)
        s_sc[sq] = s_new
        yield
        for hd in range(RET_H):
            st_ref[sq, 0, hd] = s_new[HEAD_W * hd:HEAD_W * (hd + 1), HEAD_W * hd:HEAD_W * (hd + 1)]

    def finish(sq):
        tot = outs[sq] + ob_sc[sq, c]
        mu = jnp.zeros_like(tot)
        for hd in range(RET_H):
            m1 = jnp.sum(jnp.where(head_mask[hd], tot, 0.0), axis=-1, keepdims=True) * (1.0 / HEAD_W)
            mu = jnp.where(head_mask[hd], m1, mu)
        yield
        xc = tot - mu
        var = jnp.zeros_like(tot)
        for hd in range(RET_H):
            v1 = jnp.sum(jnp.where(head_mask[hd], xc * xc, 0.0), axis=-1, keepdims=True) * (1.0 / HEAD_W)
            var = jnp.where(head_mask[hd], v1, var)
        yield
        g = r_ref[sq][:, 3 * w:4 * w]
        o_ref[sq] = (xc * lax.rsqrt(var + NORM_EPS) * (g * _sigmoid(g))).astype(o_ref.dtype)

    _round_robin([scan_step(sq) for sq in range(seqs)])

    @pl.when(p == 0)
    def _bwd():
        for sq in range(seqs):
            ob_sc[sq, n_chunks - 1 - c] = outs[sq]

    @pl.when(p == 1)
    def _fwd():
        _round_robin([finish(sq) for sq in range(seqs)])


def _retention(ret4, s0_bd, decay, *, batch, seq_len):
    n = seq_len // RET_CHUNK
    w = BRANCH_W
    seqs = RET_SEQS_PER_STEP if batch % RET_SEQS_PER_STEP == 0 else batch
    dlane = jnp.broadcast_to(jnp.repeat(decay, HEAD_W, axis=1)[:, None, :], (2, 8, w))
    dsub = jnp.broadcast_to(jnp.repeat(decay, HEAD_W, axis=1)[:, :, None], (2, w, w))
    dhead = jnp.broadcast_to(decay[:, :, None, None], (2, RET_H, 8, LANES))
    chunk_of = lambda p, c: c * p + (n - 1 - c) * (1 - p)
    o, st = pl.pallas_call(
        functools.partial(_ret_kernel, n_chunks=n),
        grid=(batch // seqs, 2, n),
        in_specs=[
            pl.BlockSpec((seqs, RET_CHUNK, 4 * w), lambda b, p, c: (b, chunk_of(p, c), 0)),
            pl.BlockSpec((seqs, 1, w, w), lambda b, p, c: (b, 1 - p, 0, 0)),
            pl.BlockSpec((1, 8, w), lambda b, p, c: (1 - p, 0, 0)),
            pl.BlockSpec((1, w, w), lambda b, p, c: (1 - p, 0, 0)),
            pl.BlockSpec((1, RET_H, 8, LANES), lambda b, p, c: (1 - p, 0, 0, 0)),
        ],
        out_specs=[
            pl.BlockSpec((seqs, RET_CHUNK, w), lambda b, p, c: (b, c * p, 0)),
            pl.BlockSpec((seqs, 1, RET_H, RET_DK, RET_DK), lambda b, p, c: (b, 1 - p, 0, 0, 0)),
        ],
        out_shape=[
            jax.ShapeDtypeStruct((batch, seq_len, w), BF16),
            jax.ShapeDtypeStruct((batch, 2, RET_H, RET_DK, RET_DK), F32),
        ],
        scratch_shapes=[
            pltpu.VMEM((seqs, w, w), F32),
            pltpu.VMEM((seqs, n, RET_CHUNK, w), F32),
            pltpu.VMEM((RET_CHUNK, w), F32),
            pltpu.VMEM((RET_CHUNK, w), F32),
            pltpu.VMEM((w, w), F32),
            pltpu.VMEM((RET_H, RET_CHUNK, RET_CHUNK), F32),
        ],
        compiler_params=_cparams(("arbitrary", "arbitrary", "arbitrary")),
        name="retention",
    )(ret4.reshape(batch, seq_len, 4 * w), s0_bd, dlane, dsub, dhead)
    return o.reshape(batch * seq_len, w), st


def _col_fold(acc, x, op):
    for j in range(x.shape[1] // LANES):
        acc = op(acc, x[:, LANES * j:LANES * (j + 1)])
    return acc


_ATTN_GROUPING = {"sink": (4, 1), "diff": (4, 2), "mla": (4, 1)}


def _attn_body(*refs, kind, n_chunks, kc, has_ctx, lam_init, static_chunks):
    it = iter(refs)
    q_ref, k_ref, v_ref = next(it), next(it), next(it)
    if has_ctx:
        if kind == "mla":
            cckv_ref, ckr_ref, wukv_ref = next(it), next(it), next(it)
        else:
            kx_ref, vx_ref = next(it), next(it)
    if kind == "sink":
        sink_ref = next(it)
    if kind == "diff":
        lam_ref, gd_ref = next(it), next(it)
    o_ref = next(it)
    nv, n_groups = _ATTN_GROUPING[kind]
    per_group = lambda: [next(it) for _ in range(n_groups)]
    s_all, m_all, l_all = per_group(), per_group(), per_group()
    qst_all = per_group() if kind != "mla" else None
    sx_all = per_group() if has_ctx else None
    if has_ctx and kind == "mla":
        kx_sc, vx_sc = next(it), next(it)

    tq = q_ref.shape[1]
    w = BRANCH_W
    rows = nv * tq
    unroll = ATTN_UNROLL if n_chunks % ATTN_UNROLL == 0 else 1
    q = q_ref[0]
    log2_scale = WIN_HD ** -0.5 * LOG2E if kind == "sink" else None

    if has_ctx and kind == "mla":
        kvc = _dot(cckv_ref[0].astype(BF16), wukv_ref[...])
        kr = ckr_ref[0]
        for hd in range(MLA_H):
            kx_sc[:, LANES * hd:LANES * (hd + 1)] = (kvc[:, LANES * hd:LANES * (hd + 1)] + kr).astype(BF16)
        vx_sc[...] = kvc[:, MLA_H * LANES:].astype(BF16)

    def rows_of(ref, ch):
        if static_chunks:
            return ref[0, ch * kc:(ch + 1) * kc, :]
        return ref[0, pl.ds(pl.multiple_of(ch * kc, kc), kc), :]

    def k_chunk(ch):
        return rows_of(k_ref, ch)

    def v_chunk(ch):
        return rows_of(v_ref, ch)

    def kx_tile():
        return kx_sc[...] if kind == "mla" else kx_ref[0]

    def vx_tile():
        return vx_sc[...] if kind == "mla" else vx_ref[0]

    def chunk_loop(body, carry):
        if static_chunks:
            for ch in range(n_chunks):
                carry = body(ch, carry)
                yield
        else:
            carry = lax.fori_loop(0, n_chunks, body, carry, unroll=unroll)
            yield
        return carry

    def weighted_values(make_weights, s_sc, sx_sc, n_rows):
        if not static_chunks:
            def body(ch, carry):
                return carry + _dot(make_weights(s_sc[ch]), v_chunk(ch))

            acc = lax.fori_loop(0, n_chunks, body, jnp.zeros((n_rows, w), F32), unroll=unroll)
            yield
            if has_ctx:
                acc = acc + _dot(make_weights(sx_sc[...]), vx_tile())
            return acc
        score_of = [functools.partial(lambda ch: s_sc[ch], ch) for ch in range(n_chunks)]
        value_of = [functools.partial(v_chunk, ch) for ch in range(n_chunks)]
        if has_ctx:
            score_of.append(lambda: sx_sc[...])
            value_of.append(vx_tile)
        acc = None
        ahead = make_weights(score_of[0]())
        for i in range(len(score_of)):
            cur = ahead
            if i + 1 < len(score_of):
                ahead = make_weights(score_of[i + 1]())
            term = _dot(cur, value_of[i]())
            acc = term if acc is None else acc + term
            yield
        return acc

    parts = []

    def group_body(grp):
        s_sc, m_sc, l_sc = s_all[grp], m_all[grp], l_all[grp]
        qst_sc = qst_all[grp] if kind != "mla" else None
        sx_sc = sx_all[grp] if has_ctx else None

        def scores(kt):
            if kind == "mla":
                heads = range(nv * grp, nv * (grp + 1))
                return jnp.concatenate(
                    [_dot_nt(q[:, LANES * hd:LANES * (hd + 1)], kt[:, LANES * hd:LANES * (hd + 1)])
                     for hd in heads], axis=0)
            return _dot_nt(qst_sc[...], kt)

        def probs(s):
            mb = m_sc[...]
            logit = (lambda x: x) if log2_scale is None else (lambda x: x * log2_scale)
            cols = [jnp.exp2(logit(s[:, LANES * j:LANES * (j + 1)]) - mb) for j in range(s.shape[1] // LANES)]
            tot = l_sc[...]
            for col in cols:
                tot = tot + col
            l_sc[...] = tot
            return jnp.concatenate(cols, axis=1)

        if kind != "mla":
            for v in range(nv):
                mask = (_lane_group_mask((tq, w), HEAD_W, v) if kind == "sink"
                        else _lane_group_mask((tq, w), DIFF_D, nv * grp + v))
                qst_sc[v * tq:(v + 1) * tq, :] = jnp.where(mask, q, jnp.zeros_like(q))
        yield

        m_sc[...] = jnp.full((rows, LANES), -jnp.inf, F32)

        def score_body(ch, carry):
            s = scores(k_chunk(ch))
            s_sc[ch] = s
            m_sc[...] = _col_fold(m_sc[...], s, jnp.maximum)
            return carry

        yield from chunk_loop(score_body, 0)
        if has_ctx:
            s = scores(kx_tile())
            sx_sc[...] = s
            m_sc[...] = _col_fold(m_sc[...], s, jnp.maximum)
        yield
        m2 = jnp.max(m_sc[...], axis=-1, keepdims=True)
        if log2_scale is not None:
            m2 = m2 * log2_scale
        if kind == "sink":
            sink2 = LOG2E * jnp.concatenate(
                [jnp.broadcast_to(sink_ref[v:v + 1, 0:1], (tq, 1)) for v in range(nv)], axis=0)
            m2 = jnp.maximum(m2, sink2)
        heads = range(nv * grp, nv * (grp + 1))
        m_sc[...] = jnp.broadcast_to(m2, (rows, LANES))
        l_sc[...] = jnp.zeros((rows, LANES), F32)
        yield

        if kind != "diff":
            acc = yield from weighted_values(lambda s: probs(s).astype(BF16), s_sc, sx_sc, rows)
            yield
            l = jnp.sum(l_sc[...], axis=-1, keepdims=True)
            if kind == "sink":
                l = l + jnp.exp2(sink2 - m2)
            pv = acc * (1.0 / l)
            for v, hd in enumerate(heads):
                parts.append(jnp.where(_lane_group_mask((tq, w), HEAD_W, hd), pv[v * tq:(v + 1) * tq], 0.0))
        else:
            def exp_body(ch, carry):
                s_sc[ch] = probs(s_sc[ch])
                return carry

            yield from chunk_loop(exp_body, 0)
            if has_ctx:
                sx_sc[...] = probs(sx_sc[...])
            yield
            dl = lam_ref[...]
            lam = (jnp.exp(jnp.sum(dl[0:1] * dl[1:2], axis=-1, keepdims=True))
                   - jnp.exp(jnp.sum(dl[2:3] * dl[3:4], axis=-1, keepdims=True)) + lam_init)
            l = jnp.sum(l_sc[...], axis=-1, keepdims=True)
            inv_l = 1.0 / l
            for v in range(2):
                first, second = slice(2 * v * tq, (2 * v + 1) * tq), slice((2 * v + 1) * tq, (2 * v + 2) * tq)
                l_sc[second, :] = jnp.broadcast_to(lam * l[first] * inv_l[second], (tq, LANES))
            yield

            def weights(pr):
                cols = []
                for j in range(pr.shape[1] // LANES):
                    x = pr[:, LANES * j:LANES * (j + 1)]
                    cols.append(jnp.concatenate(
                        [x[0:tq] - x[tq:2 * tq] * l_sc[tq:2 * tq, :],
                         x[2 * tq:3 * tq] - x[3 * tq:4 * tq] * l_sc[3 * tq:4 * tq, :]], axis=0))
                return jnp.concatenate(cols, axis=1).astype(BF16)

            acc = yield from weighted_values(weights, s_sc, sx_sc, 2 * tq)
            yield
            for v in range(2):
                head_out = acc[v * tq:(v + 1) * tq] * inv_l[2 * v * tq:(2 * v + 1) * tq]
                parts.append(jnp.where(_lane_group_mask((tq, w), HEAD_W, 2 * grp + v), head_out, 0.0))

    lag = (n_chunks + 2) if static_chunks else 0
    yield from _interleave([group_body(grp) for grp in range(n_groups)], [lag * grp for grp in range(n_groups)])
    out = parts[0]
    for part in parts[1:]:
        out = out + part
    if kind == "diff":
        ms = jnp.zeros_like(out)
        for hd in range(DIFF_H):
            mh = _lane_group_mask((tq, w), HEAD_W, hd)
            m1 = jnp.sum(jnp.where(mh, out * out, 0.0), axis=-1, keepdims=True) * (1.0 / HEAD_W)
            ms = jnp.where(mh, m1, ms)
        out = out * lax.rsqrt(ms + NORM_EPS) * gd_ref[...] * (1.0 - lam_init)
    o_ref[0] = out.astype(o_ref.dtype)


def _attn_multi_kernel(*refs, seqs, batched, **kw):
    n_io = len(batched)
    bodies = []
    for sq in range(seqs):
        view = [r.at[pl.ds(sq, 1)] if flag else r for r, flag in zip(refs[:n_io], batched)]
        view += [r.at[sq] for r in refs[n_io:]]
        bodies.append(_attn_body(*view, **kw))
    _round_robin(bodies)


def _attention(kind, q, k, v, *, ctx_args=(), params=(), lam_init=0.0):
    b, n, wq = q.shape
    nk = k.shape[1]
    tq = min(TQ_ATTN, n)
    kc = min(KC_ATTN, nk)
    n_chunks = nk // kc
    has_ctx = len(ctx_args) > 0
    w = BRANCH_W
    nv, n_groups = _ATTN_GROUPING[kind]
    rows = nv * tq
    static_chunks = n_chunks > 1
    seqs = ATTN_SEQS_PER_STEP if (n == tq and b % ATTN_SEQS_PER_STEP == 0) else 1
    per_b = lambda bb, i: (bb, 0, 0)
    shared = lambda bb, i: (0, 0)
    in_specs = [
        pl.BlockSpec((seqs, tq, wq), lambda bb, i: (bb, i, 0)),
        pl.BlockSpec((seqs, nk, k.shape[2]), per_b),
        pl.BlockSpec((seqs, nk, w), per_b),
    ]
    args = [q, k, v]
    batched = [True, True, True]
    for a in ctx_args:
        if a.ndim == 3:
            in_specs.append(pl.BlockSpec((seqs,) + a.shape[1:], per_b))
        else:
            in_specs.append(pl.BlockSpec(a.shape, shared))
        batched.append(a.ndim == 3)
        args.append(a)
    for a in params:
        in_specs.append(pl.BlockSpec(a.shape, shared))
        batched.append(False)
        args.append(a)
    batched.append(True)
    grouped = [(n_chunks, rows, kc, F32), (rows, LANES, F32), (rows, LANES, F32)]
    if kind != "mla":
        grouped.append((rows, w, BF16))
    single = []
    if has_ctx:
        m_ctx = ctx_args[0].shape[1]
        grouped.append((rows, m_ctx, F32))
        if kind == "mla":
            single = [(m_ctx, MLA_H * LANES, BF16), (m_ctx, w, BF16)]
    scratch = [sh for sh in grouped for _ in range(n_groups)] + single
    scratch_bytes = sum(seqs * math.prod(sh[:-1]) * jnp.dtype(sh[-1]).itemsize for sh in scratch)
    return pl.pallas_call(
        functools.partial(_attn_multi_kernel, seqs=seqs, batched=tuple(batched), kind=kind, n_chunks=n_chunks,
                          kc=kc, has_ctx=has_ctx, lam_init=lam_init, static_chunks=static_chunks),
        grid=(b // seqs, n // tq),
        in_specs=in_specs,
        out_specs=pl.BlockSpec((seqs, tq, w), lambda bb, i: (bb, i, 0)),
        out_shape=jax.ShapeDtypeStruct((b, n, w), BF16),
        scratch_shapes=[pltpu.VMEM((seqs,) + sh[:-1], sh[-1]) for sh in scratch],
        compiler_params=_cparams(("arbitrary", "arbitrary"), vmem=max(VMEM_LIMIT, scratch_bytes + ATTN_VMEM_EXTRA)),
        name="attn_" + kind + ("_lat" if has_ctx else "_ctx"),
    )(*args)


def _win_kernel(q_ref, kp_ref, kc_ref, kn_ref, vp_ref, vc_ref, vn_ref, kx_ref, vx_ref, sink_ref, o_ref):
    n_steps = pl.num_programs(1)
    step = pl.program_id(1)
    tq = WINDOW
    nblk = q_ref.shape[1] // tq
    w = BRANCH_W
    nh = 4
    rows = nh * tq
    scale = WIN_HD ** -0.5
    head_mask = [_lane_group_mask((tq, w), HEAD_W, hd) for hd in range(nh)]
    ii = lax.broadcasted_iota(jnp.int32, (rows, tq), 0) % tq
    jj = lax.broadcasted_iota(jnp.int32, (rows, tq), 1)
    sink_col = jnp.concatenate([jnp.broadcast_to(sink_ref[hd:hd + 1, 0:1], (tq, 1)) for hd in range(nh)], axis=0)
    kx = kx_ref[0]
    vx = vx_ref[0]

    def rows_of(ref, blk):
        return ref[0, blk * tq:(blk + 1) * tq, :]

    def block(blk):
        q = rows_of(q_ref, blk)
        qst = jnp.concatenate([jnp.where(head_mask[hd], q, jnp.zeros_like(q)) for hd in range(nh)], axis=0)
        kp, vp = (kp_ref[0], vp_ref[0]) if blk == 0 else (rows_of(kc_ref, blk - 1), rows_of(vc_ref, blk - 1))
        kn, vn = (kn_ref[0], vn_ref[0]) if blk == nblk - 1 else (rows_of(kc_ref, blk + 1), rows_of(vc_ref, blk + 1))
        mask_prev = jj >= ii
        mask_next = jj <= ii
        if blk == 0:
            mask_prev = mask_prev & (step > 0)
        if blk == nblk - 1:
            mask_next = mask_next & (step < n_steps - 1)
        yield
        sp = jnp.where(mask_prev, _dot_nt(qst, kp) * scale, NEG_INF)
        sc = _dot_nt(qst, rows_of(kc_ref, blk)) * scale
        sn = jnp.where(mask_next, _dot_nt(qst, kn) * scale, NEG_INF)
        sx = _dot_nt(qst, kx) * scale
        yield
        m = jnp.maximum(jnp.maximum(jnp.max(sp, axis=-1, keepdims=True), jnp.max(sc, axis=-1, keepdims=True)),
                        jnp.maximum(jnp.max(sn, axis=-1, keepdims=True), jnp.max(sx, axis=-1, keepdims=True)))
        m = jnp.maximum(m, sink_col)
        yield
        pp, pc, pn, px = (jnp.exp(s - m) for s in (sp, sc, sn, sx))
        l = (jnp.sum(pp, axis=-1, keepdims=True) + jnp.sum(pc, axis=-1, keepdims=True)
             + jnp.sum(pn, axis=-1, keepdims=True) + jnp.sum(px, axis=-1, keepdims=True) + jnp.exp(sink_col - m))
        yield
        pv = (_dot(pp.astype(BF16), vp) + _dot(pc.astype(BF16), rows_of(vc_ref, blk))
              + _dot(pn.astype(BF16), vn) + _dot(px.astype(BF16), vx)) * (1.0 / l)
        yield
        out = jnp.zeros((tq, w), F32)
        for hd in range(nh):
            out = out + jnp.where(head_mask[hd], pv[hd * tq:(hd + 1) * tq], 0.0)
        o_ref[0, blk * tq:(blk + 1) * tq, :] = out.astype(o_ref.dtype)

    _round_robin([block(blk) for blk in range(nblk)])


def _window_attention(q, k, v, kx, vx, sink_tile):
    b, n, w = q.shape
    tq = WINDOW
    span = WIN_BLOCKS_PER_STEP * tq
    nb = n // tq
    m_ctx = kx.shape[1]
    edge = lambda f: pl.BlockSpec((1, tq, w), f)
    prev = lambda bb, i: (bb, jnp.maximum(i * WIN_BLOCKS_PER_STEP - 1, 0), 0)
    nxt = lambda bb, i: (bb, jnp.minimum((i + 1) * WIN_BLOCKS_PER_STEP, nb - 1), 0)
    mid = pl.BlockSpec((1, span, w), lambda bb, i: (bb, i, 0))
    per_b = lambda bb, i: (bb, 0, 0)
    return pl.pallas_call(
        _win_kernel,
        grid=(b, n // span),
        in_specs=[mid, edge(prev), mid, edge(nxt), edge(prev), mid, edge(nxt),
                  pl.BlockSpec((1, m_ctx, w), per_b), pl.BlockSpec((1, m_ctx, w), per_b),
                  pl.BlockSpec(sink_tile.shape, lambda bb, i: (0, 0))],
        out_specs=mid,
        out_shape=jax.ShapeDtypeStruct((b, n, w), BF16),
        compiler_params=_cparams(("arbitrary", "arbitrary")),
        name="attn_window",
    )(q, k, k, k, v, v, v, kx, vx, sink_tile)


def _post_kernel(x_ref, mod_ref, g_ref, o0_ref, o1_ref, o2_ref, o3_ref,
                 wg_ref, bg_ref, wb_ref, wo_ref, out_ref, mix_sc):
    d = D_MODEL
    x = x_ref[...]
    mod = mod_ref[0]
    h = _rmsnorm(x, g_ref[...]) * (1.0 + mod[1:2]) + mod[0:1]
    hb = h.astype(BF16)
    cw = 256
    for j in range(d // cw):
        mixed = None
        for nbr, o_ref in enumerate((o0_ref, o1_ref, o2_ref, o3_ref)):
            lo = nbr * d + j * cw
            gate = _sigmoid(_dot(hb, wg_ref[:, lo:lo + cw]) + bg_ref[:, lo:lo + cw])
            term = gate * _dot(o_ref[...], wb_ref[nbr, :, j * cw:(j + 1) * cw])
            mixed = term if mixed is None else mixed + term
        mix_sc[:, j * cw:(j + 1) * cw] = mixed.astype(BF16)
    out_ref[...] = x + mod[2:3] * _dot(mix_sc[...], wo_ref[...])


def _post(x2d, mod, g_mix, outs, w_gate, b_gate, w_branch, w_out, *, seq_len, layer):
    t, d = x2d.shape
    tm = TM_POST
    assert seq_len % tm == 0 or (tm % seq_len == 0 and mod.shape[0] == 1)
    tiles_per_seq = max(seq_len // tm, 1)
    const2 = lambda i: (0, 0)
    row = lambda i: (i, 0)
    return pl.pallas_call(
        _post_kernel,
        grid=(t // tm,),
        in_specs=[
            pl.BlockSpec((tm, d), row),
            _mod_spec(mod, tiles_per_seq),
            pl.BlockSpec((1, d), const2),
        ] + [pl.BlockSpec((tm, BRANCH_W), row)] * 4 + [
            _resident(w_gate.shape, layer),
            pl.BlockSpec((1, 4 * d), const2),
            _resident(w_branch.shape, layer),
            _resident(w_out.shape, layer),
        ],
        out_specs=pl.BlockSpec((tm, d), row),
        out_shape=jax.ShapeDtypeStruct((t, d), F32),
        scratch_shapes=[pltpu.VMEM((tm, d), BF16)],
        compiler_params=_cparams(("arbitrary",)),
        name="post",
    )(x2d, mod, g_mix.reshape(1, d), *outs, w_gate, b_gate.reshape(1, 4 * d), w_branch, w_out)


def _ffn_kernel(*refs, tiles_per_seq, final):
    halo = tiles_per_seq > 1
    it = iter(refs)
    x_ref = next(it)
    if halo:
        xp_ref, xn_ref = next(it), next(it)
    mod_ref, g_ref, wup_ref, bup_ref, wcv_ref, bcv_ref, wdn_ref = (next(it) for _ in range(7))
    gf_ref = next(it) if final else None
    out_ref = next(it)
    h_sc = next(it)

    tm = x_ref.shape[0]
    hr = CONV_HALO
    n_ff = D_FF // FF_CHUNK
    mod = mod_ref[0]
    g = g_ref[...]

    def norm_mod(rows):
        return (_rmsnorm(rows, g) * (1.0 + mod[4:5]) + mod[3:4]).astype(BF16)

    x = x_ref[...]
    h_sc[hr:hr + tm, :] = norm_mod(x)
    if halo:
        h_sc[0:hr, :] = norm_mod(xp_ref[...])
        h_sc[hr + tm:, :] = norm_mod(xn_ref[...])
        i = pl.program_id(0)
        keep_lo = jnp.where((i % tiles_per_seq) == 0, 0.0, 1.0)
        keep_hi = jnp.where((i % tiles_per_seq) == tiles_per_seq - 1, 0.0, 1.0)
        rid = lax.broadcasted_iota(jnp.int32, (8, 1), 0)
        edge_lo = jnp.where(rid == 7, keep_lo, 1.0)
        edge_hi = jnp.where(rid == 0, keep_hi, 1.0)

    def up(ch):
        lo = ch * FF_CHUNK
        if halo:
            u = _dot(h_sc[...], wup_ref[:, lo:lo + FF_CHUNK]) + bup_ref[:, lo:lo + FF_CHUNK]
            return jnp.concatenate([u[:hr - 8], u[hr - 8:hr] * edge_lo, u[hr:hr + tm],
                                    u[hr + tm:hr + tm + 8] * edge_hi, u[hr + tm + 8:]], axis=0)
        pad = jnp.zeros((hr, FF_CHUNK), F32)
        u = _dot(h_sc[hr:hr + tm, :], wup_ref[:, lo:lo + FF_CHUNK]) + bup_ref[:, lo:lo + FF_CHUNK]
        return jnp.concatenate([pad, u, pad], axis=0)

    def conv(ch, u):
        lo = ch * FF_CHUNK
        wc = wcv_ref[:, lo:lo + FF_CHUNK]
        before = pltpu.roll(u, 1, axis=0)[hr:hr + tm]
        after = pltpu.roll(u, u.shape[0] - 1, axis=0)[hr:hr + tm]
        return (bcv_ref[:, lo:lo + FF_CHUNK] + wc[0:1] * before + wc[1:2] * u[hr:hr + tm] + wc[2:3] * after)

    u_val, u_gate = up(0), up(n_ff)
    act = None
    for c in range(n_ff + 1):
        if c + 1 < n_ff:
            u_val_next, u_gate_next = up(c + 1), up(c + 1 + n_ff)
        if c >= 1:
            term = _dot(act, wdn_ref[(c - 1) * FF_CHUNK:c * FF_CHUNK, :])
            acc = term if c == 1 else acc + term
        if c < n_ff:
            a = conv(c, u_val)
            gg = conv(c + n_ff, u_gate)
            act = (gg * _sigmoid(gg) * a).astype(BF16)
            u_val, u_gate = u_val_next, u_gate_next
    y = x + mod[5:6] * acc
    if final:
        y = _rmsnorm(y, gf_ref[...])
    out_ref[...] = y


def _ffn(x2d, mod, g_ffn, w_up, b_up, w_conv, b_conv, w_down, g_final, *, seq_len, layer):
    t, d = x2d.shape
    tm = min(TM_FFN, seq_len)
    hr = CONV_HALO
    tiles_per_seq = seq_len // tm
    halo = tiles_per_seq > 1
    final = g_final is not None
    hb = tm // hr
    n_hblocks = t // hr
    const2 = lambda i: (0, 0)
    row = lambda i: (i, 0)
    in_specs = [pl.BlockSpec((tm, d), row)]
    args = [x2d]
    if halo:
        in_specs += [pl.BlockSpec((hr, d), lambda i: (jnp.maximum(i * hb - 1, 0), 0)),
                     pl.BlockSpec((hr, d), lambda i: (jnp.minimum((i + 1) * hb, n_hblocks - 1), 0))]
        args += [x2d, x2d]
    in_specs += [
        _mod_spec(mod, tiles_per_seq),
        pl.BlockSpec((1, d), const2),
        _resident(w_up.shape, layer),
        pl.BlockSpec(b_up.shape, const2),
        pl.BlockSpec(w_conv.shape, const2),
        pl.BlockSpec(b_conv.shape, const2),
        _resident(w_down.shape, layer),
    ]
    args += [mod, g_ffn.reshape(1, d), w_up, b_up, w_conv, b_conv, w_down]
    if final:
        in_specs.append(pl.BlockSpec((1, d), const2))
        args.append(g_final.reshape(1, d))
    return pl.pallas_call(
        functools.partial(_ffn_kernel, tiles_per_seq=tiles_per_seq, final=final),
        grid=(t // tm,),
        in_specs=in_specs,
        out_specs=pl.BlockSpec((tm, d), row),
        out_shape=jax.ShapeDtypeStruct((t, d), F32),
        scratch_shapes=[
            pltpu.VMEM((tm + 2 * hr, d), BF16),
        ],
        compiler_params=_cparams(("arbitrary",)),
        name=("ffn_final" if final else "ffn") + ("_lat" if halo else "_ctx"),
    )(*args)


def _rope_tables(n_tok):
    rows = n_tok // GRID_W

    def axis(n_pos, half):
        freqs = ROPE_BASE ** (-jnp.arange(half, dtype=F32) / half)
        ang = jnp.arange(n_pos).astype(F32)[:, None] * freqs[None, :]
        return jnp.cos(ang), jnp.sin(ang)

    def unit(width):
        half = width // 4
        cr, sr = (jnp.repeat(t, GRID_W, axis=0) for t in axis(rows, half))
        cc, sc = (jnp.tile(t, (rows, 1)) for t in axis(GRID_W, half))
        z = jnp.zeros_like(sr)
        return (jnp.concatenate([cr, cr, cc, cc], axis=1),
                jnp.concatenate([-sr, z, -sc, z], axis=1),
                jnp.concatenate([z, sr, z, sc], axis=1))

    a = [jnp.tile(t, (1, 2)) for t in unit(64)]
    b = [jnp.tile(t, (1, 4)) for t in unit(32)]
    cu = unit(32)
    ones = jnp.ones((n_tok, 64), F32)
    z64 = jnp.zeros((n_tok, 64), F32)
    z32 = jnp.zeros((n_tok, 32), F32)
    c = [jnp.concatenate([ones, cu[0], ones[:, :32]], axis=1),
         jnp.concatenate([z64, cu[1], z32], axis=1),
         jnp.concatenate([z64, cu[2], z32], axis=1)]
    return jnp.stack(a + b + c, axis=0)


def _dup_groups(w):
    g0, g1 = w[..., :HEAD_W], w[..., HEAD_W:]
    return jnp.concatenate([g0, g0, g1, g1], axis=-1)


def _prep_weights(p):
    depth = p["w_in"].shape[0]
    cuts = np.cumsum((0,) + IN_SPLITS)
    parts = [p["w_in"][:, :, cuts[i]:cuts[i + 1]] for i in range(len(IN_SPLITS))]
    rq, rk, rv, rg, wq, wk, wv, dq, dk, dv, mcq, mckv, mkr = parts
    mkr128 = jnp.pad(mkr, ((0, 0), (0, 0), (MLA_NOPE, LANES - MLA_NOPE - MLA_ROPE)))
    w_in_p = jnp.concatenate([rq, rk, rv, rg, wq, _dup_groups(wk), _dup_groups(wv), dq, dk, dv,
                              mcq, mckv, mkr128], axis=2).astype(BF16)
    w_uq = p["w_mla_uq"].reshape(depth, MLA_QRANK, MLA_H, MLA_NOPE + MLA_ROPE)
    w_uq_p = jnp.pad(w_uq, ((0, 0), (0, 0), (0, 0), (0, LANES - MLA_NOPE - MLA_ROPE)))
    w_ukv = p["w_mla_ukv"].reshape(depth, MLA_KVRANK, MLA_H, MLA_NOPE + MLA_V)
    w_uk = jnp.pad(w_ukv[..., :MLA_NOPE], ((0, 0), (0, 0), (0, 0), (0, LANES - MLA_NOPE)))
    w_uv = w_ukv[..., MLA_NOPE:].reshape(depth, MLA_KVRANK, MLA_H * MLA_V)
    sink = jnp.broadcast_to(p["win_sink"].astype(F32)[:, :, None], (depth, 4, LANES))
    return dict(
        w_in_p=w_in_p,
        w_uq_p=w_uq_p.reshape(depth, MLA_QRANK, MLA_H * LANES).astype(BF16),
        w_ukv_p=jnp.concatenate([w_uk.reshape(depth, MLA_KVRANK, MLA_H * LANES), w_uv], axis=2).astype(BF16),
        w_gate=p["w_gate"].astype(BF16),
        w_branch=p["w_branch"].astype(BF16),
        w_out=p["w_out"].astype(BF16),
        w_up=p["w_up"].astype(BF16),
        w_down=p["w_down"].astype(BF16),
        sink_tile8=jnp.pad(sink, ((0, 0), (0, 4), (0, 0))),
        g_diff4=jnp.tile(p["g_diff"], (1, DIFF_H)).reshape(depth, 1, BRANCH_W),
    )


def _block_diag_state(s):
    b = s.shape[0]
    eye = jnp.eye(RET_H, dtype=s.dtype)
    bd = s[:, :, :, :, None, :] * eye[None, None, :, None, :, None]
    return bd.reshape(b, 2, RET_H * RET_DK, RET_H * RET_DK)


def _mod_rows(mods_l, start, count):
    m = mods_l[start:start + count].reshape(count, 6, D_MODEL)
    return jnp.pad(m, ((0, 0), (0, 2), (0, 0)))


def kernel(x_prompt, x_sample, state_ret, cache_win_k, cache_win_v, cache_diff_k, cache_diff_v, cache_mla_ckv, cache_mla_krope, c, c_ctx, w_mod, b_mod, g_mix, w_in, ret_decay, win_sink, diff_lambda, g_diff, g_mla_q, w_mla_uq, g_mla_kv, w_mla_ukv, w_branch, w_gate, b_gate, w_out, g_ffn, w_up, b_up, w_conv, b_conv, w_down, g_final):
    d = D_MODEL
    bc, lc, _ = x_prompt.shape
    bl, ll, _ = x_sample.shape
    m_ctx = cache_win_k.shape[2]

    cond_rows = jnp.zeros((8, d), F32).at[0].set(c_ctx).at[1:1 + bl].set(c)
    mods = _modulation(cond_rows, w_mod, b_mod)
    tabs = _rope_tables(ll)

    xp = x_prompt.reshape(bc * lc, d)
    xs = x_sample.reshape(bl * ll, d)
    produced = [[] for _ in range(7)]
    p = _prep_weights(dict(w_in=w_in, w_mla_uq=w_mla_uq, w_mla_ukv=w_mla_ukv, w_gate=w_gate, w_branch=w_branch,
                           w_out=w_out, w_up=w_up, w_down=w_down, win_sink=win_sink, g_diff=g_diff))
    zero_state = jnp.zeros((bc, 2, BRANCH_W, BRANCH_W), F32)
    for li in range(DEPTH):
        sink_tile8, g_diff4 = p["sink_tile8"][li], p["g_diff4"][li]
        ffn_small = (b_up[li].reshape(1, -1), w_conv[li], b_conv[li].reshape(1, -1))
        lam_init = 0.8 - 0.6 * math.exp(-0.3 * li)
        mod_c = _mod_rows(mods[li], 0, 1)
        mod_l = _mod_rows(mods[li], 1, bl)
        final_g = g_final if li == DEPTH - 1 else None

        (ret4, wq, wke, wve, dq, dk, dv, mq, mk, mv,
         wke32, wve32, dk32, dv32, ckv32, mkr32) = _pre(
            xp, mod_c, g_mix[li], p["w_in_p"], p["w_uq_p"], p["w_ukv_p"], g_mla_q[li], g_mla_kv[li], None,
            seq_len=lc, ctx=True, layer=li)
        o_ret, st = _retention(ret4, zero_state, ret_decay[li], batch=bc, seq_len=lc)
        r3 = lambda a: a.reshape(bc, lc, a.shape[-1])
        o_win = _attention("sink", r3(wq), r3(wke), r3(wve), params=(sink_tile8,))
        o_diff = _attention("diff", r3(dq), r3(dk), r3(dv), params=(diff_lambda[li], g_diff4), lam_init=lam_init)
        o_mla = _attention("mla", r3(mq), r3(mk), r3(mv))
        f2 = lambda a: a.reshape(bc * lc, BRANCH_W)
        xp = _post(xp, mod_c, g_mix[li], (o_ret, f2(o_win), f2(o_diff), f2(o_mla)),
                   p["w_gate"], b_gate[li], p["w_branch"], p["w_out"], seq_len=lc, layer=li)
        xp = _ffn(xp, mod_c, g_ffn[li], p["w_up"], *ffn_small, p["w_down"], final_g, seq_len=lc, layer=li)
        undup = lambda a: a.reshape(bc, lc, 2, 2, HEAD_W)[:, :, :, 0, :]
        produced[0].append(st)
        produced[1].append(undup(wke32))
        produced[2].append(undup(wve32))
        produced[3].append(dk32.reshape(bc, lc, DIFF_H, 2 * DIFF_D))
        produced[4].append(dv32.reshape(bc, lc, DIFF_H, 2 * DIFF_D))
        produced[5].append(ckv32.reshape(bc, lc, MLA_KVRANK))
        produced[6].append(mkr32.reshape(bc, lc, LANES)[:, :, MLA_NOPE:MLA_NOPE + MLA_ROPE])

        (ret4, wq, wke, wve, dq, dk, dv, mq, mk, mv) = _pre(
            xs, mod_l, g_mix[li], p["w_in_p"], p["w_uq_p"], p["w_ukv_p"], g_mla_q[li], g_mla_kv[li], tabs,
            seq_len=ll, ctx=False, layer=li)
        o_ret, _ = _retention(ret4, _block_diag_state(state_ret[:, li]), ret_decay[li], batch=bl, seq_len=ll)
        r3 = lambda a: a.reshape(bl, ll, a.shape[-1])
        kx_win = _dup_groups(cache_win_k[:, li].reshape(bl, m_ctx, 2 * HEAD_W)).astype(BF16)
        vx_win = _dup_groups(cache_win_v[:, li].reshape(bl, m_ctx, 2 * HEAD_W)).astype(BF16)
        o_win = _window_attention(r3(wq), r3(wke), r3(wve), kx_win, vx_win, sink_tile8)
        kx_diff = cache_diff_k[:, li].reshape(bl, m_ctx, BRANCH_W).astype(BF16)
        vx_diff = cache_diff_v[:, li].reshape(bl, m_ctx, BRANCH_W).astype(BF16)
        o_diff = _attention("diff", r3(dq), r3(dk), r3(dv), ctx_args=(kx_diff, vx_diff),
                            params=(diff_lambda[li], g_diff4), lam_init=lam_init)
        kr128 = jnp.pad(cache_mla_krope[:, li], ((0, 0), (0, 0), (MLA_NOPE, LANES - MLA_NOPE - MLA_ROPE)))
        o_mla = _attention("mla", r3(mq), r3(mk), r3(mv), ctx_args=(cache_mla_ckv[:, li], kr128, p["w_ukv_p"][li]))
        f2 = lambda a: a.reshape(bl * ll, BRANCH_W)
        xs = _post(xs, mod_l, g_mix[li], (o_ret, f2(o_win), f2(o_diff), f2(o_mla)),
                   p["w_gate"], b_gate[li], p["w_branch"], p["w_out"], seq_len=ll, layer=li)
        xs = _ffn(xs, mod_l, g_ffn[li], p["w_up"], *ffn_small, p["w_down"], final_g, seq_len=ll, layer=li)

    y_prompt = xp.reshape(bc, lc, d)
    y_sample = xs.reshape(bl, ll, d)
    stack = lambda lst: jnp.stack(lst, axis=1)
    return (y_prompt, y_sample, stack(produced[0]),
            stack(produced[1]), stack(produced[2]), stack(produced[3]), stack(produced[4]),
            stack(produced[5]), stack(produced[6]))
```

```python
import functools
import math

import numpy as np
import jax
import jax.numpy as jnp
from jax import lax
from jax.experimental import pallas as pl
from jax.experimental.pallas import tpu as pltpu

F32 = jnp.float32
BF16 = jnp.bfloat16

D_MODEL = 1024
DEPTH = 2
GRID_W = 64
ROPE_BASE = 10000.0
NORM_EPS = 1e-6
NEG_INF = -1e30
LOG2E = 1.4426950408889634
RET_H = 4
RET_DK = 64
RET_CHUNK = 128
WIN_HD = 64
WINDOW = 128
DIFF_H = 4
DIFF_D = 32
MLA_H = 4
MLA_NOPE = 64
MLA_ROPE = 32
MLA_V = 64
MLA_QRANK = 256
MLA_KVRANK = 128
D_FF = 2816
IN_SPLITS = (256, 256, 256, 256, 256, 128, 128, 256, 256, 256, 256, 128, 32)

LANES = 128
BF16_ROWS = 16
BRANCH_W = 256
HEAD_W = 64

_C_RET, _C_WQ, _C_WK, _C_WV, _C_DQ, _C_DK, _C_DV, _C_MCQ, _C_MCKV, _C_MKR, _C_END = (
    0, 1024, 1280, 1536, 1792, 2048, 2304, 2560, 2816, 2944, 3072)

TM_DENSE = 1024
TM_POST = 1024
TM_FFN = 256
FF_CHUNK = 256
TQ_ATTN = 256
KC_ATTN = 512
CONV_HALO = BF16_ROWS
RET_SEQS_PER_STEP = 8
WIN_BLOCKS_PER_STEP = 8
ATTN_SEQS_PER_STEP = 8
ATTN_UNROLL = 4
VMEM_LIMIT = 48 * 1024 * 1024
ATTN_VMEM_EXTRA = 16 * 1024 * 1024


def _cparams(sem, vmem=VMEM_LIMIT):
    return pltpu.CompilerParams(dimension_semantics=sem, vmem_limit_bytes=vmem)


def _dot(a, b):
    return jnp.dot(a, b, preferred_element_type=F32)


def _dot_nt(a, b):
    return lax.dot_general(a, b, (((1,), (1,)), ((), ())), preferred_element_type=F32)


def _sigmoid(x):
    return 1.0 / (1.0 + jnp.exp(-x))


def _rmsnorm(x, gain):
    ms = jnp.mean(x * x, axis=-1, keepdims=True)
    return x * lax.rsqrt(ms + NORM_EPS) * gain


def _interleave(bodies, delays):
    active = list(zip(bodies, delays))
    rnd = 0
    while active:
        active = [(body, d) for body, d in active if d > rnd or next(body, "done") != "done"]
        rnd += 1
        yield


def _round_robin(bodies):
    bodies = list(bodies)
    for _ in _interleave(bodies, [0] * len(bodies)):
        pass


def _resident(stacked_shape, layer):
    zeros = (0,) * (len(stacked_shape) - 1)
    return pl.BlockSpec((None,) + tuple(stacked_shape[1:]), lambda i: (layer,) + zeros,
                        pipeline_mode=pl.Buffered(1))


def _mod_spec(mod, tiles_per_seq):
    blk = (1,) + mod.shape[1:]
    if mod.shape[0] == 1:
        return pl.BlockSpec(blk, lambda i: (0, 0, 0))
    return pl.BlockSpec(blk, lambda i: (i // tiles_per_seq, 0, 0))


def _lane_group_mask(shape, group_width, group):
    lane = lax.broadcasted_iota(jnp.int32, shape, len(shape) - 1)
    return (lane // group_width) == group


def _rope_rows(x, cos, sin_up, sin_dn, half):
    width = x.shape[-1]
    up = pltpu.roll(x, width - half, axis=1)
    dn = pltpu.roll(x, half, axis=1)
    return x * cos + up * sin_up + dn * sin_dn


def _mod_kernel(c_ref, w_ref, b_ref, o_ref):
    c = c_ref[...]
    s = c * _sigmoid(c)
    o_ref[0] = _dot(s.astype(BF16), w_ref[0].astype(BF16)) + b_ref[0]


def _modulation(cond_rows, w_mod, b_mod):
    depth, d, n = w_mod.shape
    tn = 1024
    return pl.pallas_call(
        _mod_kernel,
        grid=(depth, n // tn),
        in_specs=[
            pl.BlockSpec((8, d), lambda l, j: (0, 0)),
            pl.BlockSpec((1, d, tn), lambda l, j: (l, 0, j)),
            pl.BlockSpec((1, 1, tn), lambda l, j: (l, 0, j)),
        ],
        out_specs=pl.BlockSpec((1, 8, tn), lambda l, j: (l, 0, j)),
        out_shape=jax.ShapeDtypeStruct((depth, 8, n), F32),
        compiler_params=_cparams(("arbitrary", "arbitrary")),
        name="modulation",
    )(cond_rows, w_mod, b_mod.reshape(depth, 1, n))


_ROPE_HALF = (16, 8, 8)


def _pre_kernel(*refs, rope, ctx):
    it = iter(refs)
    x_ref, mod_ref, g_ref, w_ref, wuq_ref, wukv_ref, gq_ref, gkv_ref = (next(it) for _ in range(8))
    tab_ref = next(it) if rope else None
    (ret_ref, wq_ref, wke_ref, wve_ref, dq_ref, dk_ref, dv_ref,
     mq_ref, mk_ref, mv_ref) = (next(it) for _ in range(10))
    if ctx:
        wke32_ref, wve32_ref, dk32_ref, dv32_ref, ckv32_ref, mkr32_ref = (next(it) for _ in range(6))

    x = x_ref[...]
    mod = mod_ref[0]
    h = _rmsnorm(x, g_ref[...]) * (1.0 + mod[1:2]) + mod[0:1]
    hb = h.astype(BF16)

    def proj(a, b):
        return _dot(hb, w_ref[:, a:b])

    def rot(v, t):
        if not rope:
            return v
        return _rope_rows(v, tab_ref[3 * t], tab_ref[3 * t + 1], tab_ref[3 * t + 2], _ROPE_HALF[t])

    def store(o_ref, val, t=None, o32_ref=None):
        for j in range(val.shape[1] // LANES):
            v = val[:, LANES * j:LANES * (j + 1)]
            if o32_ref is not None:
                o32_ref[:, LANES * j:LANES * (j + 1)] = v
            if t is not None:
                v = rot(v, t)
            o_ref[:, LANES * j:LANES * (j + 1)] = v.astype(o_ref.dtype)

    cq_in = proj(_C_MCQ, _C_MCKV)
    ckv_in = proj(_C_MCKV, _C_MKR)
    mkr = proj(_C_MKR, _C_END)
    ret_ref[...] = proj(_C_RET, _C_WQ)
    cq = _rmsnorm(cq_in, gq_ref[...])
    ckv = _rmsnorm(ckv_in, gkv_ref[...])
    store(wq_ref, proj(_C_WQ, _C_WK), 0)
    store(wke_ref, proj(_C_WK, _C_WV), 0, wke32_ref if ctx else None)
    mq = _dot(cq.astype(BF16), wuq_ref[...]) * ((MLA_NOPE + MLA_ROPE) ** -0.5 * LOG2E)
    kvp = _dot(ckv.astype(BF16), wukv_ref[...])
    store(wve_ref, proj(_C_WV, _C_DQ), None, wve32_ref if ctx else None)
    store(dq_ref, proj(_C_DQ, _C_DK) * (DIFF_D ** -0.5 * LOG2E), 1)
    store(mq_ref, mq, 2)
    store(dk_ref, proj(_C_DK, _C_DV), 1, dk32_ref if ctx else None)
    store(dv_ref, proj(_C_DV, _C_MCQ), None, dv32_ref if ctx else None)
    mkr_rot = rot(mkr, 2)
    for hd in range(MLA_H):
        mk_ref[:, LANES * hd:LANES * (hd + 1)] = (kvp[:, LANES * hd:LANES * (hd + 1)] + mkr_rot).astype(BF16)
    mv_ref[...] = kvp[:, MLA_H * LANES:].astype(BF16)
    if ctx:
        ckv32_ref[...] = ckv
        mkr32_ref[...] = mkr


def _pre(x2d, mod, g_mix, w_in_p, w_uq_p, w_ukv_p, g_q, g_kv, tabs, *, seq_len, ctx, layer):
    t, d = x2d.shape
    tm = TM_DENSE
    assert seq_len % tm == 0 or (tm % seq_len == 0 and mod.shape[0] == 1 and tabs is None)
    tiles_per_seq = max(seq_len // tm, 1)
    rope = tabs is not None
    const = lambda i: (0, 0)
    row = lambda i: (i, 0)
    in_specs = [
        pl.BlockSpec((tm, d), row),
        _mod_spec(mod, tiles_per_seq),
        pl.BlockSpec((1, d), const),
        _resident(w_in_p.shape, layer),
        _resident(w_uq_p.shape, layer),
        _resident(w_ukv_p.shape, layer),
        pl.BlockSpec((1, MLA_QRANK), const),
        pl.BlockSpec((1, MLA_KVRANK), const),
    ]
    args = [x2d, mod, g_mix.reshape(1, d), w_in_p, w_uq_p, w_ukv_p,
            g_q.reshape(1, MLA_QRANK), g_kv.reshape(1, MLA_KVRANK)]
    if rope:
        in_specs.append(pl.BlockSpec((9, tm, LANES), lambda i: (0, i % tiles_per_seq, 0)))
        args.append(tabs)
    widths = [(1024, F32)] + [(256, BF16)] * 6 + [(512, BF16), (512, BF16), (256, BF16)]
    if ctx:
        widths += [(256, F32)] * 4 + [(128, F32)] * 2
    out_specs = [pl.BlockSpec((tm, w), row) for w, _ in widths]
    out_shape = [jax.ShapeDtypeStruct((t, w), dt) for w, dt in widths]
    return pl.pallas_call(
        functools.partial(_pre_kernel, rope=rope, ctx=ctx),
        grid=(t // tm,),
        in_specs=in_specs,
        out_specs=out_specs,
        out_shape=out_shape,
        compiler_params=_cparams(("arbitrary",)),
        name="pre_ctx" if ctx else "pre_lat",
    )(*args)


def _log_sigmoid(z):
    return jnp.minimum(z, 0.0) - jnp.log(1.0 + jnp.exp(-jnp.abs(z)))


def _ret_kernel(r_ref, s0_ref, dl_ref, ds_ref, dh_ref, o_ref, st_ref,
                s_sc, ob_sc, qdec_sc, kdec_sc, cdec_sc, intra_sc, *, n_chunks):
    p = pl.program_id(1)
    c = pl.program_id(2)
    fwd = p == 1
    chunk = RET_CHUNK
    w = BRANCH_W

    @pl.when(c == 0)
    def _init():
        lgl = _log_sigmoid(dl_ref[0])[0:1, :]
        i = lax.broadcasted_iota(jnp.int32, (chunk, w), 0).astype(F32)
        qe = jnp.where(fwd, i + 1.0, chunk - i)
        ke = jnp.where(fwd, chunk - 1.0 - i, i)
        qdec_sc[...] = jnp.exp(lgl * qe)
        kdec_sc[...] = jnp.exp(lgl * ke)
        rr = lax.broadcasted_iota(jnp.int32, (w, w), 0) // HEAD_W
        cc = lax.broadcasted_iota(jnp.int32, (w, w), 1) // HEAD_W
        cdec_sc[...] = jnp.where(rr == cc, jnp.exp(_log_sigmoid(ds_ref[0]) * float(chunk)), 0.0)
        ii = lax.broadcasted_iota(jnp.int32, (chunk, chunk), 0)
        jj = lax.broadcasted_iota(jnp.int32, (chunk, chunk), 1)
        dist = jnp.where(fwd, ii - jj, jj - ii)
        for hd in range(RET_H):
            lgh = _log_sigmoid(dh_ref[0, hd])[0:1, :]
            intra_sc[hd] = jnp.where(dist >= 0, jnp.exp(lgh * jnp.maximum(dist, 0).astype(F32)), 0.0)

    seqs = r_ref.shape[0]
    head_mask = [_lane_group_mask((chunk, w), HEAD_W, hd) for hd in range(RET_H)]
    rr = lax.broadcasted_iota(jnp.int32, (w, w), 0) // HEAD_W
    cc = lax.broadcasted_iota(jnp.int32, (w, w), 1) // HEAD_W
    block_diag = rr == cc
    outs = [None] * seqs

    def scan_step(sq):
        blk = r_ref[sq]
        q = blk[:, 0:w]
        k = blk[:, w:2 * w] * (RET_DK ** -0.5)
        v = blk[:, 2 * w:3 * w]
        qb = q.astype(BF16)
        kb = k.astype(BF16)
        vb = v.astype(BF16)
        s = jnp.where(c == 0, s0_ref[sq, 0], s_sc[sq])
        yield
        o = _dot(qb, s.astype(BF16)) * qdec_sc[...]
        for hd in range(RET_H):
            sc = _dot_nt(jnp.where(head_mask[hd], qb, jnp.zeros_like(qb)), kb) * intra_sc[hd]
            yield
            o = o + jnp.where(head_mask[hd], _dot(sc.astype(BF16), vb), 0.0)
        outs[sq] = o
        yield
        kd_t = jnp.transpose(k * kdec_sc[...]).astype(BF16)
        s_new = s * cdec_sc[...] + jnp.where(block_diag, _dot(kd_t, vb), 0.0)
        s_sc[sq] = s_new
        yield
        for hd in range(RET_H):
            st_ref[sq, 0, hd] = s_new[HEAD_W * hd:HEAD_W * (hd + 1), HEAD_W * hd:HEAD_W * (hd + 1)]

    def finish(sq):
        tot = outs[sq] + ob_sc[sq, c]
        mu = jnp.zeros_like(tot)
        for hd in range(RET_H):
            m1 = jnp.sum(jnp.where(head_mask[hd], tot, 0.0), axis=-1, keepdims=True) * (1.0 / HEAD_W)
            mu = jnp.where(head_mask[hd], m1, mu)
        yield
        xc = tot - mu
        var = jnp.zeros_like(tot)
        for hd in range(RET_H):
            v1 = jnp.sum(jnp.where(head_mask[hd], xc * xc, 0.0), axis=-1, keepdims=True) * (1.0 / HEAD_W)
            var = jnp.where(head_mask[hd], v1, var)
        yield
        g = r_ref[sq][:, 3 * w:4 * w]
        o_ref[sq] = (xc * lax.rsqrt(var + NORM_EPS) * (g * _sigmoid(g))).astype(o_ref.dtype)

    _round_robin([scan_step(sq) for sq in range(seqs)])

    @pl.when(p == 0)
    def _bwd():
        for sq in range(seqs):
            ob_sc[sq, n_chunks - 1 - c] = outs[sq]

    @pl.when(p == 1)
    def _fwd():
        _round_robin([finish(sq) for sq in range(seqs)])


def _retention(ret4, s0_bd, decay, *, batch, seq_len):
    n = seq_len // RET_CHUNK
    w = BRANCH_W
    seqs = RET_SEQS_PER_STEP if batch % RET_SEQS_PER_STEP == 0 else batch
    dlane = jnp.broadcast_to(jnp.repeat(decay, HEAD_W, axis=1)[:, None, :], (2, 8, w))
    dsub = jnp.broadcast_to(jnp.repeat(decay, HEAD_W, axis=1)[:, :, None], (2, w, w))
    dhead = jnp.broadcast_to(decay[:, :, None, None], (2, RET_H, 8, LANES))
    chunk_of = lambda p, c: c * p + (n - 1 - c) * (1 - p)
    o, st = pl.pallas_call(
        functools.partial(_ret_kernel, n_chunks=n),
        grid=(batch // seqs, 2, n),
        in_specs=[
            pl.BlockSpec((seqs, RET_CHUNK, 4 * w), lambda b, p, c: (b, chunk_of(p, c), 0)),
            pl.BlockSpec((seqs, 1, w, w), lambda b, p, c: (b, 1 - p, 0, 0)),
            pl.BlockSpec((1, 8, w), lambda b, p, c: (1 - p, 0, 0)),
            pl.BlockSpec((1, w, w), lambda b, p, c: (1 - p, 0, 0)),
            pl.BlockSpec((1, RET_H, 8, LANES), lambda b, p, c: (1 - p, 0, 0, 0)),
        ],
        out_specs=[
            pl.BlockSpec((seqs, RET_CHUNK, w), lambda b, p, c: (b, c * p, 0)),
            pl.BlockSpec((seqs, 1, RET_H, RET_DK, RET_DK), lambda b, p, c: (b, 1 - p, 0, 0, 0)),
        ],
        out_shape=[
            jax.ShapeDtypeStruct((batch, seq_len, w), BF16),
            jax.ShapeDtypeStruct((batch, 2, RET_H, RET_DK, RET_DK), F32),
        ],
        scratch_shapes=[
            pltpu.VMEM((seqs, w, w), F32),
            pltpu.VMEM((seqs, n, RET_CHUNK, w), F32),
            pltpu.VMEM((RET_CHUNK, w), F32),
            pltpu.VMEM((RET_CHUNK, w), F32),
            pltpu.VMEM((w, w), F32),
            pltpu.VMEM((RET_H, RET_CHUNK, RET_CHUNK), F32),
        ],
        compiler_params=_cparams(("arbitrary", "arbitrary", "arbitrary")),
        name="retention",
    )(ret4.reshape(batch, seq_len, 4 * w), s0_bd, dlane, dsub, dhead)
    return o.reshape(batch * seq_len, w), st


def _col_fold(acc, x, op):
    for j in range(x.shape[1] // LANES):
        acc = op(acc, x[:, LANES * j:LANES * (j + 1)])
    return acc


_ATTN_GROUPING = {"sink": (4, 1), "diff": (4, 2), "mla": (4, 1)}


def _attn_body(*refs, kind, n_chunks, kc, has_ctx, lam_init, static_chunks):
    it = iter(refs)
    q_ref, k_ref, v_ref = next(it), next(it), next(it)
    if has_ctx:
        if kind == "mla":
            cckv_ref, ckr_ref, wukv_ref = next(it), next(it), next(it)
        else:
            kx_ref, vx_ref = next(it), next(it)
    if kind == "sink":
        sink_ref = next(it)
    if kind == "diff":
        lam_ref, gd_ref = next(it), next(it)
    o_ref = next(it)
    nv, n_groups = _ATTN_GROUPING[kind]
    per_group = lambda: [next(it) for _ in range(n_groups)]
    s_all, m_all, l_all = per_group(), per_group(), per_group()
    qst_all = per_group() if kind != "mla" else None
    sx_all = per_group() if has_ctx else None
    if has_ctx and kind == "mla":
        kx_sc, vx_sc = next(it), next(it)

    tq = q_ref.shape[1]
    w = BRANCH_W
    rows = nv * tq
    unroll = ATTN_UNROLL if n_chunks % ATTN_UNROLL == 0 else 1
    q = q_ref[0]
    log2_scale = WIN_HD ** -0.5 * LOG2E if kind == "sink" else None

    if has_ctx and kind == "mla":
        kvc = _dot(cckv_ref[0].astype(BF16), wukv_ref[...])
        kr = ckr_ref[0]
        for hd in range(MLA_H):
            kx_sc[:, LANES * hd:LANES * (hd + 1)] = (kvc[:, LANES * hd:LANES * (hd + 1)] + kr).astype(BF16)
        vx_sc[...] = kvc[:, MLA_H * LANES:].astype(BF16)

    def rows_of(ref, ch):
        if static_chunks:
            return ref[0, ch * kc:(ch + 1) * kc, :]
        return ref[0, pl.ds(pl.multiple_of(ch * kc, kc), kc), :]

    def k_chunk(ch):
        return rows_of(k_ref, ch)

    def v_chunk(ch):
        return rows_of(v_ref, ch)

    def kx_tile():
        return kx_sc[...] if kind == "mla" else kx_ref[0]

    def vx_tile():
        return vx_sc[...] if kind == "mla" else vx_ref[0]

    def chunk_loop(body, carry):
        if static_chunks:
            for ch in range(n_chunks):
                carry = body(ch, carry)
                yield
        else:
            carry = lax.fori_loop(0, n_chunks, body, carry, unroll=unroll)
            yield
        return carry

    def weighted_values(make_weights, s_sc, sx_sc, n_rows):
        if not static_chunks:
            def body(ch, carry):
                return carry + _dot(make_weights(s_sc[ch]), v_chunk(ch))

            acc = lax.fori_loop(0, n_chunks, body, jnp.zeros((n_rows, w), F32), unroll=unroll)
            yield
            if has_ctx:
                acc = acc + _dot(make_weights(sx_sc[...]), vx_tile())
            return acc
        score_of = [functools.partial(lambda ch: s_sc[ch], ch) for ch in range(n_chunks)]
        value_of = [functools.partial(v_chunk, ch) for ch in range(n_chunks)]
        if has_ctx:
            score_of.append(lambda: sx_sc[...])
            value_of.append(vx_tile)
        acc = None
        ahead = make_weights(score_of[0]())
        for i in range(len(score_of)):
            cur = ahead
            if i + 1 < len(score_of):
                ahead = make_weights(score_of[i + 1]())
            term = _dot(cur, value_of[i]())
            acc = term if acc is None else acc + term
            yield
        return acc

    parts = []

    def group_body(grp):
        s_sc, m_sc, l_sc = s_all[grp], m_all[grp], l_all[grp]
        qst_sc = qst_all[grp] if kind != "mla" else None
        sx_sc = sx_all[grp] if has_ctx else None

        def scores(kt):
            if kind == "mla":
                heads = range(nv * grp, nv * (grp + 1))
                return jnp.concatenate(
                    [_dot_nt(q[:, LANES * hd:LANES * (hd + 1)], kt[:, LANES * hd:LANES * (hd + 1)])
                     for hd in heads], axis=0)
            return _dot_nt(qst_sc[...], kt)

        def probs(s):
            mb = m_sc[...]
            logit = (lambda x: x) if log2_scale is None else (lambda x: x * log2_scale)
            cols = [jnp.exp2(logit(s[:, LANES * j:LANES * (j + 1)]) - mb) for j in range(s.shape[1] // LANES)]
            tot = l_sc[...]
            for col in cols:
                tot = tot + col
            l_sc[...] = tot
            return jnp.concatenate(cols, axis=1)

        if kind != "mla":
            for v in range(nv):
                mask = (_lane_group_mask((tq, w), HEAD_W, v) if kind == "sink"
                        else _lane_group_mask((tq, w), DIFF_D, nv * grp + v))
                qst_sc[v * tq:(v + 1) * tq, :] = jnp.where(mask, q, jnp.zeros_like(q))
        yield

        m_sc[...] = jnp.full((rows, LANES), -jnp.inf, F32)

        def score_body(ch, carry):
            s = scores(k_chunk(ch))
            s_sc[ch] = s
            m_sc[...] = _col_fold(m_sc[...], s, jnp.maximum)
            return carry

        yield from chunk_loop(score_body, 0)
        if has_ctx:
            s = scores(kx_tile())
            sx_sc[...] = s
            m_sc[...] = _col_fold(m_sc[...], s, jnp.maximum)
        yield
        m2 = jnp.max(m_sc[...], axis=-1, keepdims=True)
        if log2_scale is not None:
            m2 = m2 * log2_scale
        if kind == "sink":
            sink2 = LOG2E * jnp.concatenate(
                [jnp.broadcast_to(sink_ref[v:v + 1, 0:1], (tq, 1)) for v in range(nv)], axis=0)
            m2 = jnp.maximum(m2, sink2)
        heads = range(nv * grp, nv * (grp + 1))
        m_sc[...] = jnp.broadcast_to(m2, (rows, LANES))
        l_sc[...] = jnp.zeros((rows, LANES), F32)
        yield

        if kind != "diff":
            acc = yield from weighted_values(lambda s: probs(s).astype(BF16), s_sc, sx_sc, rows)
            yield
            l = jnp.sum(l_sc[...], axis=-1, keepdims=True)
            if kind == "sink":
                l = l + jnp.exp2(sink2 - m2)
            pv = acc * (1.0 / l)
            for v, hd in enumerate(heads):
                parts.append(jnp.where(_lane_group_mask((tq, w), HEAD_W, hd), pv[v * tq:(v + 1) * tq], 0.0))
        else:
            def exp_body(ch, carry):
                s_sc[ch] = probs(s_sc[ch])
                return carry

            yield from chunk_loop(exp_body, 0)
            if has_ctx:
                sx_sc[...] = probs(sx_sc[...])
            yield
            dl = lam_ref[...]
            lam = (jnp.exp(jnp.sum(dl[0:1] * dl[1:2], axis=-1, keepdims=True))
                   - jnp.exp(jnp.sum(dl[2:3] * dl[3:4], axis=-1, keepdims=True)) + lam_init)
            l = jnp.sum(l_sc[...], axis=-1, keepdims=True)
            inv_l = 1.0 / l
            for v in range(2):
                first, second = slice(2 * v * tq, (2 * v + 1) * tq), slice((2 * v + 1) * tq, (2 * v + 2) * tq)
                l_sc[second, :] = jnp.broadcast_to(lam * l[first] * inv_l[second], (tq, LANES))
            yield

            def weights(pr):
                cols = []
                for j in range(pr.shape[1] // LANES):
                    x = pr[:, LANES * j:LANES * (j + 1)]
                    cols.append(jnp.concatenate(
                        [x[0:tq] - x[tq:2 * tq] * l_sc[tq:2 * tq, :],
                         x[2 * tq:3 * tq] - x[3 * tq:4 * tq] * l_sc[3 * tq:4 * tq, :]], axis=0))
                return jnp.concatenate(cols, axis=1).astype(BF16)

            acc = yield from weighted_values(weights, s_sc, sx_sc, 2 * tq)
            yield
            for v in range(2):
                head_out = acc[v * tq:(v + 1) * tq] * inv_l[2 * v * tq:(2 * v + 1) * tq]
                parts.append(jnp.where(_lane_group_mask((tq, w), HEAD_W, 2 * grp + v), head_out, 0.0))

    lag = (n_chunks + 2) if static_chunks else 0
    yield from _interleave([group_body(grp) for grp in range(n_groups)], [lag * grp for grp in range(n_groups)])
    out = parts[0]
    for part in parts[1:]:
        out = out + part
    if kind == "diff":
        ms = jnp.zeros_like(out)
        for hd in range(DIFF_H):
            mh = _lane_group_mask((tq, w), HEAD_W, hd)
            m1 = jnp.sum(jnp.where(mh, out * out, 0.0), axis=-1, keepdims=True) * (1.0 / HEAD_W)
            ms = jnp.where(mh, m1, ms)
        out = out * lax.rsqrt(ms + NORM_EPS) * gd_ref[...] * (1.0 - lam_init)
    o_ref[0] = out.astype(o_ref.dtype)


def _attn_multi_kernel(*refs, seqs, batched, **kw):
    n_io = len(batched)
    bodies = []
    for sq in range(seqs):
        view = [r.at[pl.ds(sq, 1)] if flag else r for r, flag in zip(refs[:n_io], batched)]
        view += [r.at[sq] for r in refs[n_io:]]
        bodies.append(_attn_body(*view, **kw))
    _round_robin(bodies)


def _attention(kind, q, k, v, *, ctx_args=(), params=(), lam_init=0.0):
    b, n, wq = q.shape
    nk = k.shape[1]
    tq = min(TQ_ATTN, n)
    kc = min(KC_ATTN, nk)
    n_chunks = nk // kc
    has_ctx = len(ctx_args) > 0
    w = BRANCH_W
    nv, n_groups = _ATTN_GROUPING[kind]
    rows = nv * tq
    static_chunks = n_chunks > 1
    seqs = ATTN_SEQS_PER_STEP if (n == tq and b % ATTN_SEQS_PER_STEP == 0) else 1
    per_b = lambda bb, i: (bb, 0, 0)
    shared = lambda bb, i: (0, 0)
    in_specs = [
        pl.BlockSpec((seqs, tq, wq), lambda bb, i: (bb, i, 0)),
        pl.BlockSpec((seqs, nk, k.shape[2]), per_b),
        pl.BlockSpec((seqs, nk, w), per_b),
    ]
    args = [q, k, v]
    batched = [True, True, True]
    for a in ctx_args:
        if a.ndim == 3:
            in_specs.append(pl.BlockSpec((seqs,) + a.shape[1:], per_b))
        else:
            in_specs.append(pl.BlockSpec(a.shape, shared))
        batched.append(a.ndim == 3)
        args.append(a)
    for a in params:
        in_specs.append(pl.BlockSpec(a.shape, shared))
        batched.append(False)
        args.append(a)
    batched.append(True)
    grouped = [(n_chunks, rows, kc, F32), (rows, LANES, F32), (rows, LANES, F32)]
    if kind != "mla":
        grouped.append((rows, w, BF16))
    single = []
    if has_ctx:
        m_ctx = ctx_args[0].shape[1]
        grouped.append((rows, m_ctx, F32))
        if kind == "mla":
            single = [(m_ctx, MLA_H * LANES, BF16), (m_ctx, w, BF16)]
    scratch = [sh for sh in grouped for _ in range(n_groups)] + single
    scratch_bytes = sum(seqs * math.prod(sh[:-1]) * jnp.dtype(sh[-1]).itemsize for sh in scratch)
    return pl.pallas_call(
        functools.partial(_attn_multi_kernel, seqs=seqs, batched=tuple(batched), kind=kind, n_chunks=n_chunks,
                          kc=kc, has_ctx=has_ctx, lam_init=lam_init, static_chunks=static_chunks),
        grid=(b // seqs, n // tq),
        in_specs=in_specs,
        out_specs=pl.BlockSpec((seqs, tq, w), lambda bb, i: (bb, i, 0)),
        out_shape=jax.ShapeDtypeStruct((b, n, w), BF16),
        scratch_shapes=[pltpu.VMEM((seqs,) + sh[:-1], sh[-1]) for sh in scratch],
        compiler_params=_cparams(("arbitrary", "arbitrary"), vmem=max(VMEM_LIMIT, scratch_bytes + ATTN_VMEM_EXTRA)),
        name="attn_" + kind + ("_lat" if has_ctx else "_ctx"),
    )(*args)


def _win_kernel(q_ref, kp_ref, kc_ref, kn_ref, vp_ref, vc_ref, vn_ref, kx_ref, vx_ref, sink_ref, o_ref):
    n_steps = pl.num_programs(1)
    step = pl.program_id(1)
    tq = WINDOW
    nblk = q_ref.shape[1] // tq
    w = BRANCH_W
    nh = 4
    rows = nh * tq
    scale = WIN_HD ** -0.5
    head_mask = [_lane_group_mask((tq, w), HEAD_W, hd) for hd in range(nh)]
    ii = lax.broadcasted_iota(jnp.int32, (rows, tq), 0) % tq
    jj = lax.broadcasted_iota(jnp.int32, (rows, tq), 1)
    sink_col = jnp.concatenate([jnp.broadcast_to(sink_ref[hd:hd + 1, 0:1], (tq, 1)) for hd in range(nh)], axis=0)
    kx = kx_ref[0]
    vx = vx_ref[0]

    def rows_of(ref, blk):
        return ref[0, blk * tq:(blk + 1) * tq, :]

    def block(blk):
        q = rows_of(q_ref, blk)
        qst = jnp.concatenate([jnp.where(head_mask[hd], q, jnp.zeros_like(q)) for hd in range(nh)], axis=0)
        kp, vp = (kp_ref[0], vp_ref[0]) if blk == 0 else (rows_of(kc_ref, blk - 1), rows_of(vc_ref, blk - 1))
        kn, vn = (kn_ref[0], vn_ref[0]) if blk == nblk - 1 else (rows_of(kc_ref, blk + 1), rows_of(vc_ref, blk + 1))
        mask_prev = jj >= ii
        mask_next = jj <= ii
        if blk == 0:
            mask_prev = mask_prev & (step > 0)
        if blk == nblk - 1:
            mask_next = mask_next & (step < n_steps - 1)
        yield
        sp = jnp.where(mask_prev, _dot_nt(qst, kp) * scale, NEG_INF)
        sc = _dot_nt(qst, rows_of(kc_ref, blk)) * scale
        sn = jnp.where(mask_next, _dot_nt(qst, kn) * scale, NEG_INF)
        sx = _dot_nt(qst, kx) * scale
        yield
        m = jnp.maximum(jnp.maximum(jnp.max(sp, axis=-1, keepdims=True), jnp.max(sc, axis=-1, keepdims=True)),
                        jnp.maximum(jnp.max(sn, axis=-1, keepdims=True), jnp.max(sx, axis=-1, keepdims=True)))
        m = jnp.maximum(m, sink_col)
        yield
        pp, pc, pn, px = (jnp.exp(s - m) for s in (sp, sc, sn, sx))
        l = (jnp.sum(pp, axis=-1, keepdims=True) + jnp.sum(pc, axis=-1, keepdims=True)
             + jnp.sum(pn, axis=-1, keepdims=True) + jnp.sum(px, axis=-1, keepdims=True) + jnp.exp(sink_col - m))
        yield
        pv = (_dot(pp.astype(BF16), vp) + _dot(pc.astype(BF16), rows_of(vc_ref, blk))
              + _dot(pn.astype(BF16), vn) + _dot(px.astype(BF16), vx)) * (1.0 / l)
        yield
        out = jnp.zeros((tq, w), F32)
        for hd in range(nh):
            out = out + jnp.where(head_mask[hd], pv[hd * tq:(hd + 1) * tq], 0.0)
        o_ref[0, blk * tq:(blk + 1) * tq, :] = out.astype(o_ref.dtype)

    _round_robin([block(blk) for blk in range(nblk)])


def _window_attention(q, k, v, kx, vx, sink_tile):
    b, n, w = q.shape
    tq = WINDOW
    span = WIN_BLOCKS_PER_STEP * tq
    nb = n // tq
    m_ctx = kx.shape[1]
    edge = lambda f: pl.BlockSpec((1, tq, w), f)
    prev = lambda bb, i: (bb, jnp.maximum(i * WIN_BLOCKS_PER_STEP - 1, 0), 0)
    nxt = lambda bb, i: (bb, jnp.minimum((i + 1) * WIN_BLOCKS_PER_STEP, nb - 1), 0)
    mid = pl.BlockSpec((1, span, w), lambda bb, i: (bb, i, 0))
    per_b = lambda bb, i: (bb, 0, 0)
    return pl.pallas_call(
        _win_kernel,
        grid=(b, n // span),
        in_specs=[mid, edge(prev), mid, edge(nxt), edge(prev), mid, edge(nxt),
                  pl.BlockSpec((1, m_ctx, w), per_b), pl.BlockSpec((1, m_ctx, w), per_b),
                  pl.BlockSpec(sink_tile.shape, lambda bb, i: (0, 0))],
        out_specs=mid,
        out_shape=jax.ShapeDtypeStruct((b, n, w), BF16),
        compiler_params=_cparams(("arbitrary", "arbitrary")),
        name="attn_window",
    )(q, k, k, k, v, v, v, kx, vx, sink_tile)


def _post_kernel(x_ref, mod_ref, g_ref, o0_ref, o1_ref, o2_ref, o3_ref,
                 wg_ref, bg_ref, wb_ref, wo_ref, out_ref, mix_sc):
    d = D_MODEL
    x = x_ref[...]
    mod = mod_ref[0]
    h = _rmsnorm(x, g_ref[...]) * (1.0 + mod[1:2]) + mod[0:1]
    hb = h.astype(BF16)
    cw = 256
    for j in range(d // cw):
        mixed = None
        for nbr, o_ref in enumerate((o0_ref, o1_ref, o2_ref, o3_ref)):
            lo = nbr * d + j * cw
            gate = _sigmoid(_dot(hb, wg_ref[:, lo:lo + cw]) + bg_ref[:, lo:lo + cw])
            term = gate * _dot(o_ref[...], wb_ref[nbr, :, j * cw:(j + 1) * cw])
            mixed = term if mixed is None else mixed + term
        mix_sc[:, j * cw:(j + 1) * cw] = mixed.astype(BF16)
    out_ref[...] = x + mod[2:3] * _dot(mix_sc[...], wo_ref[...])


def _post(x2d, mod, g_mix, outs, w_gate, b_gate, w_branch, w_out, *, seq_len, layer):
    t, d = x2d.shape
    tm = TM_POST
    assert seq_len % tm == 0 or (tm % seq_len == 0 and mod.shape[0] == 1)
    tiles_per_seq = max(seq_len // tm, 1)
    const2 = lambda i: (0, 0)
    row = lambda i: (i, 0)
    return pl.pallas_call(
        _post_kernel,
        grid=(t // tm,),
        in_specs=[
            pl.BlockSpec((tm, d), row),
            _mod_spec(mod, tiles_per_seq),
            pl.BlockSpec((1, d), const2),
        ] + [pl.BlockSpec((tm, BRANCH_W), row)] * 4 + [
            _resident(w_gate.shape, layer),
            pl.BlockSpec((1, 4 * d), const2),
            _resident(w_branch.shape, layer),
            _resident(w_out.shape, layer),
        ],
        out_specs=pl.BlockSpec((tm, d), row),
        out_shape=jax.ShapeDtypeStruct((t, d), F32),
        scratch_shapes=[pltpu.VMEM((tm, d), BF16)],
        compiler_params=_cparams(("arbitrary",)),
        name="post",
    )(x2d, mod, g_mix.reshape(1, d), *outs, w_gate, b_gate.reshape(1, 4 * d), w_branch, w_out)


def _ffn_kernel(*refs, tiles_per_seq, final):
    halo = tiles_per_seq > 1
    it = iter(refs)
    x_ref = next(it)
    if halo:
        xp_ref, xn_ref = next(it), next(it)
    mod_ref, g_ref, wup_ref, bup_ref, wcv_ref, bcv_ref, wdn_ref = (next(it) for _ in range(7))
    gf_ref = next(it) if final else None
    out_ref = next(it)
    h_sc = next(it)

    tm = x_ref.shape[0]
    hr = CONV_HALO
    n_ff = D_FF // FF_CHUNK
    mod = mod_ref[0]
    g = g_ref[...]

    def norm_mod(rows):
        return (_rmsnorm(rows, g) * (1.0 + mod[4:5]) + mod[3:4]).astype(BF16)

    x = x_ref[...]
    h_sc[hr:hr + tm, :] = norm_mod(x)
    if halo:
        h_sc[0:hr, :] = norm_mod(xp_ref[...])
        h_sc[hr + tm:, :] = norm_mod(xn_ref[...])
        i = pl.program_id(0)
        keep_lo = jnp.where((i % tiles_per_seq) == 0, 0.0, 1.0)
        keep_hi = jnp.where((i % tiles_per_seq) == tiles_per_seq - 1, 0.0, 1.0)
        rid = lax.broadcasted_iota(jnp.int32, (8, 1), 0)
        edge_lo = jnp.where(rid == 7, keep_lo, 1.0)
        edge_hi = jnp.where(rid == 0, keep_hi, 1.0)

    def up(ch):
        lo = ch * FF_CHUNK
        if halo:
            u = _dot(h_sc[...], wup_ref[:, lo:lo + FF_CHUNK]) + bup_ref[:, lo:lo + FF_CHUNK]
            return jnp.concatenate([u[:hr - 8], u[hr - 8:hr] * edge_lo, u[hr:hr + tm],
                                    u[hr + tm:hr + tm + 8] * edge_hi, u[hr + tm + 8:]], axis=0)
        pad = jnp.zeros((hr, FF_CHUNK), F32)
        u = _dot(h_sc[hr:hr + tm, :], wup_ref[:, lo:lo + FF_CHUNK]) + bup_ref[:, lo:lo + FF_CHUNK]
        return jnp.concatenate([pad, u, pad], axis=0)

    def conv(ch, u):
        lo = ch * FF_CHUNK
        wc = wcv_ref[:, lo:lo + FF_CHUNK]
        before = pltpu.roll(u, 1, axis=0)[hr:hr + tm]
        after = pltpu.roll(u, u.shape[0] - 1, axis=0)[hr:hr + tm]
        return (bcv_ref[:, lo:lo + FF_CHUNK] + wc[0:1] * before + wc[1:2] * u[hr:hr + tm] + wc[2:3] * after)

    u_val, u_gate = up(0), up(n_ff)
    act = None
    for c in range(n_ff + 1):
        if c + 1 < n_ff:
            u_val_next, u_gate_next = up(c + 1), up(c + 1 + n_ff)
        if c >= 1:
            term = _dot(act, wdn_ref[(c - 1) * FF_CHUNK:c * FF_CHUNK, :])
            acc = term if c == 1 else acc + term
        if c < n_ff:
            a = conv(c, u_val)
            gg = conv(c + n_ff, u_gate)
            act = (gg * _sigmoid(gg) * a).astype(BF16)
            u_val, u_gate = u_val_next, u_gate_next
    y = x + mod[5:6] * acc
    if final:
        y = _rmsnorm(y, gf_ref[...])
    out_ref[...] = y


def _ffn(x2d, mod, g_ffn, w_up, b_up, w_conv, b_conv, w_down, g_final, *, seq_len, layer):
    t, d = x2d.shape
    tm = min(TM_FFN, seq_len)
    hr = CONV_HALO
    tiles_per_seq = seq_len // tm
    halo = tiles_per_seq > 1
    final = g_final is not None
    hb = tm // hr
    n_hblocks = t // hr
    const2 = lambda i: (0, 0)
    row = lambda i: (i, 0)
    in_specs = [pl.BlockSpec((tm, d), row)]
    args = [x2d]
    if halo:
        in_specs += [pl.BlockSpec((hr, d), lambda i: (jnp.maximum(i * hb - 1, 0), 0)),
                     pl.BlockSpec((hr, d), lambda i: (jnp.minimum((i + 1) * hb, n_hblocks - 1), 0))]
        args += [x2d, x2d]
    in_specs += [
        _mod_spec(mod, tiles_per_seq),
        pl.BlockSpec((1, d), const2),
        _resident(w_up.shape, layer),
        pl.BlockSpec(b_up.shape, const2),
        pl.BlockSpec(w_conv.shape, const2),
        pl.BlockSpec(b_conv.shape, const2),
        _resident(w_down.shape, layer),
    ]
    args += [mod, g_ffn.reshape(1, d), w_up, b_up, w_conv, b_conv, w_down]
    if final:
        in_specs.append(pl.BlockSpec((1, d), const2))
        args.append(g_final.reshape(1, d))
    return pl.pallas_call(
        functools.partial(_ffn_kernel, tiles_per_seq=tiles_per_seq, final=final),
        grid=(t // tm,),
        in_specs=in_specs,
        out_specs=pl.BlockSpec((tm, d), row),
        out_shape=jax.ShapeDtypeStruct((t, d), F32),
        scratch_shapes=[
            pltpu.VMEM((tm + 2 * hr, d), BF16),
        ],
        compiler_params=_cparams(("arbitrary",)),
        name=("ffn_final" if final else "ffn") + ("_lat" if halo else "_ctx"),
    )(*args)


def _rope_tables(n_tok):
    rows = n_tok // GRID_W

    def axis(n_pos, half):
        freqs = ROPE_BASE ** (-jnp.arange(half, dtype=F32) / half)
        ang = jnp.arange(n_pos).astype(F32)[:, None] * freqs[None, :]
        return jnp.cos(ang), jnp.sin(ang)

    def unit(width):
        half = width // 4
        cr, sr = (jnp.repeat(t, GRID_W, axis=0) for t in axis(rows, half))
        cc, sc = (jnp.tile(t, (rows, 1)) for t in axis(GRID_W, half))
        z = jnp.zeros_like(sr)
        return (jnp.concatenate([cr, cr, cc, cc], axis=1),
                jnp.concatenate([-sr, z, -sc, z], axis=1),
                jnp.concatenate([z, sr, z, sc], axis=1))

    a = [jnp.tile(t, (1, 2)) for t in unit(64)]
    b = [jnp.tile(t, (1, 4)) for t in unit(32)]
    cu = unit(32)
    ones = jnp.ones((n_tok, 64), F32)
    z64 = jnp.zeros((n_tok, 64), F32)
    z32 = jnp.zeros((n_tok, 32), F32)
    c = [jnp.concatenate([ones, cu[0], ones[:, :32]], axis=1),
         jnp.concatenate([z64, cu[1], z32], axis=1),
         jnp.concatenate([z64, cu[2], z32], axis=1)]
    return jnp.stack(a + b + c, axis=0)


def _dup_groups(w):
    g0, g1 = w[..., :HEAD_W], w[..., HEAD_W:]
    return jnp.concatenate([g0, g0, g1, g1], axis=-1)


def _prep_weights(p):
    depth = p["w_in"].shape[0]
    cuts = np.cumsum((0,) + IN_SPLITS)
    parts = [p["w_in"][:, :, cuts[i]:cuts[i + 1]] for i in range(len(IN_SPLITS))]
    rq, rk, rv, rg, wq, wk, wv, dq, dk, dv, mcq, mckv, mkr = parts
    mkr128 = jnp.pad(mkr, ((0, 0), (0, 0), (MLA_NOPE, LANES - MLA_NOPE - MLA_ROPE)))
    w_in_p = jnp.concatenate([rq, rk, rv, rg, wq, _dup_groups(wk), _dup_groups(wv), dq, dk, dv,
                              mcq, mckv, mkr128], axis=2).astype(BF16)
    w_uq = p["w_mla_uq"].reshape(depth, MLA_QRANK, MLA_H, MLA_NOPE + MLA_ROPE)
    w_uq_p = jnp.pad(w_uq, ((0, 0), (0, 0), (0, 0), (0, LANES - MLA_NOPE - MLA_ROPE)))
    w_ukv = p["w_mla_ukv"].reshape(depth, MLA_KVRANK, MLA_H, MLA_NOPE + MLA_V)
    w_uk = jnp.pad(w_ukv[..., :MLA_NOPE], ((0, 0), (0, 0), (0, 0), (0, LANES - MLA_NOPE)))
    w_uv = w_ukv[..., MLA_NOPE:].reshape(depth, MLA_KVRANK, MLA_H * MLA_V)
    sink = jnp.broadcast_to(p["win_sink"].astype(F32)[:, :, None], (depth, 4, LANES))
    return dict(
        w_in_p=w_in_p,
        w_uq_p=w_uq_p.reshape(depth, MLA_QRANK, MLA_H * LANES).astype(BF16),
        w_ukv_p=jnp.concatenate([w_uk.reshape(depth, MLA_KVRANK, MLA_H * LANES), w_uv], axis=2).astype(BF16),
        w_gate=p["w_gate"].astype(BF16),
        w_branch=p["w_branch"].astype(BF16),
        w_out=p["w_out"].astype(BF16),
        w_up=p["w_up"].astype(BF16),
        w_down=p["w_down"].astype(BF16),
        sink_tile8=jnp.pad(sink, ((0, 0), (0, 4), (0, 0))),
        g_diff4=jnp.tile(p["g_diff"], (1, DIFF_H)).reshape(depth, 1, BRANCH_W),
    )


def _block_diag_state(s):
    b = s.shape[0]
    eye = jnp.eye(RET_H, dtype=s.dtype)
    bd = s[:, :, :, :, None, :] * eye[None, None, :, None, :, None]
    return bd.reshape(b, 2, RET_H * RET_DK, RET_H * RET_DK)


def _mod_rows(mods_l, start, count):
    m = mods_l[start:start + count].reshape(count, 6, D_MODEL)
    return jnp.pad(m, ((0, 0), (0, 2), (0, 0)))


def kernel(x_prompt, x_sample, state_ret, cache_win_k, cache_win_v, cache_diff_k, cache_diff_v, cache_mla_ckv, cache_mla_krope, c, c_ctx, w_mod, b_mod, g_mix, w_in, ret_decay, win_sink, diff_lambda, g_diff, g_mla_q, w_mla_uq, g_mla_kv, w_mla_ukv, w_branch, w_gate, b_gate, w_out, g_ffn, w_up, b_up, w_conv, b_conv, w_down, g_final):
    d = D_MODEL
    bc, lc, _ = x_prompt.shape
    bl, ll, _ = x_sample.shape
    m_ctx = cache_win_k.shape[2]

    cond_rows = jnp.zeros((8, d), F32).at[0].set(c_ctx).at[1:1 + bl].set(c)
    mods = _modulation(cond_rows, w_mod, b_mod)
    tabs = _rope_tables(ll)

    xp = x_prompt.reshape(bc * lc, d)
    xs = x_sample.reshape(bl * ll, d)
    produced = [[] for _ in range(7)]
    p = _prep_weights(dict(w_in=w_in, w_mla_uq=w_mla_uq, w_mla_ukv=w_mla_ukv, w_gate=w_gate, w_branch=w_branch,
                           w_out=w_out, w_up=w_up, w_down=w_down, win_sink=win_sink, g_diff=g_diff))
    zero_state = jnp.zeros((bc, 2, BRANCH_W, BRANCH_W), F32)
    for li in range(DEPTH):
        sink_tile8, g_diff4 = p["sink_tile8"][li], p["g_diff4"][li]
        ffn_small = (b_up[li].reshape(1, -1), w_conv[li], b_conv[li].reshape(1, -1))
        lam_init = 0.8 - 0.6 * math.exp(-0.3 * li)
        mod_c = _mod_rows(mods[li], 0, 1)
        mod_l = _mod_rows(mods[li], 1, bl)
        final_g = g_final if li == DEPTH - 1 else None

        (ret4, wq, wke, wve, dq, dk, dv, mq, mk, mv,
         wke32, wve32, dk32, dv32, ckv32, mkr32) = _pre(
            xp, mod_c, g_mix[li], p["w_in_p"], p["w_uq_p"], p["w_ukv_p"], g_mla_q[li], g_mla_kv[li], None,
            seq_len=lc, ctx=True, layer=li)
        o_ret, st = _retention(ret4, zero_state, ret_decay[li], batch=bc, seq_len=lc)
        r3 = lambda a: a.reshape(bc, lc, a.shape[-1])
        o_win = _attention("sink", r3(wq), r3(wke), r3(wve), params=(sink_tile8,))
        o_diff = _attention("diff", r3(dq), r3(dk), r3(dv), params=(diff_lambda[li], g_diff4), lam_init=lam_init)
        o_mla = _attention("mla", r3(mq), r3(mk), r3(mv))
        f2 = lambda a: a.reshape(bc * lc, BRANCH_W)
        xp = _post(xp, mod_c, g_mix[li], (o_ret, f2(o_win), f2(o_diff), f2(o_mla)),
                   p["w_gate"], b_gate[li], p["w_branch"], p["w_out"], seq_len=lc, layer=li)
        xp = _ffn(xp, mod_c, g_ffn[li], p["w_up"], *ffn_small, p["w_down"], final_g, seq_len=lc, layer=li)
        undup = lambda a: a.reshape(bc, lc, 2, 2, HEAD_W)[:, :, :, 0, :]
        produced[0].append(st)
        produced[1].append(undup(wke32))
        produced[2].append(undup(wve32))
        produced[3].append(dk32.reshape(bc, lc, DIFF_H, 2 * DIFF_D))
        produced[4].append(dv32.reshape(bc, lc, DIFF_H, 2 * DIFF_D))
        produced[5].append(ckv32.reshape(bc, lc, MLA_KVRANK))
        produced[6].append(mkr32.reshape(bc, lc, LANES)[:, :, MLA_NOPE:MLA_NOPE + MLA_ROPE])

        (ret4, wq, wke, wve, dq, dk, dv, mq, mk, mv) = _pre(
            xs, mod_l, g_mix[li], p["w_in_p"], p["w_uq_p"], p["w_ukv_p"], g_mla_q[li], g_mla_kv[li], tabs,
            seq_len=ll, ctx=False, layer=li)
        o_ret, _ = _retention(ret4, _block_diag_state(state_ret[:, li]), ret_decay[li], batch=bl, seq_len=ll)
        r3 = lambda a: a.reshape(bl, ll, a.shape[-1])
        kx_win = _dup_groups(cache_win_k[:, li].reshape(bl, m_ctx, 2 * HEAD_W)).astype(BF16)
        vx_win = _dup_groups(cache_win_v[:, li].reshape(bl, m_ctx, 2 * HEAD_W)).astype(BF16)
        o_win = _window_attention(r3(wq), r3(wke), r3(wve), kx_win, vx_win, sink_tile8)
        kx_diff = cache_diff_k[:, li].reshape(bl, m_ctx, BRANCH_W).astype(BF16)
        vx_diff = cache_diff_v[:, li].reshape(bl, m_ctx, BRANCH_W).astype(BF16)
        o_diff = _attention("diff", r3(dq), r3(dk), r3(dv), ctx_args=(kx_diff, vx_diff),
                            params=(diff_lambda[li], g_diff4), lam_init=lam_init)
        kr128 = jnp.pad(cache_mla_krope[:, li], ((0, 0), (0, 0), (MLA_NOPE, LANES - MLA_NOPE - MLA_ROPE)))
        o_mla = _attention("mla", r3(mq), r3(mk), r3(mv), ctx_args=(cache_mla_ckv[:, li], kr128, p["w_ukv_p"][li]))
        f2 = lambda a: a.reshape(bl * ll, BRANCH_W)
        xs = _post(xs, mod_l, g_mix[li], (o_ret, f2(o_win), f2(o_diff), f2(o_mla)),
                   p["w_gate"], b_gate[li], p["w_branch"], p["w_out"], seq_len=ll, layer=li)
        xs = _ffn(xs, mod_l, g_ffn[li], p["w_up"], *ffn_small, p["w_down"], final_g, seq_len=ll, layer=li)

    y_prompt = xp.reshape(bc, lc, d)
    y_sample = xs.reshape(bl, ll, d)
    stack = lambda lst: jnp.stack(lst, axis=1)
    return (y_prompt, y_sample, stack(produced[0]),
            stack(produced[1]), stack(produced[2]), stack(produced[3]), stack(produced[4]),
            stack(produced[5]), stack(produced[6]))
```

```python
import functools
import math

import numpy as np
import jax
import jax.numpy as jnp
from jax import lax
from jax.experimental import pallas as pl
from jax.experimental.pallas import tpu as pltpu

F32 = jnp.float32
BF16 = jnp.bfloat16

D_MODEL = 1024
DEPTH = 2
GRID_W = 64
ROPE_BASE = 10000.0
NORM_EPS = 1e-6
NEG_INF = -1e30
LOG2E = 1.4426950408889634
RET_H = 4
RET_DK = 64
RET_CHUNK = 128
WIN_HD = 64
WINDOW = 128
DIFF_H = 4
DIFF_D = 32
MLA_H = 4
MLA_NOPE = 64
MLA_ROPE = 32
MLA_V = 64
MLA_QRANK = 256
MLA_KVRANK = 128
D_FF = 2816
IN_SPLITS = (256, 256, 256, 256, 256, 128, 128, 256, 256, 256, 256, 128, 32)

LANES = 128
BF16_ROWS = 16
BRANCH_W = 256
HEAD_W = 64

_C_RET, _C_WQ, _C_WK, _C_WV, _C_DQ, _C_DK, _C_DV, _C_MCQ, _C_MCKV, _C_MKR, _C_END = (
    0, 1024, 1280, 1536, 1792, 2048, 2304, 2560, 2816, 2944, 3072)

TM_DENSE = 1024
TM_POST = 1024
TM_FFN = 256
FF_CHUNK = 256
TQ_ATTN = 256
KC_ATTN = 512
CONV_HALO = BF16_ROWS
RET_SEQS_PER_STEP = 8
WIN_BLOCKS_PER_STEP = 8
ATTN_SEQS_PER_STEP = 8
ATTN_UNROLL = 4
VMEM_LIMIT = 48 * 1024 * 1024
ATTN_VMEM_EXTRA = 16 * 1024 * 1024


def _cparams(sem, vmem=VMEM_LIMIT, fuse=None):
    return pltpu.CompilerParams(dimension_semantics=sem, vmem_limit_bytes=vmem, allow_input_fusion=fuse)


def _dot(a, b):
    return jnp.dot(a, b, preferred_element_type=F32)


def _dot_nt(a, b):
    return lax.dot_general(a, b, (((1,), (1,)), ((), ())), preferred_element_type=F32)


def _sigmoid(x):
    return 1.0 / (1.0 + jnp.exp(-x))


def _rmsnorm(x, gain):
    ms = jnp.mean(x * x, axis=-1, keepdims=True)
    return x * lax.rsqrt(ms + NORM_EPS) * gain


def _interleave(bodies, delays):
    active = list(zip(bodies, delays))
    rnd = 0
    while active:
        active = [(body, d) for body, d in active if d > rnd or next(body, "done") != "done"]
        rnd += 1
        yield


def _round_robin(bodies):
    bodies = list(bodies)
    for _ in _interleave(bodies, [0] * len(bodies)):
        pass


def _resident(stacked_shape, layer):
    zeros = (0,) * (len(stacked_shape) - 1)
    return pl.BlockSpec((None,) + tuple(stacked_shape[1:]), lambda i: (layer,) + zeros,
                        pipeline_mode=pl.Buffered(1))


def _mod_spec(mod, tiles_per_seq):
    blk = (1,) + mod.shape[1:]
    if mod.shape[0] == 1:
        return pl.BlockSpec(blk, lambda i: (0, 0, 0))
    return pl.BlockSpec(blk, lambda i: (i // tiles_per_seq, 0, 0))


def _lane_group_mask(shape, group_width, group):
    lane = lax.broadcasted_iota(jnp.int32, shape, len(shape) - 1)
    return (lane // group_width) == group


def _rope_rows(x, cos, sin_up, sin_dn, half):
    width = x.shape[-1]
    up = pltpu.roll(x, width - half, axis=1)
    dn = pltpu.roll(x, half, axis=1)
    return x * cos + up * sin_up + dn * sin_dn


def _mod_kernel(c_ref, w_ref, b_ref, o_ref):
    c = c_ref[...]
    s = c * _sigmoid(c)
    o_ref[0] = _dot(s.astype(BF16), w_ref[0].astype(BF16)) + b_ref[0]


def _modulation(cond_rows, w_mod, b_mod):
    depth, d, n = w_mod.shape
    tn = 1024
    return pl.pallas_call(
        _mod_kernel,
        grid=(depth, n // tn),
        in_specs=[
            pl.BlockSpec((8, d), lambda l, j: (0, 0)),
            pl.BlockSpec((1, d, tn), lambda l, j: (l, 0, j)),
            pl.BlockSpec((1, 1, tn), lambda l, j: (l, 0, j)),
        ],
        out_specs=pl.BlockSpec((1, 8, tn), lambda l, j: (l, 0, j)),
        out_shape=jax.ShapeDtypeStruct((depth, 8, n), F32),
        compiler_params=_cparams(("arbitrary", "arbitrary")),
        name="modulation",
    )(cond_rows, w_mod, b_mod.reshape(depth, 1, n))


_ROPE_HALF = (16, 8, 8)


def _pre_kernel(*refs, rope, ctx):
    it = iter(refs)
    x_ref, mod_ref, g_ref, w_ref, wuq_ref, wukv_ref, gq_ref, gkv_ref = (next(it) for _ in range(8))
    tab_ref = next(it) if rope else None
    (ret_ref, wq_ref, wke_ref, wve_ref, dq_ref, dk_ref, dv_ref,
     mq_ref, mk_ref, mv_ref) = (next(it) for _ in range(10))
    if ctx:
        wke32_ref, wve32_ref, dk32_ref, dv32_ref, ckv32_ref, mkr32_ref = (next(it) for _ in range(6))

    x = x_ref[...]
    mod = mod_ref[0]
    h = _rmsnorm(x, g_ref[...]) * (1.0 + mod[1:2]) + mod[0:1]
    hb = h.astype(BF16)

    def proj(a, b):
        return _dot(hb, w_ref[:, a:b])

    def rot(v, t):
        if not rope:
            return v
        return _rope_rows(v, tab_ref[3 * t], tab_ref[3 * t + 1], tab_ref[3 * t + 2], _ROPE_HALF[t])

    def store(o_ref, val, t=None, o32_ref=None):
        for j in range(val.shape[1] // LANES):
            v = val[:, LANES * j:LANES * (j + 1)]
            if o32_ref is not None:
                o32_ref[:, LANES * j:LANES * (j + 1)] = v
            if t is not None:
                v = rot(v, t)
            o_ref[:, LANES * j:LANES * (j + 1)] = v.astype(o_ref.dtype)

    cq_in = proj(_C_MCQ, _C_MCKV)
    ckv_in = proj(_C_MCKV, _C_MKR)
    mkr = proj(_C_MKR, _C_END)
    ret_ref[...] = proj(_C_RET, _C_WQ)
    cq = _rmsnorm(cq_in, gq_ref[...])
    ckv = _rmsnorm(ckv_in, gkv_ref[...])
    store(wq_ref, proj(_C_WQ, _C_WK), 0)
    store(wke_ref, proj(_C_WK, _C_WV), 0, wke32_ref if ctx else None)
    mq = _dot(cq.astype(BF16), wuq_ref[...]) * ((MLA_NOPE + MLA_ROPE) ** -0.5 * LOG2E)
    kvp = _dot(ckv.astype(BF16), wukv_ref[...])
    store(wve_ref, proj(_C_WV, _C_DQ), None, wve32_ref if ctx else None)
    store(dq_ref, proj(_C_DQ, _C_DK) * (DIFF_D ** -0.5 * LOG2E), 1)
    store(mq_ref, mq, 2)
    store(dk_ref, proj(_C_DK, _C_DV), 1, dk32_ref if ctx else None)
    store(dv_ref, proj(_C_DV, _C_MCQ), None, dv32_ref if ctx else None)
    mkr_rot = rot(mkr, 2)
    for hd in range(MLA_H):
        mk_ref[:, LANES * hd:LANES * (hd + 1)] = (kvp[:, LANES * hd:LANES * (hd + 1)] + mkr_rot).astype(BF16)
    mv_ref[...] = kvp[:, MLA_H * LANES:].astype(BF16)
    if ctx:
        ckv32_ref[...] = ckv
        mkr32_ref[...] = mkr


def _pre(x2d, mod, g_mix, w_in_p, w_uq_p, w_ukv_p, g_q, g_kv, tabs, *, seq_len, ctx, layer):
    t, d = x2d.shape
    tm = TM_DENSE
    assert seq_len % tm == 0 or (tm % seq_len == 0 and mod.shape[0] == 1 and tabs is None)
    tiles_per_seq = max(seq_len // tm, 1)
    rope = tabs is not None
    const = lambda i: (0, 0)
    row = lambda i: (i, 0)
    in_specs = [
        pl.BlockSpec((tm, d), row),
        _mod_spec(mod, tiles_per_seq),
        pl.BlockSpec((1, d), const),
        _resident(w_in_p.shape, layer),
        _resident(w_uq_p.shape, layer),
        _resident(w_ukv_p.shape, layer),
        pl.BlockSpec((1, MLA_QRANK), const),
        pl.BlockSpec((1, MLA_KVRANK), const),
    ]
    args = [x2d, mod, g_mix.reshape(1, d), w_in_p, w_uq_p, w_ukv_p,
            g_q.reshape(1, MLA_QRANK), g_kv.reshape(1, MLA_KVRANK)]
    if rope:
        in_specs.append(pl.BlockSpec((9, tm, LANES), lambda i: (0, i % tiles_per_seq, 0)))
        args.append(tabs)
    widths = [(1024, F32)] + [(256, BF16)] * 6 + [(512, BF16), (512, BF16), (256, BF16)]
    if ctx:
        widths += [(256, F32)] * 4 + [(128, F32)] * 2
    out_specs = [pl.BlockSpec((tm, w), row) for w, _ in widths]
    out_shape = [jax.ShapeDtypeStruct((t, w), dt) for w, dt in widths]
    return pl.pallas_call(
        functools.partial(_pre_kernel, rope=rope, ctx=ctx),
        grid=(t // tm,),
        in_specs=in_specs,
        out_specs=out_specs,
        out_shape=out_shape,
        compiler_params=_cparams(("arbitrary",)),
        name="pre_ctx" if ctx else "pre_lat",
    )(*args)


def _log_sigmoid(z):
    return jnp.minimum(z, 0.0) - jnp.log(1.0 + jnp.exp(-jnp.abs(z)))


def _ret_kernel(r_ref, s0_ref, dl_ref, ds_ref, dh_ref, o_ref, st_ref,
                s_sc, ob_sc, qdec_sc, kdec_sc, cdec_sc, intra_sc, *, n_chunks):
    p = pl.program_id(1)
    c = pl.program_id(2)
    fwd = p == 1
    chunk = RET_CHUNK
    w = BRANCH_W

    @pl.when(c == 0)
    def _init():
        lgl = _log_sigmoid(dl_ref[0])[0:1, :]
        i = lax.broadcasted_iota(jnp.int32, (chunk, w), 0).astype(F32)
        qe = jnp.where(fwd, i + 1.0, chunk - i)
        ke = jnp.where(fwd, chunk - 1.0 - i, i)
        qdec_sc[...] = jnp.exp(lgl * qe)
        kdec_sc[...] = jnp.exp(lgl * ke)
        rr = lax.broadcasted_iota(jnp.int32, (w, w), 0) // HEAD_W
        cc = lax.broadcasted_iota(jnp.int32, (w, w), 1) // HEAD_W
        cdec_sc[...] = jnp.where(rr == cc, jnp.exp(_log_sigmoid(ds_ref[0]) * float(chunk)), 0.0)
        ii = lax.broadcasted_iota(jnp.int32, (chunk, chunk), 0)
        jj = lax.broadcasted_iota(jnp.int32, (chunk, chunk), 1)
        dist = jnp.where(fwd, ii - jj, jj - ii)
        for hd in range(RET_H):
            lgh = _log_sigmoid(dh_ref[0, hd])[0:1, :]
            intra_sc[hd] = jnp.where(dist >= 0, jnp.exp(lgh * jnp.maximum(dist, 0).astype(F32)), 0.0)

    seqs = r_ref.shape[0]
    head_mask = [_lane_group_mask((chunk, w), HEAD_W, hd) for hd in range(RET_H)]
    rr = lax.broadcasted_iota(jnp.int32, (w, w), 0) // HEAD_W
    cc = lax.broadcasted_iota(jnp.int32, (w, w), 1) // HEAD_W
    block_diag = rr == cc
    outs = [None] * seqs

    def scan_step(sq):
        blk = r_ref[sq]
        q = blk[:, 0:w]
        k = blk[:, w:2 * w] * (RET_DK ** -0.5)
        v = blk[:, 2 * w:3 * w]
        qb = q.astype(BF16)
        kb = k.astype(BF16)
        vb = v.astype(BF16)
        s = jnp.where(c == 0, s0_ref[sq, 0], s_sc[sq])
        yield
        o = _dot(qb, s.astype(BF16)) * qdec_sc[...]
        for hd in range(RET_H):
            sc = _dot_nt(jnp.where(head_mask[hd], qb, jnp.zeros_like(qb)), kb) * intra_sc[hd]
            yield
            o = o + jnp.where(head_mask[hd], _dot(sc.astype(BF16), vb), 0.0)
        outs[sq] = o
        yield
        kd_t = jnp.transpose(k * kdec_sc[...]).astype(BF16)
        s_new = s * cdec_sc[...] + jnp.where(block_diag, _dot(kd_t, vb), 0.0)
        s_sc[sq] = s_new
        yield
        for hd in range(RET_H):
            st_ref[sq, 0, hd] = s_new[HEAD_W * hd:HEAD_W * (hd + 1), HEAD_W * hd:HEAD_W * (hd + 1)]

    def finish(sq):
        tot = outs[sq] + ob_sc[sq, c]
        mu = jnp.zeros_like(tot)
        for hd in range(RET_H):
            m1 = jnp.sum(jnp.where(head_mask[hd], tot, 0.0), axis=-1, keepdims=True) * (1.0 / HEAD_W)
            mu = jnp.where(head_mask[hd], m1, mu)
        yield
        xc = tot - mu
        var = jnp.zeros_like(tot)
        for hd in range(RET_H):
            v1 = jnp.sum(jnp.where(head_mask[hd], xc * xc, 0.0), axis=-1, keepdims=True) * (1.0 / HEAD_W)
            var = jnp.where(head_mask[hd], v1, var)
        yield
        g = r_ref[sq][:, 3 * w:4 * w]
        o_ref[sq] = (xc * lax.rsqrt(var + NORM_EPS) * (g * _sigmoid(g))).astype(o_ref.dtype)

    _round_robin([scan_step(sq) for sq in range(seqs)])

    @pl.when(p == 0)
    def _bwd():
        for sq in range(seqs):
            ob_sc[sq, n_chunks - 1 - c] = outs[sq]

    @pl.when(p == 1)
    def _fwd():
        _round_robin([finish(sq) for sq in range(seqs)])


def _retention(ret4, s0_bd, decay, *, batch, seq_len):
    n = seq_len // RET_CHUNK
    w = BRANCH_W
    seqs = RET_SEQS_PER_STEP if batch % RET_SEQS_PER_STEP == 0 else batch
    dlane = jnp.broadcast_to(jnp.repeat(decay, HEAD_W, axis=1)[:, None, :], (2, 8, w))
    dsub = jnp.broadcast_to(jnp.repeat(decay, HEAD_W, axis=1)[:, :, None], (2, w, w))
    dhead = jnp.broadcast_to(decay[:, :, None, None], (2, RET_H, 8, LANES))
    chunk_of = lambda p, c: c * p + (n - 1 - c) * (1 - p)
    o, st = pl.pallas_call(
        functools.partial(_ret_kernel, n_chunks=n),
        grid=(batch // seqs, 2, n),
        in_specs=[
            pl.BlockSpec((seqs, RET_CHUNK, 4 * w), lambda b, p, c: (b, chunk_of(p, c), 0)),
            pl.BlockSpec((seqs, 1, w, w), lambda b, p, c: (b, 1 - p, 0, 0)),
            pl.BlockSpec((1, 8, w), lambda b, p, c: (1 - p, 0, 0)),
            pl.BlockSpec((1, w, w), lambda b, p, c: (1 - p, 0, 0)),
            pl.BlockSpec((1, RET_H, 8, LANES), lambda b, p, c: (1 - p, 0, 0, 0)),
        ],
        out_specs=[
            pl.BlockSpec((seqs, RET_CHUNK, w), lambda b, p, c: (b, c * p, 0)),
            pl.BlockSpec((seqs, 1, RET_H, RET_DK, RET_DK), lambda b, p, c: (b, 1 - p, 0, 0, 0)),
        ],
        out_shape=[
            jax.ShapeDtypeStruct((batch, seq_len, w), BF16),
            jax.ShapeDtypeStruct((batch, 2, RET_H, RET_DK, RET_DK), F32),
        ],
        scratch_shapes=[
            pltpu.VMEM((seqs, w, w), F32),
            pltpu.VMEM((seqs, n, RET_CHUNK, w), F32),
            pltpu.VMEM((RET_CHUNK, w), F32),
            pltpu.VMEM((RET_CHUNK, w), F32),
            pltpu.VMEM((w, w), F32),
            pltpu.VMEM((RET_H, RET_CHUNK, RET_CHUNK), F32),
        ],
        compiler_params=_cparams(("arbitrary", "arbitrary", "arbitrary")),
        name="retention",
    )(ret4.reshape(batch, seq_len, 4 * w), s0_bd, dlane, dsub, dhead)
    return o.reshape(batch * seq_len, w), st


def _col_fold(acc, x, op):
    for j in range(x.shape[1] // LANES):
        acc = op(acc, x[:, LANES * j:LANES * (j + 1)])
    return acc


_ATTN_GROUPING = {"sink": (4, 1), "diff": (4, 2), "mla": (4, 1)}


def _attn_body(*refs, kind, n_chunks, kc, has_ctx, lam_init, static_chunks):
    it = iter(refs)
    q_ref, k_ref, v_ref = next(it), next(it), next(it)
    if has_ctx:
        if kind == "mla":
            cckv_ref, ckr_ref, wukv_ref = next(it), next(it), next(it)
        else:
            kx_ref, vx_ref = next(it), next(it)
    if kind == "sink":
        sink_ref = next(it)
    if kind == "diff":
        lam_ref, gd_ref = next(it), next(it)
    o_ref = next(it)
    nv, n_groups = _ATTN_GROUPING[kind]
    per_group = lambda: [next(it) for _ in range(n_groups)]
    s_all, m_all, l_all = per_group(), per_group(), per_group()
    qst_all = per_group() if kind != "mla" else None
    sx_all = per_group() if has_ctx else None
    if has_ctx and kind == "mla":
        kx_sc, vx_sc = next(it), next(it)

    tq = q_ref.shape[1]
    w = BRANCH_W
    rows = nv * tq
    unroll = ATTN_UNROLL if n_chunks % ATTN_UNROLL == 0 else 1
    q = q_ref[0]
    log2_scale = WIN_HD ** -0.5 * LOG2E if kind == "sink" else None

    if has_ctx and kind == "mla":
        kvc = _dot(cckv_ref[0].astype(BF16), wukv_ref[...])
        kr = ckr_ref[0]
        for hd in range(MLA_H):
            kx_sc[:, LANES * hd:LANES * (hd + 1)] = (kvc[:, LANES * hd:LANES * (hd + 1)] + kr).astype(BF16)
        vx_sc[...] = kvc[:, MLA_H * LANES:].astype(BF16)

    def rows_of(ref, ch):
        if static_chunks:
            return ref[0, ch * kc:(ch + 1) * kc, :]
        return ref[0, pl.ds(pl.multiple_of(ch * kc, kc), kc), :]

    def k_chunk(ch):
        return rows_of(k_ref, ch)

    def v_chunk(ch):
        return rows_of(v_ref, ch)

    def kx_tile():
        return kx_sc[...] if kind == "mla" else kx_ref[0]

    def vx_tile():
        return vx_sc[...] if kind == "mla" else vx_ref[0]

    def chunk_loop(body, carry):
        if static_chunks:
            for ch in range(n_chunks):
                carry = body(ch, carry)
                yield
        else:
            carry = lax.fori_loop(0, n_chunks, body, carry, unroll=unroll)
            yield
        return carry

    def weighted_values(make_weights, s_sc, sx_sc, n_rows):
        if not static_chunks:
            def body(ch, carry):
                return carry + _dot(make_weights(s_sc[ch]), v_chunk(ch))

            acc = lax.fori_loop(0, n_chunks, body, jnp.zeros((n_rows, w), F32), unroll=unroll)
            yield
            if has_ctx:
                acc = acc + _dot(make_weights(sx_sc[...]), vx_tile())
            return acc
        score_of = [functools.partial(lambda ch: s_sc[ch], ch) for ch in range(n_chunks)]
        value_of = [functools.partial(v_chunk, ch) for ch in range(n_chunks)]
        if has_ctx:
            score_of.append(lambda: sx_sc[...])
            value_of.append(vx_tile)
        acc = None
        ahead = make_weights(score_of[0]())
        for i in range(len(score_of)):
            cur = ahead
            if i + 1 < len(score_of):
                ahead = make_weights(score_of[i + 1]())
            term = _dot(cur, value_of[i]())
            acc = term if acc is None else acc + term
            yield
        return acc

    parts = []

    def group_body(grp):
        s_sc, m_sc, l_sc = s_all[grp], m_all[grp], l_all[grp]
        qst_sc = qst_all[grp] if kind != "mla" else None
        sx_sc = sx_all[grp] if has_ctx else None

        def scores(kt):
            if kind == "mla":
                heads = range(nv * grp, nv * (grp + 1))
                return jnp.concatenate(
                    [_dot_nt(q[:, LANES * hd:LANES * (hd + 1)], kt[:, LANES * hd:LANES * (hd + 1)])
                     for hd in heads], axis=0)
            return _dot_nt(qst_sc[...], kt)

        def probs(s):
            mb = m_sc[...]
            logit = (lambda x: x) if log2_scale is None else (lambda x: x * log2_scale)
            cols = [jnp.exp2(logit(s[:, LANES * j:LANES * (j + 1)]) - mb) for j in range(s.shape[1] // LANES)]
            tot = l_sc[...]
            for col in cols:
                tot = tot + col
            l_sc[...] = tot
            return jnp.concatenate(cols, axis=1)

        if kind != "mla":
            for v in range(nv):
                mask = (_lane_group_mask((tq, w), HEAD_W, v) if kind == "sink"
                        else _lane_group_mask((tq, w), DIFF_D, nv * grp + v))
                qst_sc[v * tq:(v + 1) * tq, :] = jnp.where(mask, q, jnp.zeros_like(q))
        yield

        m_sc[...] = jnp.full((rows, LANES), -jnp.inf, F32)

        def score_body(ch, carry):
            s = scores(k_chunk(ch))
            s_sc[ch] = s
            m_sc[...] = _col_fold(m_sc[...], s, jnp.maximum)
            return carry

        yield from chunk_loop(score_body, 0)
        if has_ctx:
            s = scores(kx_tile())
            sx_sc[...] = s
            m_sc[...] = _col_fold(m_sc[...], s, jnp.maximum)
        yield
        m2 = jnp.max(m_sc[...], axis=-1, keepdims=True)
        if log2_scale is not None:
            m2 = m2 * log2_scale
        if kind == "sink":
            sink2 = LOG2E * jnp.concatenate(
                [jnp.broadcast_to(sink_ref[v:v + 1, 0:1], (tq, 1)) for v in range(nv)], axis=0)
            m2 = jnp.maximum(m2, sink2)
        heads = range(nv * grp, nv * (grp + 1))
        m_sc[...] = jnp.broadcast_to(m2, (rows, LANES))
        l_sc[...] = jnp.zeros((rows, LANES), F32)
        yield

        if kind != "diff":
            acc = yield from weighted_values(lambda s: probs(s).astype(BF16), s_sc, sx_sc, rows)
            yield
            l = jnp.sum(l_sc[...], axis=-1, keepdims=True)
            if kind == "sink":
                l = l + jnp.exp2(sink2 - m2)
            pv = acc * (1.0 / l)
            for v, hd in enumerate(heads):
                parts.append(jnp.where(_lane_group_mask((tq, w), HEAD_W, hd), pv[v * tq:(v + 1) * tq], 0.0))
        else:
            def exp_body(ch, carry):
                s_sc[ch] = probs(s_sc[ch])
                return carry

            yield from chunk_loop(exp_body, 0)
            if has_ctx:
                sx_sc[...] = probs(sx_sc[...])
            yield
            dl = lam_ref[...]
            lam = (jnp.exp(jnp.sum(dl[0:1] * dl[1:2], axis=-1, keepdims=True))
                   - jnp.exp(jnp.sum(dl[2:3] * dl[3:4], axis=-1, keepdims=True)) + lam_init)
            l = jnp.sum(l_sc[...], axis=-1, keepdims=True)
            inv_l = 1.0 / l
            for v in range(2):
                first, second = slice(2 * v * tq, (2 * v + 1) * tq), slice((2 * v + 1) * tq, (2 * v + 2) * tq)
                l_sc[second, :] = jnp.broadcast_to(lam * l[first] * inv_l[second], (tq, LANES))
            yield

            def weights(pr):
                cols = []
                for j in range(pr.shape[1] // LANES):
                    x = pr[:, LANES * j:LANES * (j + 1)]
                    cols.append(jnp.concatenate(
                        [x[0:tq] - x[tq:2 * tq] * l_sc[tq:2 * tq, :],
                         x[2 * tq:3 * tq] - x[3 * tq:4 * tq] * l_sc[3 * tq:4 * tq, :]], axis=0))
                return jnp.concatenate(cols, axis=1).astype(BF16)

            acc = yield from weighted_values(weights, s_sc, sx_sc, 2 * tq)
            yield
            for v in range(2):
                head_out = acc[v * tq:(v + 1) * tq] * inv_l[2 * v * tq:(2 * v + 1) * tq]
                parts.append(jnp.where(_lane_group_mask((tq, w), HEAD_W, 2 * grp + v), head_out, 0.0))

    lag = (n_chunks + 2) if static_chunks else 0
    yield from _interleave([group_body(grp) for grp in range(n_groups)], [lag * grp for grp in range(n_groups)])
    out = parts[0]
    for part in parts[1:]:
        out = out + part
    if kind == "diff":
        ms = jnp.zeros_like(out)
        for hd in range(DIFF_H):
            mh = _lane_group_mask((tq, w), HEAD_W, hd)
            m1 = jnp.sum(jnp.where(mh, out * out, 0.0), axis=-1, keepdims=True) * (1.0 / HEAD_W)
            ms = jnp.where(mh, m1, ms)
        out = out * lax.rsqrt(ms + NORM_EPS) * gd_ref[...] * (1.0 - lam_init)
    o_ref[0] = out.astype(o_ref.dtype)


def _attn_multi_kernel(*refs, seqs, batched, **kw):
    n_io = len(batched)
    bodies = []
    for sq in range(seqs):
        view = [r.at[pl.ds(sq, 1)] if flag else r for r, flag in zip(refs[:n_io], batched)]
        view += [r.at[sq] for r in refs[n_io:]]
        bodies.append(_attn_body(*view, **kw))
    _round_robin(bodies)


def _attention(kind, q, k, v, *, ctx_args=(), params=(), lam_init=0.0):
    b, n, wq = q.shape
    nk = k.shape[1]
    tq = min(TQ_ATTN, n)
    kc = min(KC_ATTN, nk)
    n_chunks = nk // kc
    has_ctx = len(ctx_args) > 0
    w = BRANCH_W
    nv, n_groups = _ATTN_GROUPING[kind]
    rows = nv * tq
    static_chunks = n_chunks > 1
    seqs = ATTN_SEQS_PER_STEP if (n == tq and b % ATTN_SEQS_PER_STEP == 0) else 1
    per_b = lambda bb, i: (bb, 0, 0)
    shared = lambda bb, i: (0, 0)
    in_specs = [
        pl.BlockSpec((seqs, tq, wq), lambda bb, i: (bb, i, 0)),
        pl.BlockSpec((seqs, nk, k.shape[2]), per_b),
        pl.BlockSpec((seqs, nk, w), per_b),
    ]
    args = [q, k, v]
    batched = [True, True, True]
    for a in ctx_args:
        if a.ndim == 3:
            in_specs.append(pl.BlockSpec((seqs,) + a.shape[1:], per_b))
        else:
            in_specs.append(pl.BlockSpec(a.shape, shared))
        batched.append(a.ndim == 3)
        args.append(a)
    for a in params:
        in_specs.append(pl.BlockSpec(a.shape, shared))
        batched.append(False)
        args.append(a)
    batched.append(True)
    grouped = [(n_chunks, rows, kc, F32), (rows, LANES, F32), (rows, LANES, F32)]
    if kind != "mla":
        grouped.append((rows, w, BF16))
    single = []
    if has_ctx:
        m_ctx = ctx_args[0].shape[1]
        grouped.append((rows, m_ctx, F32))
        if kind == "mla":
            single = [(m_ctx, MLA_H * LANES, BF16), (m_ctx, w, BF16)]
    scratch = [sh for sh in grouped for _ in range(n_groups)] + single
    scratch_bytes = sum(seqs * math.prod(sh[:-1]) * jnp.dtype(sh[-1]).itemsize for sh in scratch)
    return pl.pallas_call(
        functools.partial(_attn_multi_kernel, seqs=seqs, batched=tuple(batched), kind=kind, n_chunks=n_chunks,
                          kc=kc, has_ctx=has_ctx, lam_init=lam_init, static_chunks=static_chunks),
        grid=(b // seqs, n // tq),
        in_specs=in_specs,
        out_specs=pl.BlockSpec((seqs, tq, w), lambda bb, i: (bb, i, 0)),
        out_shape=jax.ShapeDtypeStruct((b, n, w), BF16),
        scratch_shapes=[pltpu.VMEM((seqs,) + sh[:-1], sh[-1]) for sh in scratch],
        compiler_params=_cparams(("arbitrary", "arbitrary"), vmem=max(VMEM_LIMIT, scratch_bytes + ATTN_VMEM_EXTRA)),
        name="attn_" + kind + ("_lat" if has_ctx else "_ctx"),
    )(*args)


def _win_kernel(q_ref, kp_ref, kc_ref, kn_ref, vp_ref, vc_ref, vn_ref, kx_ref, vx_ref, sink_ref, o_ref):
    n_steps = pl.num_programs(1)
    step = pl.program_id(1)
    tq = WINDOW
    nblk = q_ref.shape[1] // tq
    w = BRANCH_W
    nh = 4
    rows = nh * tq
    scale = WIN_HD ** -0.5
    head_mask = [_lane_group_mask((tq, w), HEAD_W, hd) for hd in range(nh)]
    ii = lax.broadcasted_iota(jnp.int32, (rows, tq), 0) % tq
    jj = lax.broadcasted_iota(jnp.int32, (rows, tq), 1)
    sink_col = jnp.concatenate([jnp.broadcast_to(sink_ref[hd:hd + 1, 0:1], (tq, 1)) for hd in range(nh)], axis=0)
    kx = kx_ref[0]
    vx = vx_ref[0]

    def rows_of(ref, blk):
        return ref[0, blk * tq:(blk + 1) * tq, :]

    def block(blk):
        q = rows_of(q_ref, blk)
        qst = jnp.concatenate([jnp.where(head_mask[hd], q, jnp.zeros_like(q)) for hd in range(nh)], axis=0)
        kp, vp = (kp_ref[0], vp_ref[0]) if blk == 0 else (rows_of(kc_ref, blk - 1), rows_of(vc_ref, blk - 1))
        kn, vn = (kn_ref[0], vn_ref[0]) if blk == nblk - 1 else (rows_of(kc_ref, blk + 1), rows_of(vc_ref, blk + 1))
        mask_prev = jj >= ii
        mask_next = jj <= ii
        if blk == 0:
            mask_prev = mask_prev & (step > 0)
        if blk == nblk - 1:
            mask_next = mask_next & (step < n_steps - 1)
        yield
        sp = jnp.where(mask_prev, _dot_nt(qst, kp) * scale, NEG_INF)
        sc = _dot_nt(qst, rows_of(kc_ref, blk)) * scale
        sn = jnp.where(mask_next, _dot_nt(qst, kn) * scale, NEG_INF)
        sx = _dot_nt(qst, kx) * scale
        yield
        m = jnp.maximum(jnp.maximum(jnp.max(sp, axis=-1, keepdims=True), jnp.max(sc, axis=-1, keepdims=True)),
                        jnp.maximum(jnp.max(sn, axis=-1, keepdims=True), jnp.max(sx, axis=-1, keepdims=True)))
        m = jnp.maximum(m, sink_col)
        yield
        pp, pc, pn, px = (jnp.exp(s - m) for s in (sp, sc, sn, sx))
        l = (jnp.sum(pp, axis=-1, keepdims=True) + jnp.sum(pc, axis=-1, keepdims=True)
             + jnp.sum(pn, axis=-1, keepdims=True) + jnp.sum(px, axis=-1, keepdims=True) + jnp.exp(sink_col - m))
        yield
        pv = (_dot(pp.astype(BF16), vp) + _dot(pc.astype(BF16), rows_of(vc_ref, blk))
              + _dot(pn.astype(BF16), vn) + _dot(px.astype(BF16), vx)) * (1.0 / l)
        yield
        out = jnp.zeros((tq, w), F32)
        for hd in range(nh):
            out = out + jnp.where(head_mask[hd], pv[hd * tq:(hd + 1) * tq], 0.0)
        o_ref[0, blk * tq:(blk + 1) * tq, :] = out.astype(o_ref.dtype)

    _round_robin([block(blk) for blk in range(nblk)])


def _window_attention(q, k, v, kx, vx, sink_tile):
    b, n, w = q.shape
    tq = WINDOW
    span = WIN_BLOCKS_PER_STEP * tq
    nb = n // tq
    m_ctx = kx.shape[1]
    edge = lambda f: pl.BlockSpec((1, tq, w), f)
    prev = lambda bb, i: (bb, jnp.maximum(i * WIN_BLOCKS_PER_STEP - 1, 0), 0)
    nxt = lambda bb, i: (bb, jnp.minimum((i + 1) * WIN_BLOCKS_PER_STEP, nb - 1), 0)
    mid = pl.BlockSpec((1, span, w), lambda bb, i: (bb, i, 0))
    per_b = lambda bb, i: (bb, 0, 0)
    return pl.pallas_call(
        _win_kernel,
        grid=(b, n // span),
        in_specs=[mid, edge(prev), mid, edge(nxt), edge(prev), mid, edge(nxt),
                  pl.BlockSpec((1, m_ctx, w), per_b), pl.BlockSpec((1, m_ctx, w), per_b),
                  pl.BlockSpec(sink_tile.shape, lambda bb, i: (0, 0))],
        out_specs=mid,
        out_shape=jax.ShapeDtypeStruct((b, n, w), BF16),
        compiler_params=_cparams(("arbitrary", "arbitrary")),
        name="attn_window",
    )(q, k, k, k, v, v, v, kx, vx, sink_tile)


def _post_kernel(x_ref, mod_ref, g_ref, o0_ref, o1_ref, o2_ref, o3_ref,
                 wg_ref, bg_ref, wb_ref, wo_ref, out_ref, mix_sc):
    d = D_MODEL
    x = x_ref[...]
    mod = mod_ref[0]
    h = _rmsnorm(x, g_ref[...]) * (1.0 + mod[1:2]) + mod[0:1]
    hb = h.astype(BF16)
    cw = 256
    for j in range(d // cw):
        mixed = None
        for nbr, o_ref in enumerate((o0_ref, o1_ref, o2_ref, o3_ref)):
            lo = nbr * d + j * cw
            gate = _sigmoid(_dot(hb, wg_ref[:, lo:lo + cw]) + bg_ref[:, lo:lo + cw])
            term = gate * _dot(o_ref[...], wb_ref[nbr, :, j * cw:(j + 1) * cw])
            mixed = term if mixed is None else mixed + term
        mix_sc[:, j * cw:(j + 1) * cw] = mixed.astype(BF16)
    out_ref[...] = x + mod[2:3] * _dot(mix_sc[...], wo_ref[...])


def _post(x2d, mod, g_mix, outs, w_gate, b_gate, w_branch, w_out, *, seq_len, layer):
    t, d = x2d.shape
    tm = TM_POST
    assert seq_len % tm == 0 or (tm % seq_len == 0 and mod.shape[0] == 1)
    tiles_per_seq = max(seq_len // tm, 1)
    const2 = lambda i: (0, 0)
    row = lambda i: (i, 0)
    return pl.pallas_call(
        _post_kernel,
        grid=(t // tm,),
        in_specs=[
            pl.BlockSpec((tm, d), row),
            _mod_spec(mod, tiles_per_seq),
            pl.BlockSpec((1, d), const2),
        ] + [pl.BlockSpec((tm, BRANCH_W), row)] * 4 + [
            _resident(w_gate.shape, layer),
            pl.BlockSpec((1, 4 * d), const2),
            _resident(w_branch.shape, layer),
            _resident(w_out.shape, layer),
        ],
        out_specs=pl.BlockSpec((tm, d), row),
        out_shape=jax.ShapeDtypeStruct((t, d), F32),
        scratch_shapes=[pltpu.VMEM((tm, d), BF16)],
        compiler_params=_cparams(("arbitrary",), fuse=[False] * 7 + [True, False, True, True]),
        name="post",
    )(x2d, mod, g_mix.reshape(1, d), *outs, w_gate, b_gate.reshape(1, 4 * d), w_branch, w_out)


def _ffn_kernel(*refs, tiles_per_seq, final):
    halo = tiles_per_seq > 1
    it = iter(refs)
    x_ref = next(it)
    if halo:
        xp_ref, xn_ref = next(it), next(it)
    mod_ref, g_ref, wup_ref, bup_ref, wcv_ref, bcv_ref, wdn_ref = (next(it) for _ in range(7))
    gf_ref = next(it) if final else None
    out_ref = next(it)
    h_sc = next(it)

    tm = x_ref.shape[0]
    hr = CONV_HALO
    n_ff = D_FF // FF_CHUNK
    mod = mod_ref[0]
    g = g_ref[...]

    def norm_mod(rows):
        return (_rmsnorm(rows, g) * (1.0 + mod[4:5]) + mod[3:4]).astype(BF16)

    x = x_ref[...]
    h_sc[hr:hr + tm, :] = norm_mod(x)
    if halo:
        h_sc[0:hr, :] = norm_mod(xp_ref[...])
        h_sc[hr + tm:, :] = norm_mod(xn_ref[...])
        i = pl.program_id(0)
        keep_lo = jnp.where((i % tiles_per_seq) == 0, 0.0, 1.0)
        keep_hi = jnp.where((i % tiles_per_seq) == tiles_per_seq - 1, 0.0, 1.0)
        rid = lax.broadcasted_iota(jnp.int32, (8, 1), 0)
        edge_lo = jnp.where(rid == 7, keep_lo, 1.0)
        edge_hi = jnp.where(rid == 0, keep_hi, 1.0)

    def up(ch):
        lo = ch * FF_CHUNK
        if halo:
            u = _dot(h_sc[...], wup_ref[:, lo:lo + FF_CHUNK]) + bup_ref[:, lo:lo + FF_CHUNK]
            return jnp.concatenate([u[:hr - 8], u[hr - 8:hr] * edge_lo, u[hr:hr + tm],
                                    u[hr + tm:hr + tm + 8] * edge_hi, u[hr + tm + 8:]], axis=0)
        pad = jnp.zeros((hr, FF_CHUNK), F32)
        u = _dot(h_sc[hr:hr + tm, :], wup_ref[:, lo:lo + FF_CHUNK]) + bup_ref[:, lo:lo + FF_CHUNK]
        return jnp.concatenate([pad, u, pad], axis=0)

    def conv(ch, u):
        lo = ch * FF_CHUNK
        wc = wcv_ref[:, lo:lo + FF_CHUNK]
        before = pltpu.roll(u, 1, axis=0)[hr:hr + tm]
        after = pltpu.roll(u, u.shape[0] - 1, axis=0)[hr:hr + tm]
        return (bcv_ref[:, lo:lo + FF_CHUNK] + wc[0:1] * before + wc[1:2] * u[hr:hr + tm] + wc[2:3] * after)

    u_val, u_gate = up(0), up(n_ff)
    act = None
    for c in range(n_ff + 1):
        if c + 1 < n_ff:
            u_val_next, u_gate_next = up(c + 1), up(c + 1 + n_ff)
        if c >= 1:
            term = _dot(act, wdn_ref[(c - 1) * FF_CHUNK:c * FF_CHUNK, :])
            acc = term if c == 1 else acc + term
        if c < n_ff:
            a = conv(c, u_val)
            gg = conv(c + n_ff, u_gate)
            act = (gg * _sigmoid(gg) * a).astype(BF16)
            u_val, u_gate = u_val_next, u_gate_next
    y = x + mod[5:6] * acc
    if final:
        y = _rmsnorm(y, gf_ref[...])
    out_ref[...] = y


def _ffn(x2d, mod, g_ffn, w_up, b_up, w_conv, b_conv, w_down, g_final, *, seq_len, layer):
    t, d = x2d.shape
    tm = min(TM_FFN, seq_len)
    hr = CONV_HALO
    tiles_per_seq = seq_len // tm
    halo = tiles_per_seq > 1
    final = g_final is not None
    hb = tm // hr
    n_hblocks = t // hr
    const2 = lambda i: (0, 0)
    row = lambda i: (i, 0)
    in_specs = [pl.BlockSpec((tm, d), row)]
    args = [x2d]
    if halo:
        in_specs += [pl.BlockSpec((hr, d), lambda i: (jnp.maximum(i * hb - 1, 0), 0)),
                     pl.BlockSpec((hr, d), lambda i: (jnp.minimum((i + 1) * hb, n_hblocks - 1), 0))]
        args += [x2d, x2d]
    in_specs += [
        _mod_spec(mod, tiles_per_seq),
        pl.BlockSpec((1, d), const2),
        _resident(w_up.shape, layer),
        pl.BlockSpec(b_up.shape, const2),
        pl.BlockSpec(w_conv.shape, const2),
        pl.BlockSpec(b_conv.shape, const2),
        _resident(w_down.shape, layer),
    ]
    args += [mod, g_ffn.reshape(1, d), w_up, b_up, w_conv, b_conv, w_down]
    if final:
        in_specs.append(pl.BlockSpec((1, d), const2))
        args.append(g_final.reshape(1, d))
    return pl.pallas_call(
        functools.partial(_ffn_kernel, tiles_per_seq=tiles_per_seq, final=final),
        grid=(t // tm,),
        in_specs=in_specs,
        out_specs=pl.BlockSpec((tm, d), row),
        out_shape=jax.ShapeDtypeStruct((t, d), F32),
        scratch_shapes=[
            pltpu.VMEM((tm + 2 * hr, d), BF16),
        ],
        compiler_params=_cparams(("arbitrary",), fuse=[a is w_up or a is w_down for a in args]),
        name=("ffn_final" if final else "ffn") + ("_lat" if halo else "_ctx"),
    )(*args)


def _rope_tables(n_tok):
    rows = n_tok // GRID_W

    def axis(n_pos, half):
        freqs = ROPE_BASE ** (-jnp.arange(half, dtype=F32) / half)
        ang = jnp.arange(n_pos).astype(F32)[:, None] * freqs[None, :]
        return jnp.cos(ang), jnp.sin(ang)

    def unit(width):
        half = width // 4
        cr, sr = (jnp.repeat(t, GRID_W, axis=0) for t in axis(rows, half))
        cc, sc = (jnp.tile(t, (rows, 1)) for t in axis(GRID_W, half))
        z = jnp.zeros_like(sr)
        return (jnp.concatenate([cr, cr, cc, cc], axis=1),
                jnp.concatenate([-sr, z, -sc, z], axis=1),
                jnp.concatenate([z, sr, z, sc], axis=1))

    a = [jnp.tile(t, (1, 2)) for t in unit(64)]
    b = [jnp.tile(t, (1, 4)) for t in unit(32)]
    cu = unit(32)
    ones = jnp.ones((n_tok, 64), F32)
    z64 = jnp.zeros((n_tok, 64), F32)
    z32 = jnp.zeros((n_tok, 32), F32)
    c = [jnp.concatenate([ones, cu[0], ones[:, :32]], axis=1),
         jnp.concatenate([z64, cu[1], z32], axis=1),
         jnp.concatenate([z64, cu[2], z32], axis=1)]
    return jnp.stack(a + b + c, axis=0)


def _dup_groups(w):
    g0, g1 = w[..., :HEAD_W], w[..., HEAD_W:]
    return jnp.concatenate([g0, g0, g1, g1], axis=-1)


def _prep_weights(p):
    depth = p["w_in"].shape[0]
    cuts = np.cumsum((0,) + IN_SPLITS)
    parts = [p["w_in"][:, :, cuts[i]:cuts[i + 1]] for i in range(len(IN_SPLITS))]
    rq, rk, rv, rg, wq, wk, wv, dq, dk, dv, mcq, mckv, mkr = parts
    mkr128 = jnp.pad(mkr, ((0, 0), (0, 0), (MLA_NOPE, LANES - MLA_NOPE - MLA_ROPE)))
    w_in_p = jnp.concatenate([rq, rk, rv, rg, wq, _dup_groups(wk), _dup_groups(wv), dq, dk, dv,
                              mcq, mckv, mkr128], axis=2).astype(BF16)
    w_uq = p["w_mla_uq"].reshape(depth, MLA_QRANK, MLA_H, MLA_NOPE + MLA_ROPE)
    w_uq_p = jnp.pad(w_uq, ((0, 0), (0, 0), (0, 0), (0, LANES - MLA_NOPE - MLA_ROPE)))
    w_ukv = p["w_mla_ukv"].reshape(depth, MLA_KVRANK, MLA_H, MLA_NOPE + MLA_V)
    w_uk = jnp.pad(w_ukv[..., :MLA_NOPE], ((0, 0), (0, 0), (0, 0), (0, LANES - MLA_NOPE)))
    w_uv = w_ukv[..., MLA_NOPE:].reshape(depth, MLA_KVRANK, MLA_H * MLA_V)
    sink = jnp.broadcast_to(p["win_sink"].astype(F32)[:, :, None], (depth, 4, LANES))
    return dict(
        w_in_p=w_in_p,
        w_uq_p=w_uq_p.reshape(depth, MLA_QRANK, MLA_H * LANES).astype(BF16),
        w_ukv_p=jnp.concatenate([w_uk.reshape(depth, MLA_KVRANK, MLA_H * LANES), w_uv], axis=2).astype(BF16),
        w_gate=p["w_gate"].astype(BF16),
        w_branch=p["w_branch"].astype(BF16),
        w_out=p["w_out"].astype(BF16),
        w_up=p["w_up"].astype(BF16),
        w_down=p["w_down"].astype(BF16),
        sink_tile8=jnp.pad(sink, ((0, 0), (0, 4), (0, 0))),
        g_diff4=jnp.tile(p["g_diff"], (1, DIFF_H)).reshape(depth, 1, BRANCH_W),
    )


def _block_diag_state(s):
    b = s.shape[0]
    eye = jnp.eye(RET_H, dtype=s.dtype)
    bd = s[:, :, :, :, None, :] * eye[None, None, :, None, :, None]
    return bd.reshape(b, 2, RET_H * RET_DK, RET_H * RET_DK)


def _mod_rows(mods_l, start, count):
    m = mods_l[start:start + count].reshape(count, 6, D_MODEL)
    return jnp.pad(m, ((0, 0), (0, 2), (0, 0)))


def kernel(x_prompt, x_sample, state_ret, cache_win_k, cache_win_v, cache_diff_k, cache_diff_v, cache_mla_ckv, cache_mla_krope, c, c_ctx, w_mod, b_mod, g_mix, w_in, ret_decay, win_sink, diff_lambda, g_diff, g_mla_q, w_mla_uq, g_mla_kv, w_mla_ukv, w_branch, w_gate, b_gate, w_out, g_ffn, w_up, b_up, w_conv, b_conv, w_down, g_final):
    d = D_MODEL
    bc, lc, _ = x_prompt.shape
    bl, ll, _ = x_sample.shape
    m_ctx = cache_win_k.shape[2]

    cond_rows = jnp.zeros((8, d), F32).at[0].set(c_ctx).at[1:1 + bl].set(c)
    mods = _modulation(cond_rows, w_mod, b_mod)
    tabs = _rope_tables(ll)

    xp = x_prompt.reshape(bc * lc, d)
    xs = x_sample.reshape(bl * ll, d)
    produced = [[] for _ in range(7)]
    p = _prep_weights(dict(w_in=w_in, w_mla_uq=w_mla_uq, w_mla_ukv=w_mla_ukv, w_gate=w_gate, w_branch=w_branch,
                           w_out=w_out, w_up=w_up, w_down=w_down, win_sink=win_sink, g_diff=g_diff))
    zero_state = jnp.zeros((bc, 2, BRANCH_W, BRANCH_W), F32)
    for li in range(DEPTH):
        sink_tile8, g_diff4 = p["sink_tile8"][li], p["g_diff4"][li]
        ffn_small = (b_up[li].reshape(1, -1), w_conv[li], b_conv[li].reshape(1, -1))
        lam_init = 0.8 - 0.6 * math.exp(-0.3 * li)
        mod_c = _mod_rows(mods[li], 0, 1)
        mod_l = _mod_rows(mods[li], 1, bl)
        final_g = g_final if li == DEPTH - 1 else None

        (ret4, wq, wke, wve, dq, dk, dv, mq, mk, mv,
         wke32, wve32, dk32, dv32, ckv32, mkr32) = _pre(
            xp, mod_c, g_mix[li], p["w_in_p"], p["w_uq_p"], p["w_ukv_p"], g_mla_q[li], g_mla_kv[li], None,
            seq_len=lc, ctx=True, layer=li)
        o_ret, st = _retention(ret4, zero_state, ret_decay[li], batch=bc, seq_len=lc)
        r3 = lambda a: a.reshape(bc, lc, a.shape[-1])
        o_win = _attention("sink", r3(wq), r3(wke), r3(wve), params=(sink_tile8,))
        o_diff = _attention("diff", r3(dq), r3(dk), r3(dv), params=(diff_lambda[li], g_diff4), lam_init=lam_init)
        o_mla = _attention("mla", r3(mq), r3(mk), r3(mv))
        f2 = lambda a: a.reshape(bc * lc, BRANCH_W)
        xp = _post(xp, mod_c, g_mix[li], (o_ret, f2(o_win), f2(o_diff), f2(o_mla)),
                   p["w_gate"], b_gate[li], p["w_branch"], p["w_out"], seq_len=lc, layer=li)
        xp = _ffn(xp, mod_c, g_ffn[li], p["w_up"], *ffn_small, p["w_down"], final_g, seq_len=lc, layer=li)
        undup = lambda a: a.reshape(bc, lc, 2, 2, HEAD_W)[:, :, :, 0, :]
        produced[0].append(st)
        produced[1].append(undup(wke32))
        produced[2].append(undup(wve32))
        produced[3].append(dk32.reshape(bc, lc, DIFF_H, 2 * DIFF_D))
        produced[4].append(dv32.reshape(bc, lc, DIFF_H, 2 * DIFF_D))
        produced[5].append(ckv32.reshape(bc, lc, MLA_KVRANK))
        produced[6].append(mkr32.reshape(bc, lc, LANES)[:, :, MLA_NOPE:MLA_NOPE + MLA_ROPE])

        (ret4, wq, wke, wve, dq, dk, dv, mq, mk, mv) = _pre(
            xs, mod_l, g_mix[li], p["w_in_p"], p["w_uq_p"], p["w_ukv_p"], g_mla_q[li], g_mla_kv[li], tabs,
            seq_len=ll, ctx=False, layer=li)
        o_ret, _ = _retention(ret4, _block_diag_state(state_ret[:, li]), ret_decay[li], batch=bl, seq_len=ll)
        r3 = lambda a: a.reshape(bl, ll, a.shape[-1])
        kx_win = _dup_groups(cache_win_k[:, li].reshape(bl, m_ctx, 2 * HEAD_W)).astype(BF16)
        vx_win = _dup_groups(cache_win_v[:, li].reshape(bl, m_ctx, 2 * HEAD_W)).astype(BF16)
        o_win = _window_attention(r3(wq), r3(wke), r3(wve), kx_win, vx_win, sink_tile8)
        kx_diff = cache_diff_k[:, li].reshape(bl, m_ctx, BRANCH_W).astype(BF16)
        vx_diff = cache_diff_v[:, li].reshape(bl, m_ctx, BRANCH_W).astype(BF16)
        o_diff = _attention("diff", r3(dq), r3(dk), r3(dv), ctx_args=(kx_diff, vx_diff),
                            params=(diff_lambda[li], g_diff4), lam_init=lam_init)
        kr128 = jnp.pad(cache_mla_krope[:, li], ((0, 0), (0, 0), (MLA_NOPE, LANES - MLA_NOPE - MLA_ROPE)))
        o_mla = _attention("mla", r3(mq), r3(mk), r3(mv), ctx_args=(cache_mla_ckv[:, li], kr128, p["w_ukv_p"][li]))
        f2 = lambda a: a.reshape(bl * ll, BRANCH_W)
        xs = _post(xs, mod_l, g_mix[li], (o_ret, f2(o_win), f2(o_diff), f2(o_mla)),
                   p["w_gate"], b_gate[li], p["w_branch"], p["w_out"], seq_len=ll, layer=li)
        xs = _ffn(xs, mod_l, g_ffn[li], p["w_up"], *ffn_small, p["w_down"], final_g, seq_len=ll, layer=li)

    y_prompt = xp.reshape(bc, lc, d)
    y_sample = xs.reshape(bl, ll, d)
    stack = lambda lst: jnp.stack(lst, axis=1)
    return (y_prompt, y_sample, stack(produced[0]),
            stack(produced[1]), stack(produced[2]), stack(produced[3]), stack(produced[4]),
            stack(produced[5]), stack(produced[6]))
```
